```python
import math
import jax, jax.numpy as jnp
from jax import lax
import numpy as np

D_MODEL = 2048
BATCH = 16
SEQ = 256
DEPTH = 1
DEC_BATCH = 2
DEC_SEQ = 2048
PAST_LEN = 512

GRID_W = 64
CHUNK = 128
N_DIR = 2
MIX_M = D_MODEL // 2
M_HEADS = 4
M_DV = MIX_M // M_HEADS
M_DQK = M_DV // 2
MIX_S = D_MODEL - MIX_M
S_HEADDIM = 64
S_HEADS = MIX_S // S_HEADDIM
S_STATE = 128
S_GROUPS = 4
CONV_K = 3
D_FF = 4 * D_MODEL
EPS = 1e-6
IN_SIZES = (M_HEADS * M_DQK, M_HEADS * M_DQK, MIX_M, MIX_M, N_DIR * M_HEADS, N_DIR * M_HEADS,
            MIX_S, MIX_S + 2 * S_GROUPS * S_STATE, N_DIR * S_HEADS)
D_IN_PROJ = sum(IN_SIZES)

kernel_name = "bidir_mlstm_ssd_hybrid_diffusion_step"


def _split_points(sizes):
    pts, acc = [], 0
    for s in sizes[:-1]:
        acc += s
        pts.append(acc)
    return pts


def rmsnorm(x, g):
    xf = x.astype(jnp.float32)
    y = xf * lax.rsqrt(jnp.mean(xf * xf, -1, keepdims=True) + EPS)
    return (y * g.astype(jnp.float32)).astype(x.dtype)


def modulation(cond, w_mod, b_mod):
    m = jax.nn.silu(cond) @ w_mod + b_mod
    return jnp.split(m[:, None, :], 6, axis=-1)


def grid_dwconv(u, w, b, rows):
    bn, L, cn = u.shape
    img = u.reshape(bn, rows, L // rows, cn)
    out = lax.conv_general_dilated(img, w[:, :, None, :].astype(u.dtype), (1, 1), 'SAME',
                                   dimension_numbers=('NHWC', 'HWIO', 'NHWC'), feature_group_count=cn)
    return out.reshape(bn, L, cn) + b


def mlstm_chunkwise(q, k, v, i_pre, logf, c0, n0, m0):
    f32 = jnp.float32
    bn, H, L, dqk = q.shape
    nc = L // CHUNK

    def to_chunks(t):
        return jnp.moveaxis(t.astype(f32).reshape(bn, H, nc, CHUNK, *t.shape[3:]), 2, 0)

    qc, kc, vc, ic, fc = map(to_chunks, (q * (dqk ** -0.5), k, v, i_pre, logf))
    causal = jnp.tril(jnp.ones((CHUNK, CHUNK), bool))

    def step(carry, inp):
        C, n, m = carry
        qb, kb, vb, ib, fb = inp
        b = jnp.cumsum(fb, -1)
        dmat = jnp.where(causal, b[..., :, None] - b[..., None, :] + ib[..., None, :], -jnp.inf)
        m_inter = b + m[..., None]
        m_t = jnp.maximum(m_inter, jnp.max(dmat, -1))
        w_inter = jnp.exp(m_inter - m_t)
        s = jnp.einsum('bhtd,bhsd->bhts', qb, kb) * jnp.exp(dmat - m_t[..., None])
        num = w_inter[..., None] * jnp.einsum('bhtd,bhde->bhte', qb, C) + jnp.einsum('bhts,bhse->bhte', s, vb)
        den = w_inter * jnp.einsum('bhtd,bhd->bht', qb, n) + jnp.sum(s, -1)
        h = num / jnp.maximum(jnp.abs(den), jnp.exp(-m_t))[..., None]
        b_last = b[..., -1]
        g = b_last[..., None] - b + ib
        m_new = jnp.maximum(b_last + m, jnp.max(g, -1))
        decay = jnp.exp(b_last + m - m_new)
        wk = jnp.exp(g - m_new[..., None])
        C_new = decay[..., None, None] * C + jnp.einsum('bhs,bhsd,bhse->bhde', wk, kb, vb)
        n_new = decay[..., None] * n + jnp.einsum('bhs,bhsd->bhd', wk, kb)
        return (C_new, n_new, m_new), h

    (C, n, m), hs = lax.scan(step, (c0.astype(f32), n0.astype(f32), m0.astype(f32)), (qc, kc, vc, ic, fc))
    h = jnp.moveaxis(hs, 0, 2).reshape(bn, H, L, -1)
    return h, C, n, m


def ssd_chunkwise(x, dt, a, Bm, Cm, s0):
    f32 = jnp.float32
    bn, L, H, P = x.shape
    rep = H // S_GROUPS
    nc = L // CHUNK

    def to_chunks(t):
        return jnp.moveaxis(t.astype(f32).reshape(bn, nc, CHUNK, *t.shape[2:]), 1, 0)

    xg = to_chunks(x.reshape(bn, L, S_GROUPS, rep, P))
    dtg = dt.astype(f32).reshape(bn, L, S_GROUPS, rep)
    lag = to_chunks(dtg * a.astype(f32).reshape(S_GROUPS, rep))
    dtc = to_chunks(dtg)
    Bc, Cc = to_chunks(Bm), to_chunks(Cm)
    causal = jnp.tril(jnp.ones((CHUNK, CHUNK), bool))[None, :, :, None, None]

    def step(S, inp):
        xb, lab, dtb, Bb, Cb = inp
        cs = jnp.cumsum(lab, axis=1)
        seg = jnp.exp(jnp.where(causal, cs[:, :, None] - cs[:, None, :], -jnp.inf))
        cb = jnp.einsum('btgn,bsgn->btsg', Cb, Bb)
        mix = cb[..., None] * seg * dtb[:, None]
        y = jnp.einsum('btsgr,bsgrp->btgrp', mix, xb) + jnp.einsum('btgn,bgrpn,btgr->btgrp', Cb, S, jnp.exp(cs))
        tot = cs[:, -1]
        wk = jnp.exp(tot[:, None] - cs) * dtb
        S_new = jnp.exp(tot)[..., None, None] * S + jnp.einsum('bsgr,bsgrp,bsgn->bgrpn', wk, xb, Bb)
        return S_new, y

    S, ys = lax.scan(step, s0.astype(f32).reshape(bn, S_GROUPS, rep, P, S_STATE), (xg, lag, dtc, Bc, Cc))
    y = jnp.moveaxis(ys, 0, 1).reshape(bn, L, H, P)
    return y, S.reshape(bn, H, P, S_STATE)


def mixer(u, rows, state, p):
    f32 = jnp.float32
    bn, L, _ = u.shape
    c0, n0, m0, s0 = state
    q, k, v, o, ig, fg, z, xbc, dt = jnp.split(u @ p['w_in'], _split_points(IN_SIZES), axis=-1)

    def heads(t, d):
        return t.reshape(bn, L, M_HEADS, d).transpose(0, 2, 1, 3)
    qh, kh, vh = heads(q, M_DQK), heads(k, M_DQK), heads(v, M_DV)
    ig = (ig.astype(f32).reshape(bn, L, N_DIR, M_HEADS) + p['b_igate']).transpose(2, 0, 3, 1)
    lf = jax.nn.log_sigmoid(fg.astype(f32).reshape(bn, L, N_DIR, M_HEADS) + p['b_fgate']).transpose(2, 0, 3, 1)
    fl = lambda t: jnp.flip(t, axis=2)
    h_f, cf, nf, mf = mlstm_chunkwise(qh, kh, vh, ig[0], lf[0], c0[:, 0], n0[:, 0], m0[:, 0])
    h_b, cbk, nbk, mbk = mlstm_chunkwise(fl(qh), fl(kh), fl(vh), fl(ig[1]), fl(lf[1]), c0[:, 1], n0[:, 1], m0[:, 1])
    h_m = (h_f + fl(h_b)).transpose(0, 2, 1, 3).astype(u.dtype)
    h_m = rmsnorm(h_m, p['g_mlstm_norm'].reshape(M_HEADS, M_DV)).reshape(bn, L, MIX_M) * jax.nn.sigmoid(o)

    xbc = jax.nn.silu(grid_dwconv(xbc, p['conv_w'], p['conv_b'], rows))
    xs, Bs, Cs = jnp.split(xbc, [MIX_S, MIX_S + S_GROUPS * S_STATE], axis=-1)
    xs = xs.reshape(bn, L, S_HEADS, S_HEADDIM)
    Bs = Bs.reshape(bn, L, S_GROUPS, S_STATE)
    Cs = Cs.reshape(bn, L, S_GROUPS, S_STATE)
    dt = jax.nn.softplus(dt.astype(f32).reshape(bn, L, N_DIR, S_HEADS) + p['dt_bias'])
    a = -jnp.exp(p['a_log'].astype(f32))
    f1 = lambda t: jnp.flip(t, axis=1)
    y_f, sf = ssd_chunkwise(xs, dt[:, :, 0], a[0], Bs, Cs, s0[:, 0])
    y_b, sbk = ssd_chunkwise(f1(xs), f1(dt[:, :, 1]), a[1], f1(Bs), f1(Cs), s0[:, 1])
    y_s = (y_f + f1(y_b) + p['d_skip'][:, None].astype(f32) * xs.astype(f32)).astype(u.dtype)
    y_s = rmsnorm(y_s.reshape(bn, L, MIX_S) * jax.nn.silu(z), p['g_ssd_norm'])

    out = jnp.concatenate([h_m, y_s], axis=-1) @ p['w_out']
    new_state = (jnp.stack([cf, cbk], 1), jnp.stack([nf, nbk], 1), jnp.stack([mf, mbk], 1), jnp.stack([sf, sbk], 1))
    return out, new_state


def block(x, cond, rows, state, p):
    sh1, sc1, g1, sh2, sc2, g2 = modulation(cond, p['w_mod'], p['b_mod'])
    u = rmsnorm(x, p['g_pre_mix']) * (1.0 + sc1) + sh1
    mix, new_state = mixer(u, rows, state, p)
    x = x + g1 * rmsnorm(mix, p['g_post_mix'])
    u = rmsnorm(x, p['g_pre_mlp']) * (1.0 + sc2) + sh2
    hdn = jnp.square(jax.nn.relu(u @ p['w_mlp_in']))
    x = x + g2 * rmsnorm(hdn @ p['w_mlp_out'], p['g_post_mlp'])
    return x, new_state


def setup_inputs(seed: int = 0) -> dict:
    key = jax.random.key(seed)
    ks = iter(jax.random.split(key, 40))
    f32 = jnp.float32
    D = D_MODEL

    def nrm(shape, s):
        return s * jax.random.normal(next(ks), shape, f32)

    def gain(shape):
        return 1.0 + nrm(shape, 0.02)

    dt0 = jnp.exp(jax.random.uniform(next(ks), (DEPTH, N_DIR, S_HEADS), f32, math.log(1e-3), math.log(1e-1)))
    dt_bias = dt0 + jnp.log(-jnp.expm1(-dt0))
    a_log = jnp.log(jax.random.uniform(next(ks), (DEPTH, N_DIR, S_HEADS), f32, 1.0, 16.0))
    b_fgate = jax.random.uniform(next(ks), (DEPTH, N_DIR, M_HEADS), f32, 3.0, 6.0)
    return {
        "x_prompt": nrm((BATCH, SEQ, D), 1.0),
        "x_sample": nrm((DEC_BATCH, DEC_SEQ, D), 1.0),
        "state_mlstm_c": nrm((DEC_BATCH, DEPTH, N_DIR, M_HEADS, M_DQK, M_DV), 0.1),
        "state_mlstm_n": nrm((DEC_BATCH, DEPTH, N_DIR, M_HEADS, M_DQK), 0.5),
        "state_mlstm_m": nrm((DEC_BATCH, DEPTH, N_DIR, M_HEADS), 1.0),
        "state_ssd": nrm((DEC_BATCH, DEPTH, N_DIR, S_HEADS, S_HEADDIM, S_STATE), 0.1),
        "c": nrm((DEC_BATCH, D), 1.0),
        "c_ctx": nrm((D,), 1.0),
        "w_mod": nrm((DEPTH, D, 6 * D), 0.2 * D ** -0.5),
        "b_mod": nrm((DEPTH, 6 * D), 0.01),
        "g_pre_mix": gain((DEPTH, D)),
        "g_post_mix": gain((DEPTH, D)),
        "w_in": nrm((DEPTH, D, D_IN_PROJ), D ** -0.5),
        "b_igate": nrm((DEPTH, N_DIR, M_HEADS), 0.1),
        "b_fgate": b_fgate,
        "conv_w": nrm((DEPTH, CONV_K, CONV_K, MIX_S + 2 * S_GROUPS * S_STATE), (CONV_K * CONV_K) ** -0.5),
        "conv_b": nrm((DEPTH, MIX_S + 2 * S_GROUPS * S_STATE), 0.01),
        "dt_bias": dt_bias,
        "a_log": a_log,
        "d_skip": gain((DEPTH, S_HEADS)),
        "g_mlstm_norm": gain((DEPTH, MIX_M)),
        "g_ssd_norm": gain((DEPTH, MIX_S)),
        "w_out": nrm((DEPTH, D, D), D ** -0.5),
        "g_pre_mlp": gain((DEPTH, D)),
        "g_post_mlp": gain((DEPTH, D)),
        "w_mlp_in": nrm((DEPTH, D, D_FF), D ** -0.5),
        "w_mlp_out": nrm((DEPTH, D_FF, D), D_FF ** -0.5),
    }


def reference(x_prompt, x_sample, state_mlstm_c, state_mlstm_n, state_mlstm_m, state_ssd, c, c_ctx,
              w_mod, b_mod, g_pre_mix, g_post_mix, w_in, b_igate, b_fgate, conv_w, conv_b, dt_bias, a_log,
              d_skip, g_mlstm_norm, g_ssd_norm, w_out, g_pre_mlp, g_post_mlp, w_mlp_in, w_mlp_out):
    f32 = jnp.float32
    bp = x_prompt.shape[0]
    rows = x_sample.shape[1] // GRID_W
    zero_state = (jnp.zeros((bp, N_DIR, M_HEADS, M_DQK, M_DV), f32),
                  jnp.zeros((bp, N_DIR, M_HEADS, M_DQK), f32),
                  jnp.zeros((bp, N_DIR, M_HEADS), f32),
                  jnp.zeros((bp, N_DIR, S_HEADS, S_HEADDIM, S_STATE), f32))
    hp, hs = x_prompt, x_sample
    nc_list, nn_list, nm_list, ns_list = [], [], [], []
    for l in range(DEPTH):
        p = dict(w_mod=w_mod[l], b_mod=b_mod[l], g_pre_mix=g_pre_mix[l], g_post_mix=g_post_mix[l],
                 w_in=w_in[l], b_igate=b_igate[l], b_fgate=b_fgate[l], conv_w=conv_w[l], conv_b=conv_b[l],
                 dt_bias=dt_bias[l], a_log=a_log[l], d_skip=d_skip[l], g_mlstm_norm=g_mlstm_norm[l],
                 g_ssd_norm=g_ssd_norm[l], w_out=w_out[l], g_pre_mlp=g_pre_mlp[l], g_post_mlp=g_post_mlp[l],
                 w_mlp_in=w_mlp_in[l], w_mlp_out=w_mlp_out[l])
        hp, st = block(hp, c_ctx[None, :], 1, zero_state, p)
        nc_list.append(st[0]); nn_list.append(st[1]); nm_list.append(st[2]); ns_list.append(st[3])
        cache_l = (state_mlstm_c[:, l], state_mlstm_n[:, l], state_mlstm_m[:, l], state_ssd[:, l])
        hs, _ = block(hs, c, rows, cache_l, p)
    new_c = jnp.stack(nc_list, 1).astype(x_prompt.dtype)
    new_n = jnp.stack(nn_list, 1).astype(x_prompt.dtype)
    new_m = jnp.stack(nm_list, 1).astype(x_prompt.dtype)
    new_s = jnp.stack(ns_list, 1).astype(x_prompt.dtype)
    return (hp, hs, new_c, new_n, new_m, new_s)
```

```python
import functools

import jax
import jax.numpy as jnp
from jax import lax
from jax.experimental import pallas as pl
from jax.experimental.pallas import tpu as pltpu

F32 = jnp.float32
BF16 = jnp.bfloat16

D_MODEL = 2048
CHUNK = 128
N_DIR = 2
M_HEADS = 4
M_DQK = 128
M_DV = 256
MIX_M = M_HEADS * M_DV
S_HEADS = 16
S_HEADDIM = 64
S_STATE = 128
S_GROUPS = 4
S_REP = S_HEADS // S_GROUPS
MIX_S = S_HEADS * S_HEADDIM
XBC = MIX_S + 2 * S_GROUPS * S_STATE
D_FF = 4 * D_MODEL
GRID_W = 64
EPS = 1e-6

P_Q = 0
P_K = M_HEADS * M_DQK
P_V = 2 * M_HEADS * M_DQK
P_O = P_V + MIX_M
P_Z = P_O + MIX_M
P_XBC = P_Z + MIX_S
P_GATE = P_XBC + XBC
GATE_W = 128
P_WIDTH = P_GATE + GATE_W
G_I = 0
G_F = N_DIR * M_HEADS
G_DT = 2 * N_DIR * M_HEADS
N_GATES = G_DT + N_DIR * S_HEADS

VMEM_LIMIT = 48 * 1024 * 1024

_NT = (((1,), (1,)), ((), ()))
_TN = (((0,), (0,)), ((), ()))


def _params(sem, limit=VMEM_LIMIT):
    return pltpu.CompilerParams(dimension_semantics=sem, vmem_limit_bytes=limit)


def _silu(x):
    return x / (1.0 + jnp.exp(-x))


def _sigmoid(x):
    return 1.0 / (1.0 + jnp.exp(-x))


def _softplus(x):
    return jnp.maximum(x, 0.0) + jnp.log1p(jnp.exp(-jnp.abs(x)))


def _rms(x):
    return x * lax.rsqrt(jnp.mean(x * x, axis=-1, keepdims=True) + EPS)


def _dot(a, b, dims=None, precision=None):
    if dims is None:
        dims = (((a.ndim - 1,), (0,)), ((), ()))
    return lax.dot_general(a, b, dims, precision=precision, preferred_element_type=F32)


def _mod_kernel(c_ref, w_ref, b_ref, o_ref):
    a = _silu(c_ref[...]).astype(BF16)
    o_ref[...] = _dot(a, w_ref[...].astype(BF16)) + b_ref[...]


def _modulation(cond8, w_mod, b_mod):
    n = w_mod.shape[1]
    tn = 1024
    return pl.pallas_call(
        _mod_kernel,
        grid=(n // tn,),
        in_specs=[pl.BlockSpec((8, D_MODEL), lambda j: (0, 0)),
                  pl.BlockSpec((D_MODEL, tn), lambda j: (0, j)),
                  pl.BlockSpec((1, tn), lambda j: (0, j))],
        out_specs=pl.BlockSpec((8, tn), lambda j: (0, j)),
        out_shape=jax.ShapeDtypeStruct((8, n), F32),
        compiler_params=_params(("parallel",)),
        name="modulation",
    )(cond8, w_mod, b_mod)


IN_TM = 1024
IN_TN = 896
IN_SUB = 128


def _inproj_kernel(x_ref, sc_ref, sh_ref, g_ref, w_ref, wgt_ref, o_ref, gt_ref, u_ref):
    j = pl.program_id(1)

    @pl.when(j == 0)
    def _():
        def body(r, carry):
            rows = pl.ds(pl.multiple_of(r * IN_SUB, IN_SUB), IN_SUB)
            y = _rms(x_ref[rows, :]) * g_ref[...]
            u_ref[rows, :] = (y * (1.0 + sc_ref[...]) + sh_ref[...]).astype(BF16)
            return carry
        lax.fori_loop(0, IN_TM // IN_SUB, body, 0)

    o_ref[...] = _dot(u_ref[...], w_ref[...])

    @pl.when(j == pl.num_programs(1) - 1)
    def _():
        gt_ref[...] = _dot(wgt_ref[...], u_ref[...], _NT)


def _in_proj(x2d, sc, sh, g, w_pack, wg_t, rows_per_mod):
    t = x2d.shape[0]
    tiles_per_mod = rows_per_mod // IN_TM
    return pl.pallas_call(
        _inproj_kernel,
        grid=(t // IN_TM, P_WIDTH // IN_TN),
        in_specs=[pl.BlockSpec((IN_TM, D_MODEL), lambda i, j: (i, 0)),
                  pl.BlockSpec((None, 1, D_MODEL), lambda i, j: (i // tiles_per_mod, 0, 0)),
                  pl.BlockSpec((None, 1, D_MODEL), lambda i, j: (i // tiles_per_mod, 0, 0)),
                  pl.BlockSpec((1, D_MODEL), lambda i, j: (0, 0)),
                  pl.BlockSpec((D_MODEL, IN_TN), lambda i, j: (0, j)),
                  pl.BlockSpec((GATE_W, D_MODEL), lambda i, j: (0, 0))],
        out_specs=[pl.BlockSpec((IN_TM, IN_TN), lambda i, j: (i, j)),
                   pl.BlockSpec((GATE_W, IN_TM), lambda i, j: (0, i))],
        out_shape=[jax.ShapeDtypeStruct((t, P_WIDTH), F32),
                   jax.ShapeDtypeStruct((GATE_W, t), F32)],
        scratch_shapes=[pltpu.VMEM((IN_TM, D_MODEL), BF16)],
        compiler_params=_params(("parallel", "arbitrary")),
        name="in_proj",
    )(x2d, sc, sh, g, w_pack, wg_t)


def _conv_kernel(x_ref, w_ref, b_ref, o_ref, *, seq, width):
    x = x_ref[...]
    t = lax.broadcasted_iota(jnp.int32, x.shape, 0)
    c = jnp.bitwise_and(t, width - 1)
    xl = jnp.where(c >= 1, pltpu.roll(x, 1, 0), 0.0)
    xr = jnp.where(c <= width - 2, pltpu.roll(x, seq - 1, 0), 0.0)

    def taps(di):
        return w_ref[3 * di:3 * di + 1, :] * xl + w_ref[3 * di + 1:3 * di + 2, :] * x \
            + w_ref[3 * di + 2:3 * di + 3, :] * xr

    out = taps(1) + b_ref[...]
    if seq > width:
        out = out + jnp.where(t >= width, pltpu.roll(taps(0), width, 0), 0.0)
        out = out + jnp.where(t < seq - width, pltpu.roll(taps(2), seq - width, 0), 0.0)
    o_ref[...] = _silu(out)


def _conv(p, conv_w9, conv_b, nb, seq, width, cn):
    col0 = P_XBC // cn
    return pl.pallas_call(
        functools.partial(_conv_kernel, seq=seq, width=width),
        grid=(nb, XBC // cn),
        in_specs=[pl.BlockSpec((seq, cn), lambda b, j: (b, col0 + j)),
                  pl.BlockSpec((9, cn), lambda b, j: (0, j)),
                  pl.BlockSpec((1, cn), lambda b, j: (0, j))],
        out_specs=pl.BlockSpec((seq, cn), lambda b, j: (b, j)),
        out_shape=jax.ShapeDtypeStruct((nb * seq, XBC), F32),
        compiler_params=_params(("parallel", "parallel")),
        name="grid_conv",
    )(p, conv_w9, conv_b)


def _tri_masks():
    r = lax.broadcasted_iota(jnp.int32, (CHUNK, CHUNK), 0)
    c = lax.broadcasted_iota(jnp.int32, (CHUNK, CHUNK), 1)
    return r >= c, r <= c


def _cumsums(col_vals, row_vals, direction, lower, upper):
    lo = lower.astype(F32)
    up = upper.astype(F32)
    hi = lax.Precision.HIGHEST
    if direction == 0:
        return _dot(lo, col_vals, precision=hi), _dot(row_vals, up, precision=hi)
    return _dot(up, col_vals, precision=hi), _dot(row_vals, lo, precision=hi)


def _mlstm_kernel(*refs, nc, zero_init):
    (qf, kf, vf, of, gcf, gtf, qb, kb, vb, ob, gcb, gtb, bias_c, bias_r, gn) = refs[:15]
    pos = 15
    if not zero_init:
        c0, n0, m0 = refs[pos:pos + 3]
        pos += 3
    hm_out, c_out, n_out, m_out, c_s, n_s, m_s, hpart = refs[pos:]
    s = pl.program_id(1)
    half = nc // 2

    @pl.when(s == 0)
    def _():
        if zero_init:
            c_s[...] = jnp.zeros_like(c_s)
            n_s[...] = jnp.zeros_like(n_s)
            m_s[...] = jnp.zeros_like(m_s)
        else:
            c_s[...] = c0[...]
            n_s[...] = n0[...]
            m_s[...] = m0[...]

    lower, upper = _tri_masks()
    per_dir = ((qf, kf, vf, of, gcf, gtf, lower, s), (qb, kb, vb, ob, gcb, gtb, upper, nc - 1 - s))
    for d, (q_ref, k_ref, v_ref, o_ref, gc_ref, gt_ref, mask, chunk) in enumerate(per_dir):
        gc = gc_ref[...] + bias_c[...]
        gt = gt_ref[...] + bias_r[...]
        lf_c = jnp.minimum(gc, 0.0) - jnp.log1p(jnp.exp(-jnp.abs(gc)))
        lf_t = jnp.minimum(gt, 0.0) - jnp.log1p(jnp.exp(-jnp.abs(gt)))
        b_c, b_t = _cumsums(lf_c, lf_t, d, lower, upper)
        last = CHUNK - 1 if d == 0 else 0
        rows = pl.ds(pl.multiple_of(chunk * CHUNK, CHUNK), CHUNK)
        for h in range(M_HEADS):
            gi = G_I + d * M_HEADS + h
            gf = G_F + d * M_HEADS + h
            sd = d * M_HEADS + h
            i_col, b_col = gc[:, gi:gi + 1], b_c[:, gf:gf + 1]
            i_row, b_row = gt[gi:gi + 1, :], b_t[gf:gf + 1, :]
            m_prev = m_s[sd:sd + 1, 0:1]
            qh = q_ref[:, h * M_DQK:(h + 1) * M_DQK] * (M_DQK ** -0.5)
            kh = k_ref[:, h * M_DQK:(h + 1) * M_DQK]
            vh = v_ref[:, h * M_DV:(h + 1) * M_DV].astype(BF16)
            cmat = c_s[d, h]
            nrow = n_s[sd:sd + 1, :]

            dmat = jnp.where(mask, b_col - b_row + i_row, -jnp.inf)
            m_inter = b_col + m_prev
            m_t = jnp.maximum(m_inter, jnp.max(dmat, axis=-1, keepdims=True))
            w_inter = jnp.exp(m_inter - m_t)
            qh16 = qh.astype(BF16)
            sc = _dot(qh16, kh.astype(BF16), _NT) * jnp.exp(dmat - m_t)
            num = w_inter * _dot(qh16, cmat.astype(BF16)) + _dot(sc.astype(BF16), vh)
            den = w_inter * jnp.sum(qh * nrow, axis=-1, keepdims=True) + jnp.sum(sc, axis=-1, keepdims=True)
            hval = num / jnp.maximum(jnp.abs(den), jnp.exp(-m_t))

            b_last = b_col[last:last + 1, :]
            g_row = b_last - b_row + i_row
            g_col = b_last - b_col + i_col
            m_new = jnp.maximum(b_last + m_prev, jnp.max(g_row, axis=-1, keepdims=True))
            decay = jnp.exp(b_last + m_prev - m_new)
            kw = kh * jnp.exp(g_col - m_new)
            c_s[d, h] = decay * cmat + _dot(kw.astype(BF16), vh, _TN)
            n_s[sd:sd + 1, :] = decay * nrow + jnp.sum(kw, axis=0, keepdims=True)
            m_s[sd:sd + 1, :] = jnp.broadcast_to(m_new, (1, M_DQK))

            cols = slice(h * M_DV, (h + 1) * M_DV)

            @pl.when(s < half)
            def _():
                hpart[rows, cols] = hval

            @pl.when(s >= half)
            def _():
                tot = hval + hpart[rows, cols]
                hm_out[rows, cols] = (_rms(tot) * gn[:, cols] * _sigmoid(o_ref[:, cols])).astype(BF16)

    @pl.when(s == nc - 1)
    def _():
        c_out[...] = c_s[...]
        n_out[...] = n_s[...]
        m_out[...] = m_s[...]


def _mlstm(p, gt, bias_c, bias_r, g_norm, state, nb, nc):
    zero_init = state is None
    seq = nc * CHUNK

    def fwd(cb):
        return lambda b, s: (b * nc + s, cb)

    def bwd(cb):
        return lambda b, s: (b * nc + nc - 1 - s, cb)

    def stream(mk):
        return [pl.BlockSpec((CHUNK, M_HEADS * M_DQK), mk(P_Q // (M_HEADS * M_DQK))),
                pl.BlockSpec((CHUNK, M_HEADS * M_DQK), mk(P_K // (M_HEADS * M_DQK))),
                pl.BlockSpec((CHUNK, MIX_M), mk(P_V // MIX_M)),
                pl.BlockSpec((CHUNK, MIX_M), mk(P_O // MIX_M)),
                pl.BlockSpec((CHUNK, GATE_W), mk(P_GATE // GATE_W))]

    in_specs = (stream(fwd) + [pl.BlockSpec((GATE_W, CHUNK), lambda b, s: (0, b * nc + s))]
                + stream(bwd) + [pl.BlockSpec((GATE_W, CHUNK), lambda b, s: (0, b * nc + nc - 1 - s))]
                + [pl.BlockSpec((1, GATE_W), lambda b, s: (0, 0)),
                   pl.BlockSpec((GATE_W, 1), lambda b, s: (0, 0)),
                   pl.BlockSpec((1, MIX_M), lambda b, s: (0, 0))])
    args = [p] * 5 + [gt] + [p] * 5 + [gt, bias_c, bias_r, g_norm]
    state_specs = [pl.BlockSpec((None, N_DIR, M_HEADS, M_DQK, M_DV), lambda b, s: (b, 0, 0, 0, 0)),
                   pl.BlockSpec((None, N_DIR * M_HEADS, M_DQK), lambda b, s: (b, 0, 0)),
                   pl.BlockSpec((None, N_DIR * M_HEADS, M_DQK), lambda b, s: (b, 0, 0))]
    if not zero_init:
        in_specs += state_specs
        args += list(state)
    return pl.pallas_call(
        functools.partial(_mlstm_kernel, nc=nc, zero_init=zero_init),
        grid=(nb, nc),
        in_specs=in_specs,
        out_specs=[pl.BlockSpec((None, seq, MIX_M), lambda b, s: (b, 0, 0))] + state_specs,
        out_shape=[jax.ShapeDtypeStruct((nb, seq, MIX_M), BF16),
                   jax.ShapeDtypeStruct((nb, N_DIR, M_HEADS, M_DQK, M_DV), F32),
                   jax.ShapeDtypeStruct((nb, N_DIR * M_HEADS, M_DQK), F32),
                   jax.ShapeDtypeStruct((nb, N_DIR * M_HEADS, M_DQK), F32)],
        scratch_shapes=[pltpu.VMEM((N_DIR, M_HEADS, M_DQK, M_DV), F32),
                        pltpu.VMEM((N_DIR * M_HEADS, M_DQK), F32),
                        pltpu.VMEM((N_DIR * M_HEADS, M_DQK), F32),
                        pltpu.VMEM((seq, MIX_M), F32)],
        compiler_params=_params(("parallel", "arbitrary")),
        name="mlstm",
    )(*args)


GROUP_P = S_REP * S_HEADDIM


def _ssd_kernel(*refs, nc, zero_init):
    (xf, bf, cf, zf, gcf, gtf, xb, bb, cb, zb, gcb, gtb, bias_c, bias_r, alog_c, alog_r, dskip, gn) = refs[:18]
    pos = 18
    if not zero_init:
        s0 = refs[pos]
        pos += 1
    ys_out, s_out, s_s, ypart, ybuf, xwbuf = refs[pos:]
    s = pl.program_id(1)
    half = nc // 2

    @pl.when(s == 0)
    def _():
        if zero_init:
            s_s[...] = jnp.zeros_like(s_s)
        else:
            s_s[...] = s0[...]

    lower, upper = _tri_masks()
    per_dir = ((xf, bf, cf, zf, gcf, gtf, lower, s), (xb, bb, cb, zb, gcb, gtb, upper, nc - 1 - s))
    for d, (x_ref, b_ref, c_ref, z_ref, gc_ref, gt_ref, mask, chunk) in enumerate(per_dir):
        dt_c = _softplus(gc_ref[...] + bias_c[...])
        dt_t = _softplus(gt_ref[...] + bias_r[...])
        la_c = dt_c * -jnp.exp(alog_c[...])
        la_t = dt_t * -jnp.exp(alog_r[...])
        cs_c, cs_t = _cumsums(la_c, la_t, d, lower, upper)
        last = CHUNK - 1 if d == 0 else 0
        rows = pl.ds(pl.multiple_of(chunk * CHUNK, CHUNK), CHUNK)
        for g in range(S_GROUPS):
            cg = c_ref[:, g * S_STATE:(g + 1) * S_STATE].astype(BF16)
            bg = b_ref[:, g * S_STATE:(g + 1) * S_STATE].astype(BF16)
            cbm = _dot(cg, bg, _NT)
            state = s_s[d, g]
            c_state = _dot(cg, state.astype(BF16), _NT)
            gcols = slice(g * GROUP_P, (g + 1) * GROUP_P)
            for r in range(S_REP):
                gi = G_DT + d * S_HEADS + g * S_REP + r
                hc = slice(g * GROUP_P + r * S_HEADDIM, g * GROUP_P + (r + 1) * S_HEADDIM)
                lc = slice(r * S_HEADDIM, (r + 1) * S_HEADDIM)
                dt_col, cs_col = dt_c[:, gi:gi + 1], cs_c[:, gi:gi + 1]
                dt_row, cs_row = dt_t[gi:gi + 1, :], cs_t[gi:gi + 1, :]
                xh = x_ref[:, hc]
                seg = jnp.exp(jnp.where(mask, cs_col - cs_row, -jnp.inf))
                mix = cbm * seg * dt_row
                ybuf[:, hc] = _dot(mix.astype(BF16), xh.astype(BF16)) + c_state[:, lc] * jnp.exp(cs_col)
                tot = cs_col[last:last + 1, :]
                xwbuf[:, hc] = xh * (jnp.exp(tot - cs_col) * dt_col)
                s_s[d, g, lc, :] = state[lc, :] * jnp.exp(tot)
            s_s[d, g] = s_s[d, g] + _dot(xwbuf[:, gcols].astype(BF16), bg, _TN)

        @pl.when(s < half)
        def _():
            ypart[rows, :] = ybuf[...]

        @pl.when(s >= half)
        def _():
            y = ybuf[...] + ypart[rows, :] + dskip[...] * x_ref[...]
            ys_out[rows, :] = (_rms(y * _silu(z_ref[...])) * gn[...]).astype(BF16)

    @pl.when(s == nc - 1)
    def _():
        s_out[...] = s_s[...]


def _ssd(p, xbc, gt, bias_c, bias_r, alog_c, alog_r, dskip, g_norm, state, nb, nc):
    zero_init = state is None
    seq = nc * CHUNK
    bc_w = S_GROUPS * S_STATE

    def fwd(cb):
        return lambda b, s: (b * nc + s, cb)

    def bwd(cb):
        return lambda b, s: (b * nc + nc - 1 - s, cb)

    def stream(mk):
        return [pl.BlockSpec((CHUNK, MIX_S), mk(0)),
                pl.BlockSpec((CHUNK, bc_w), mk(MIX_S // bc_w)),
                pl.BlockSpec((CHUNK, bc_w), mk(MIX_S // bc_w + 1)),
                pl.BlockSpec((CHUNK, MIX_S), mk(P_Z // MIX_S)),
                pl.BlockSpec((CHUNK, GATE_W), mk(P_GATE // GATE_W))]

    const = lambda b, s: (0, 0)
    in_specs = (stream(fwd) + [pl.BlockSpec((GATE_W, CHUNK), lambda b, s: (0, b * nc + s))]
                + stream(bwd) + [pl.BlockSpec((GATE_W, CHUNK), lambda b, s: (0, b * nc + nc - 1 - s))]
                + [pl.BlockSpec((1, GATE_W), const), pl.BlockSpec((GATE_W, 1), const),
                   pl.BlockSpec((1, GATE_W), const), pl.BlockSpec((GATE_W, 1), const),
                   pl.BlockSpec((1, MIX_S), const), pl.BlockSpec((1, MIX_S), const)])
    args = [xbc, xbc, xbc, p, p, gt] * 2 + [bias_c, bias_r, alog_c, alog_r, dskip, g_norm]
    state_spec = pl.BlockSpec((None, N_DIR, S_GROUPS, GROUP_P, S_STATE), lambda b, s: (b, 0, 0, 0, 0))
    if not zero_init:
        in_specs.append(state_spec)
        args.append(state)
    return pl.pallas_call(
        functools.partial(_ssd_kernel, nc=nc, zero_init=zero_init),
        grid=(nb, nc),
        in_specs=in_specs,
        out_specs=[pl.BlockSpec((None, seq, MIX_S), lambda b, s: (b, 0, 0)), state_spec],
        out_shape=[jax.ShapeDtypeStruct((nb, seq, MIX_S), BF16),
                   jax.ShapeDtypeStruct((nb, N_DIR, S_GROUPS, GROUP_P, S_STATE), F32)],
        scratch_shapes=[pltpu.VMEM((N_DIR, S_GROUPS, GROUP_P, S_STATE), F32),
                        pltpu.VMEM((seq, MIX_S), F32),
                        pltpu.VMEM((CHUNK, MIX_S), F32),
                        pltpu.VMEM((CHUNK, MIX_S), F32)],
        compiler_params=_params(("parallel", "arbitrary")),
        name="ssd",
    )(*args)


OUT_TM = 256


def _outproj_kernel(hm_ref, ys_ref, w_ref, x_ref, g1_ref, gpost_ref, gpre_ref, sc_ref, sh_ref, x1_ref, u2_ref):
    mix = _dot(hm_ref[...], w_ref[0:MIX_M, :]) + _dot(ys_ref[...], w_ref[MIX_M:, :])
    x1 = x_ref[...] + g1_ref[...] * (_rms(mix) * gpost_ref[...])
    x1_ref[...] = x1
    u2_ref[...] = (_rms(x1) * gpre_ref[...] * (1.0 + sc_ref[...]) + sh_ref[...]).astype(BF16)


def _out_proj(hm, ys, w_out16, x2d, g1, gpost, gpre, sc2, sh2, rows_per_mod):
    t = x2d.shape[0]
    tiles_per_mod = rows_per_mod // OUT_TM
    row = lambda i: (i, 0)
    const = lambda i: (0, 0)
    mod = pl.BlockSpec((None, 1, D_MODEL), lambda i: (i // tiles_per_mod, 0, 0))
    return pl.pallas_call(
        _outproj_kernel,
        grid=(t // OUT_TM,),
        in_specs=[pl.BlockSpec((OUT_TM, MIX_M), row), pl.BlockSpec((OUT_TM, MIX_S), row),
                  pl.BlockSpec((D_MODEL, D_MODEL), const), pl.BlockSpec((OUT_TM, D_MODEL), row),
                  mod, pl.BlockSpec((1, D_MODEL), const), pl.BlockSpec((1, D_MODEL), const), mod, mod],
        out_specs=[pl.BlockSpec((OUT_TM, D_MODEL), row), pl.BlockSpec((OUT_TM, D_MODEL), row)],
        out_shape=[jax.ShapeDtypeStruct((t, D_MODEL), F32), jax.ShapeDtypeStruct((t, D_MODEL), BF16)],
        compiler_params=_params(("parallel",)),
        name="out_proj",
    )(hm, ys, w_out16, x2d, g1, gpost, gpre, sc2, sh2)


MLP_TM = 512
MLP_TH = 512
MLP_SUB = 256


def _mlp_kernel(u_ref, w1_ref, w2_ref, x1_ref, g2_ref, gpost_ref, o_ref):
    j = pl.program_id(1)

    @pl.when(j == 0)
    def _():
        o_ref[...] = jnp.zeros_like(o_ref)

    def body(r, carry):
        rows = pl.ds(pl.multiple_of(r * MLP_SUB, MLP_SUB), MLP_SUB)
        hid = jnp.square(jnp.maximum(_dot(u_ref[rows, :], w1_ref[...]), 0.0)).astype(BF16)
        o_ref[rows, :] += _dot(hid, w2_ref[...])
        return carry
    lax.fori_loop(0, MLP_TM // MLP_SUB, body, 0)

    @pl.when(j == pl.num_programs(1) - 1)
    def _():
        def fin(r, carry):
            rows = pl.ds(pl.multiple_of(r * MLP_SUB, MLP_SUB), MLP_SUB)
            o_ref[rows, :] = x1_ref[rows, :] + g2_ref[...] * (_rms(o_ref[rows, :]) * gpost_ref[...])
            return carry
        lax.fori_loop(0, MLP_TM // MLP_SUB, fin, 0)


def _mlp(u2, w1, w2, x1, g2, gpost, rows_per_mod):
    t = u2.shape[0]
    tiles_per_mod = rows_per_mod // MLP_TM
    return pl.pallas_call(
        _mlp_kernel,
        grid=(t // MLP_TM, D_FF // MLP_TH),
        in_specs=[pl.BlockSpec((MLP_TM, D_MODEL), lambda i, j: (i, 0)),
                  pl.BlockSpec((D_MODEL, MLP_TH), lambda i, j: (0, j)),
                  pl.BlockSpec((MLP_TH, D_MODEL), lambda i, j: (j, 0)),
                  pl.BlockSpec((MLP_TM, D_MODEL), lambda i, j: (i, 0)),
                  pl.BlockSpec((None, 1, D_MODEL), lambda i, j: (i // tiles_per_mod, 0, 0)),
                  pl.BlockSpec((1, D_MODEL), lambda i, j: (0, 0))],
        out_specs=pl.BlockSpec((MLP_TM, D_MODEL), lambda i, j: (i, 0)),
        out_shape=jax.ShapeDtypeStruct((t, D_MODEL), F32),
        compiler_params=_params(("parallel", "arbitrary"), 56 * 1024 * 1024),
        name="mlp",
    )(u2, w1, w2, x1, g2, gpost)


def _gate_layout(i_vals, f_vals, dt_vals):
    v = jnp.concatenate([i_vals.reshape(-1), f_vals.reshape(-1), dt_vals.reshape(-1)]).astype(F32)
    v = jnp.pad(v, (0, GATE_W - N_GATES))
    return v.reshape(1, GATE_W), v.reshape(GATE_W, 1)


def _block(x, mods, state, weights, width):
    nb, seq, _ = x.shape
    nc = seq // CHUNK
    t = nb * seq
    x2d = x.reshape(t, D_MODEL)
    sh1, sc1, g1, sh2, sc2, g2 = mods
    rows_per_mod = t // sh1.shape[0]
    w = weights

    p, gt = _in_proj(x2d, sc1, sh1, w["g_pre_mix"], w["w_pack"], w["wg_t"], rows_per_mod)
    xbc = _conv(p, w["conv_w9"], w["conv_b"], nb, seq, width, 512)

    if state is None:
        m_state = s_state = None
    else:
        c0, n0, m0, s0 = state
        m_state = (c0, n0.reshape(nb, N_DIR * M_HEADS, M_DQK),
                   jnp.broadcast_to(m0.reshape(nb, N_DIR * M_HEADS, 1), (nb, N_DIR * M_HEADS, M_DQK)))
        s_state = s0.reshape(nb, N_DIR, S_GROUPS, GROUP_P, S_STATE)
    hm, c_new, n_new, m_new = _mlstm(p, gt, w["gate_bias_c"], w["gate_bias_r"], w["g_mlstm_norm"],
                                     m_state, nb, nc)
    ys, s_new = _ssd(p, xbc, gt, w["gate_bias_c"], w["gate_bias_r"], w["alog_c"], w["alog_r"],
                     w["dskip"], w["g_ssd_norm"], s_state, nb, nc)

    x1, u2 = _out_proj(hm.reshape(t, MIX_M), ys.reshape(t, MIX_S), w["w_out"], x2d, g1,
                       w["g_post_mix"], w["g_pre_mlp"], sc2, sh2, rows_per_mod)
    y = _mlp(u2, w["w_mlp_in"], w["w_mlp_out"], x1, g2, w["g_post_mlp"], rows_per_mod)
    new_state = (c_new.reshape(nb, 1, N_DIR, M_HEADS, M_DQK, M_DV),
                 n_new.reshape(nb, 1, N_DIR, M_HEADS, M_DQK),
                 m_new[:, :, 0].reshape(nb, 1, N_DIR, M_HEADS),
                 s_new.reshape(nb, 1, N_DIR, S_HEADS, S_HEADDIM, S_STATE))
    return y.reshape(nb, seq, D_MODEL), new_state


def kernel(x_prompt, x_sample, state_mlstm_c, state_mlstm_n, state_mlstm_m, state_ssd, c, c_ctx, w_mod, b_mod,
           g_pre_mix, g_post_mix, w_in, b_igate, b_fgate, conv_w, conv_b, dt_bias, a_log, d_skip, g_mlstm_norm,
           g_ssd_norm, w_out, g_pre_mlp, g_post_mlp, w_mlp_in, w_mlp_out):
    assert w_mod.shape[0] == 1, "one layer"
    nb_s = x_sample.shape[0]

    cond8 = jnp.zeros((8, D_MODEL), F32).at[0].set(c_ctx).at[1:1 + nb_s].set(c)
    mod = _modulation(cond8, w_mod[0], b_mod[0].reshape(1, -1))
    mods = [mod[:, k * D_MODEL:(k + 1) * D_MODEL] for k in range(6)]
    mods_p = [m[0:1].reshape(1, 1, D_MODEL) for m in mods]
    mods_s = [m[1:1 + nb_s].reshape(nb_s, 1, D_MODEL) for m in mods]

    wi = w_in[0]
    sec = P_Z
    gate_cols = jnp.concatenate([wi[:, sec:sec + G_DT], wi[:, -N_DIR * S_HEADS:]], axis=1)
    gate_cols = jnp.pad(gate_cols, ((0, 0), (0, GATE_W - N_GATES)))
    w_pack = jnp.concatenate([wi[:, :sec], wi[:, sec + G_DT:sec + G_DT + MIX_S + XBC], gate_cols],
                             axis=1).astype(BF16)
    zeros_i = jnp.zeros_like(b_igate[0])
    bias_c, bias_r = _gate_layout(b_igate[0], b_fgate[0], dt_bias[0])
    alog_c, alog_r = _gate_layout(zeros_i, zeros_i, a_log[0])
    row = lambda v: v.reshape(1, -1)
    weights = dict(
        w_pack=w_pack, wg_t=gate_cols.T.astype(BF16),
        g_pre_mix=row(g_pre_mix[0]), g_post_mix=row(g_post_mix[0]),
        g_pre_mlp=row(g_pre_mlp[0]), g_post_mlp=row(g_post_mlp[0]),
        conv_w9=conv_w[0].reshape(9, XBC), conv_b=row(conv_b[0]),
        gate_bias_c=bias_c, gate_bias_r=bias_r, alog_c=alog_c, alog_r=alog_r,
        dskip=row(jnp.repeat(d_skip[0], S_HEADDIM)),
        g_mlstm_norm=row(g_mlstm_norm[0]), g_ssd_norm=row(g_ssd_norm[0]),
        w_out=w_out[0].astype(BF16), w_mlp_in=w_mlp_in[0].astype(BF16), w_mlp_out=w_mlp_out[0].astype(BF16))

    y_p, st = _block(x_prompt, mods_p, None, weights, x_prompt.shape[1])
    cache = (state_mlstm_c[:, 0], state_mlstm_n[:, 0], state_mlstm_m[:, 0], state_ssd[:, 0])
    y_s, _ = _block(x_sample, mods_s, cache, weights, GRID_W)
    return (y_p, y_s) + st
```

```python
import functools

import jax
import jax.numpy as jnp
from jax import lax
from jax.experimental import pallas as pl
from jax.experimental.pallas import tpu as pltpu

F32 = jnp.float32
BF16 = jnp.bfloat16

D_MODEL = 2048
CHUNK = 128
N_DIR = 2
M_HEADS = 4
M_DQK = 128
M_DV = 256
MIX_M = M_HEADS * M_DV
S_HEADS = 16
S_HEADDIM = 64
S_STATE = 128
S_GROUPS = 4
S_REP = S_HEADS // S_GROUPS
MIX_S = S_HEADS * S_HEADDIM
XBC = MIX_S + 2 * S_GROUPS * S_STATE
D_FF = 4 * D_MODEL
GRID_W = 64
EPS = 1e-6

P_Q = 0
P_K = M_HEADS * M_DQK
P_V = 2 * M_HEADS * M_DQK
P_O = P_V + MIX_M
P_Z = P_O + MIX_M
P_XBC = P_Z + MIX_S
P_GATE = P_XBC + XBC
GATE_W = 128
G_I = 0
G_F = N_DIR * M_HEADS
G_DT = 2 * N_DIR * M_HEADS
N_GATES = G_DT + N_DIR * S_HEADS

VMEM_LIMIT = 48 * 1024 * 1024

_NT = (((1,), (1,)), ((), ()))
_TN = (((0,), (0,)), ((), ()))


def _params(sem, limit=VMEM_LIMIT):
    return pltpu.CompilerParams(dimension_semantics=sem, vmem_limit_bytes=limit)


def _silu(x):
    return x / (1.0 + jnp.exp(-x))


def _sigmoid(x):
    return 1.0 / (1.0 + jnp.exp(-x))


def _softplus(x):
    return jnp.maximum(x, 0.0) + jnp.log1p(jnp.exp(-jnp.abs(x)))


def _rms(x):
    return x * lax.rsqrt(jnp.mean(x * x, axis=-1, keepdims=True) + EPS)


def _dot(a, b, dims=None, precision=None):
    if dims is None:
        dims = (((a.ndim - 1,), (0,)), ((), ()))
    return lax.dot_general(a, b, dims, precision=precision, preferred_element_type=F32)


def _mod_kernel(c_ref, w_ref, b_ref, o_ref):
    a = _silu(c_ref[...]).astype(BF16)
    o_ref[...] = _dot(a, w_ref[...].astype(BF16)) + b_ref[...]


def _modulation(cond8, w_mod, b_mod):
    n = w_mod.shape[1]
    tn = 1024
    return pl.pallas_call(
        _mod_kernel,
        grid=(n // tn,),
        in_specs=[pl.BlockSpec((8, D_MODEL), lambda j: (0, 0)),
                  pl.BlockSpec((D_MODEL, tn), lambda j: (0, j)),
                  pl.BlockSpec((1, tn), lambda j: (0, j))],
        out_specs=pl.BlockSpec((8, tn), lambda j: (0, j)),
        out_shape=jax.ShapeDtypeStruct((8, n), F32),
        compiler_params=_params(("parallel",)),
        name="modulation",
    )(cond8, w_mod, b_mod)


IN_TM = 1024
IN_TN = 1024
IN_SUB = 128


def _inproj_kernel(x_ref, sc_ref, sh_ref, g_ref, w_ref, wg_ref, o_ref, gc_ref, gt_ref, u_ref):
    j = pl.program_id(1)

    @pl.when(j == 0)
    def _():
        def body(r, carry):
            rows = pl.ds(pl.multiple_of(r * IN_SUB, IN_SUB), IN_SUB)
            y = _rms(x_ref[rows, :]) * g_ref[...]
            u_ref[rows, :] = (y * (1.0 + sc_ref[...]) + sh_ref[...]).astype(BF16)
            return carry
        lax.fori_loop(0, IN_TM // IN_SUB, body, 0)

    o_ref[...] = _dot(u_ref[...], w_ref[...])

    @pl.when(j == pl.num_programs(1) - 1)
    def _():
        gates = _dot(u_ref[...], wg_ref[...])
        gc_ref[...] = gates
        gt_ref[...] = gates.T


def _in_proj(x2d, sc, sh, g, w_pack, wg, rows_per_mod):
    t = x2d.shape[0]
    tiles_per_mod = rows_per_mod // IN_TM
    return pl.pallas_call(
        _inproj_kernel,
        grid=(t // IN_TM, P_GATE // IN_TN),
        in_specs=[pl.BlockSpec((IN_TM, D_MODEL), lambda i, j: (i, 0)),
                  pl.BlockSpec((None, 1, D_MODEL), lambda i, j: (i // tiles_per_mod, 0, 0)),
                  pl.BlockSpec((None, 1, D_MODEL), lambda i, j: (i // tiles_per_mod, 0, 0)),
                  pl.BlockSpec((1, D_MODEL), lambda i, j: (0, 0)),
                  pl.BlockSpec((D_MODEL, IN_TN), lambda i, j: (0, j)),
                  pl.BlockSpec((D_MODEL, GATE_W), lambda i, j: (0, 0))],
        out_specs=[pl.BlockSpec((IN_TM, IN_TN), lambda i, j: (i, j)),
                   pl.BlockSpec((IN_TM, GATE_W), lambda i, j: (i, 0)),
                   pl.BlockSpec((GATE_W, IN_TM), lambda i, j: (0, i))],
        out_shape=[jax.ShapeDtypeStruct((t, P_GATE), F32),
                   jax.ShapeDtypeStruct((t, GATE_W), F32),
                   jax.ShapeDtypeStruct((GATE_W, t), F32)],
        scratch_shapes=[pltpu.VMEM((IN_TM, D_MODEL), BF16)],
        compiler_params=_params(("parallel", "arbitrary")),
        name="in_proj",
    )(x2d, sc, sh, g, w_pack, wg)


def _conv_kernel(x_ref, w_ref, b_ref, o_ref, *, seq, width):
    x = x_ref[...]
    t = lax.broadcasted_iota(jnp.int32, x.shape, 0)
    c = jnp.bitwise_and(t, width - 1)
    xl = jnp.where(c >= 1, pltpu.roll(x, 1, 0), 0.0)
    xr = jnp.where(c <= width - 2, pltpu.roll(x, seq - 1, 0), 0.0)

    def taps(di):
        return w_ref[3 * di:3 * di + 1, :] * xl + w_ref[3 * di + 1:3 * di + 2, :] * x \
            + w_ref[3 * di + 2:3 * di + 3, :] * xr

    out = taps(1) + b_ref[...]
    if seq > width:
        out = out + jnp.where(t >= width, pltpu.roll(taps(0), width, 0), 0.0)
        out = out + jnp.where(t < seq - width, pltpu.roll(taps(2), seq - width, 0), 0.0)
    o_ref[...] = _silu(out)


def _conv(p, conv_w9, conv_b, nb, seq, width, cn):
    col0 = P_XBC // cn
    return pl.pallas_call(
        functools.partial(_conv_kernel, seq=seq, width=width),
        grid=(nb, XBC // cn),
        in_specs=[pl.BlockSpec((seq, cn), lambda b, j: (b, col0 + j)),
                  pl.BlockSpec((9, cn), lambda b, j: (0, j)),
                  pl.BlockSpec((1, cn), lambda b, j: (0, j))],
        out_specs=pl.BlockSpec((seq, cn), lambda b, j: (b, j)),
        out_shape=jax.ShapeDtypeStruct((nb * seq, XBC), F32),
        compiler_params=_params(("parallel", "parallel")),
        name="grid_conv",
    )(p, conv_w9, conv_b)


def _tri_masks():
    r = lax.broadcasted_iota(jnp.int32, (CHUNK, CHUNK), 0)
    c = lax.broadcasted_iota(jnp.int32, (CHUNK, CHUNK), 1)
    return r >= c, r <= c


def _cumsums(col_vals, row_vals, direction, lower, upper):
    lo = lower.astype(F32)
    up = upper.astype(F32)
    hi = lax.Precision.HIGHEST
    if direction == 0:
        return _dot(lo, col_vals, precision=hi), _dot(row_vals, up, precision=hi)
    return _dot(up, col_vals, precision=hi), _dot(row_vals, lo, precision=hi)


def _mlstm_kernel(*refs, nc, zero_init):
    (qf, kf, vf, of, gcf, gtf, qb, kb, vb, ob, gcb, gtb, bias_c, bias_r, gn) = refs[:15]
    pos = 15
    if not zero_init:
        c0, n0, m0 = refs[pos:pos + 3]
        pos += 3
    hm_out, c_out, n_out, m_out, c_s, n_s, m_s, hpart = refs[pos:]
    s = pl.program_id(1)
    half = nc // 2

    @pl.when(s == 0)
    def _():
        if zero_init:
            c_s[...] = jnp.zeros_like(c_s)
            n_s[...] = jnp.zeros_like(n_s)
            m_s[...] = jnp.zeros_like(m_s)
        else:
            c_s[...] = c0[...]
            n_s[...] = n0[...]
            m_s[...] = m0[...]

    lower, upper = _tri_masks()
    per_dir = ((qf, kf, vf, of, gcf, gtf, lower, s), (qb, kb, vb, ob, gcb, gtb, upper, nc - 1 - s))
    for d, (q_ref, k_ref, v_ref, o_ref, gc_ref, gt_ref, mask, chunk) in enumerate(per_dir):
        gc = gc_ref[...] + bias_c[...]
        gt = gt_ref[...] + bias_r[...]
        lf_c = jnp.minimum(gc, 0.0) - jnp.log1p(jnp.exp(-jnp.abs(gc)))
        lf_t = jnp.minimum(gt, 0.0) - jnp.log1p(jnp.exp(-jnp.abs(gt)))
        b_c, b_t = _cumsums(lf_c, lf_t, d, lower, upper)
        last = CHUNK - 1 if d == 0 else 0
        rows = pl.ds(pl.multiple_of(chunk * CHUNK, CHUNK), CHUNK)
        for h in range(M_HEADS):
            gi = G_I + d * M_HEADS + h
            gf = G_F + d * M_HEADS + h
            sd = d * M_HEADS + h
            i_col, b_col = gc[:, gi:gi + 1], b_c[:, gf:gf + 1]
            i_row, b_row = gt[gi:gi + 1, :], b_t[gf:gf + 1, :]
            m_prev = m_s[sd:sd + 1, 0:1]
            qh = q_ref[:, h * M_DQK:(h + 1) * M_DQK] * (M_DQK ** -0.5)
            kh = k_ref[:, h * M_DQK:(h + 1) * M_DQK]
            vh = v_ref[:, h * M_DV:(h + 1) * M_DV].astype(BF16)
            cmat = c_s[d, h]
            nrow = n_s[sd:sd + 1, :]

            dmat = jnp.where(mask, b_col - b_row + i_row, -jnp.inf)
            m_inter = b_col + m_prev
            m_t = jnp.maximum(m_inter, jnp.max(dmat, axis=-1, keepdims=True))
            w_inter = jnp.exp(m_inter - m_t)
            qh16 = qh.astype(BF16)
            sc = _dot(qh16, kh.astype(BF16), _NT) * jnp.exp(dmat - m_t)
            num = w_inter * _dot(qh16, cmat.astype(BF16)) + _dot(sc.astype(BF16), vh)
            den = w_inter * jnp.sum(qh * nrow, axis=-1, keepdims=True) + jnp.sum(sc, axis=-1, keepdims=True)
            hval = num / jnp.maximum(jnp.abs(den), jnp.exp(-m_t))

            b_last = b_col[last:last + 1, :]
            g_row = b_last - b_row + i_row
            g_col = b_last - b_col + i_col
            m_new = jnp.maximum(b_last + m_prev, jnp.max(g_row, axis=-1, keepdims=True))
            decay = jnp.exp(b_last + m_prev - m_new)
            kw = kh * jnp.exp(g_col - m_new)
            c_s[d, h] = decay * cmat + _dot(kw.astype(BF16), vh, _TN)
            n_s[sd:sd + 1, :] = decay * nrow + jnp.sum(kw, axis=0, keepdims=True)
            m_s[sd:sd + 1, :] = jnp.broadcast_to(m_new, (1, M_DQK))

            cols = slice(h * M_DV, (h + 1) * M_DV)

            @pl.when(s < half)
            def _():
                hpart[rows, cols] = hval

            @pl.when(s >= half)
            def _():
                tot = hval + hpart[rows, cols]
                hm_out[rows, cols] = (_rms(tot) * gn[:, cols] * _sigmoid(o_ref[:, cols])).astype(BF16)

    @pl.when(s == nc - 1)
    def _():
        c_out[...] = c_s[...]
        n_out[...] = n_s[...]
        m_out[...] = m_s[...]


def _mlstm(p, gc, gt, bias_c, bias_r, g_norm, state, nb, nc):
    zero_init = state is None
    seq = nc * CHUNK

    def fwd(cb):
        return lambda b, s: (b * nc + s, cb)

    def bwd(cb):
        return lambda b, s: (b * nc + nc - 1 - s, cb)

    def stream(mk):
        return [pl.BlockSpec((CHUNK, M_HEADS * M_DQK), mk(P_Q // (M_HEADS * M_DQK))),
                pl.BlockSpec((CHUNK, M_HEADS * M_DQK), mk(P_K // (M_HEADS * M_DQK))),
                pl.BlockSpec((CHUNK, MIX_M), mk(P_V // MIX_M)),
                pl.BlockSpec((CHUNK, MIX_M), mk(P_O // MIX_M)),
                pl.BlockSpec((CHUNK, GATE_W), mk(0))]

    in_specs = (stream(fwd) + [pl.BlockSpec((GATE_W, CHUNK), lambda b, s: (0, b * nc + s))]
                + stream(bwd) + [pl.BlockSpec((GATE_W, CHUNK), lambda b, s: (0, b * nc + nc - 1 - s))]
                + [pl.BlockSpec((1, GATE_W), lambda b, s: (0, 0)),
                   pl.BlockSpec((GATE_W, 1), lambda b, s: (0, 0)),
                   pl.BlockSpec((1, MIX_M), lambda b, s: (0, 0))])
    args = ([p] * 4 + [gc, gt]) * 2 + [bias_c, bias_r, g_norm]
    state_specs = [pl.BlockSpec((None, N_DIR, M_HEADS, M_DQK, M_DV), lambda b, s: (b, 0, 0, 0, 0)),
                   pl.BlockSpec((None, N_DIR * M_HEADS, M_DQK), lambda b, s: (b, 0, 0)),
                   pl.BlockSpec((None, N_DIR * M_HEADS, M_DQK), lambda b, s: (b, 0, 0))]
    if not zero_init:
        in_specs += state_specs
        args += list(state)
    return pl.pallas_call(
        functools.partial(_mlstm_kernel, nc=nc, zero_init=zero_init),
        grid=(nb, nc),
        in_specs=in_specs,
        out_specs=[pl.BlockSpec((None, seq, MIX_M), lambda b, s: (b, 0, 0))] + state_specs,
        out_shape=[jax.ShapeDtypeStruct((nb, seq, MIX_M), BF16),
                   jax.ShapeDtypeStruct((nb, N_DIR, M_HEADS, M_DQK, M_DV), F32),
                   jax.ShapeDtypeStruct((nb, N_DIR * M_HEADS, M_DQK), F32),
                   jax.ShapeDtypeStruct((nb, N_DIR * M_HEADS, M_DQK), F32)],
        scratch_shapes=[pltpu.VMEM((N_DIR, M_HEADS, M_DQK, M_DV), F32),
                        pltpu.VMEM((N_DIR * M_HEADS, M_DQK), F32),
                        pltpu.VMEM((N_DIR * M_HEADS, M_DQK), F32),
                        pltpu.VMEM((seq, MIX_M), F32)],
        compiler_params=_params(("parallel", "arbitrary")),
        name="mlstm",
    )(*args)


GROUP_P = S_REP * S_HEADDIM


def _ssd_kernel(*refs, nc, zero_init):
    (xf, bf, cf, zf, gcf, gtf, xb, bb, cb, zb, gcb, gtb, bias_c, bias_r, alog_c, alog_r, dskip, gn) = refs[:18]
    pos = 18
    if not zero_init:
        s0 = refs[pos]
        pos += 1
    ys_out, s_out, s_s, ypart, ybuf, xwbuf = refs[pos:]
    s = pl.program_id(1)
    half = nc // 2

    @pl.when(s == 0)
    def _():
        if zero_init:
            s_s[...] = jnp.zeros_like(s_s)
        else:
            s_s[...] = s0[...]

    lower, upper = _tri_masks()
    per_dir = ((xf, bf, cf, zf, gcf, gtf, lower, s), (xb, bb, cb, zb, gcb, gtb, upper, nc - 1 - s))
    for d, (x_ref, b_ref, c_ref, z_ref, gc_ref, gt_ref, mask, chunk) in enumerate(per_dir):
        dt_c = _softplus(gc_ref[...] + bias_c[...])
        dt_t = _softplus(gt_ref[...] + bias_r[...])
        la_c = dt_c * -jnp.exp(alog_c[...])
        la_t = dt_t * -jnp.exp(alog_r[...])
        cs_c, cs_t = _cumsums(la_c, la_t, d, lower, upper)
        last = CHUNK - 1 if d == 0 else 0
        rows = pl.ds(pl.multiple_of(chunk * CHUNK, CHUNK), CHUNK)
        for g in range(S_GROUPS):
            cg = c_ref[:, g * S_STATE:(g + 1) * S_STATE].astype(BF16)
            bg = b_ref[:, g * S_STATE:(g + 1) * S_STATE].astype(BF16)
            cbm = _dot(cg, bg, _NT)
            state = s_s[d, g]
            c_state = _dot(cg, state.astype(BF16), _NT)
            gcols = slice(g * GROUP_P, (g + 1) * GROUP_P)
            for r in range(S_REP):
                gi = G_DT + d * S_HEADS + g * S_REP + r
                hc = slice(g * GROUP_P + r * S_HEADDIM, g * GROUP_P + (r + 1) * S_HEADDIM)
                lc = slice(r * S_HEADDIM, (r + 1) * S_HEADDIM)
                dt_col, cs_col = dt_c[:, gi:gi + 1], cs_c[:, gi:gi + 1]
                dt_row, cs_row = dt_t[gi:gi + 1, :], cs_t[gi:gi + 1, :]
                xh = x_ref[:, hc]
                seg = jnp.exp(jnp.where(mask, cs_col - cs_row, -jnp.inf))
                mix = cbm * seg * dt_row
                ybuf[:, hc] = _dot(mix.astype(BF16), xh.astype(BF16)) + c_state[:, lc] * jnp.exp(cs_col)
                tot = cs_col[last:last + 1, :]
                xwbuf[:, hc] = xh * (jnp.exp(tot - cs_col) * dt_col)
                s_s[d, g, lc, :] = state[lc, :] * jnp.exp(tot)
            s_s[d, g] = s_s[d, g] + _dot(xwbuf[:, gcols].astype(BF16), bg, _TN)

        @pl.when(s < half)
        def _():
            ypart[rows, :] = ybuf[...]

        @pl.when(s >= half)
        def _():
            y = ybuf[...] + ypart[rows, :] + dskip[...] * x_ref[...]
            ys_out[rows, :] = (_rms(y * _silu(z_ref[...])) * gn[...]).astype(BF16)

    @pl.when(s == nc - 1)
    def _():
        s_out[...] = s_s[...]


def _ssd(p, xbc, gc, gt, bias_c, bias_r, alog_c, alog_r, dskip, g_norm, state, nb, nc):
    zero_init = state is None
    seq = nc * CHUNK
    bc_w = S_GROUPS * S_STATE

    def fwd(cb):
        return lambda b, s: (b * nc + s, cb)

    def bwd(cb):
        return lambda b, s: (b * nc + nc - 1 - s, cb)

    def stream(mk):
        return [pl.BlockSpec((CHUNK, MIX_S), mk(0)),
                pl.BlockSpec((CHUNK, bc_w), mk(MIX_S // bc_w)),
                pl.BlockSpec((CHUNK, bc_w), mk(MIX_S // bc_w + 1)),
                pl.BlockSpec((CHUNK, MIX_S), mk(P_Z // MIX_S)),
                pl.BlockSpec((CHUNK, GATE_W), mk(0))]

    const = lambda b, s: (0, 0)
    in_specs = (stream(fwd) + [pl.BlockSpec((GATE_W, CHUNK), lambda b, s: (0, b * nc + s))]
                + stream(bwd) + [pl.BlockSpec((GATE_W, CHUNK), lambda b, s: (0, b * nc + nc - 1 - s))]
                + [pl.BlockSpec((1, GATE_W), const), pl.BlockSpec((GATE_W, 1), const),
                   pl.BlockSpec((1, GATE_W), const), pl.BlockSpec((GATE_W, 1), const),
                   pl.BlockSpec((1, MIX_S), const), pl.BlockSpec((1, MIX_S), const)])
    args = [xbc, xbc, xbc, p, gc, gt] * 2 + [bias_c, bias_r, alog_c, alog_r, dskip, g_norm]
    state_spec = pl.BlockSpec((None, N_DIR, S_GROUPS, GROUP_P, S_STATE), lambda b, s: (b, 0, 0, 0, 0))
    if not zero_init:
        in_specs.append(state_spec)
        args.append(state)
    return pl.pallas_call(
        functools.partial(_ssd_kernel, nc=nc, zero_init=zero_init),
        grid=(nb, nc),
        in_specs=in_specs,
        out_specs=[pl.BlockSpec((None, seq, MIX_S), lambda b, s: (b, 0, 0)), state_spec],
        out_shape=[jax.ShapeDtypeStruct((nb, seq, MIX_S), BF16),
                   jax.ShapeDtypeStruct((nb, N_DIR, S_GROUPS, GROUP_P, S_STATE), F32)],
        scratch_shapes=[pltpu.VMEM((N_DIR, S_GROUPS, GROUP_P, S_STATE), F32),
                        pltpu.VMEM((seq, MIX_S), F32),
                        pltpu.VMEM((CHUNK, MIX_S), F32),
                        pltpu.VMEM((CHUNK, MIX_S), F32)],
        compiler_params=_params(("parallel", "arbitrary")),
        name="ssd",
    )(*args)


OUT_TM = 256


def _outproj_kernel(hm_ref, ys_ref, w_ref, x_ref, g1_ref, gpost_ref, gpre_ref, sc_ref, sh_ref, x1_ref, u2_ref):
    mix = _dot(hm_ref[...], w_ref[0:MIX_M, :]) + _dot(ys_ref[...], w_ref[MIX_M:, :])
    x1 = x_ref[...] + g1_ref[...] * (_rms(mix) * gpost_ref[...])
    x1_ref[...] = x1
    u2_ref[...] = (_rms(x1) * gpre_ref[...] * (1.0 + sc_ref[...]) + sh_ref[...]).astype(BF16)


def _out_proj(hm, ys, w_out16, x2d, g1, gpost, gpre, sc2, sh2, rows_per_mod):
    t = x2d.shape[0]
    tiles_per_mod = rows_per_mod // OUT_TM
    row = lambda i: (i, 0)
    const = lambda i: (0, 0)
    mod = pl.BlockSpec((None, 1, D_MODEL), lambda i: (i // tiles_per_mod, 0, 0))
    return pl.pallas_call(
        _outproj_kernel,
        grid=(t // OUT_TM,),
        in_specs=[pl.BlockSpec((OUT_TM, MIX_M), row), pl.BlockSpec((OUT_TM, MIX_S), row),
                  pl.BlockSpec((D_MODEL, D_MODEL), const), pl.BlockSpec((OUT_TM, D_MODEL), row),
                  mod, pl.BlockSpec((1, D_MODEL), const), pl.BlockSpec((1, D_MODEL), const), mod, mod],
        out_specs=[pl.BlockSpec((OUT_TM, D_MODEL), row), pl.BlockSpec((OUT_TM, D_MODEL), row)],
        out_shape=[jax.ShapeDtypeStruct((t, D_MODEL), F32), jax.ShapeDtypeStruct((t, D_MODEL), BF16)],
        compiler_params=_params(("parallel",)),
        name="out_proj",
    )(hm, ys, w_out16, x2d, g1, gpost, gpre, sc2, sh2)


MLP_TM = 1024
MLP_TH = 512
MLP_SUB = 256


def _mlp_kernel(u_ref, w1_ref, w2_ref, x1_hbm, g2_ref, gpost_ref, o_ref, x1_buf, x1_sem):
    i = pl.program_id(0)
    j = pl.program_id(1)

    def x1_copy():
        rows = pl.ds(pl.multiple_of(i * MLP_TM, MLP_TM), MLP_TM)
        return pltpu.make_async_copy(x1_hbm.at[rows, :], x1_buf, x1_sem)

    @pl.when(j == 0)
    def _():
        x1_copy().start()
        o_ref[...] = jnp.zeros_like(o_ref)

    for r in range(MLP_TM // MLP_SUB):
        rows = slice(r * MLP_SUB, (r + 1) * MLP_SUB)
        hid = jnp.square(jnp.maximum(_dot(u_ref[rows, :], w1_ref[...]), 0.0)).astype(BF16)
        o_ref[rows, :] += _dot(hid, w2_ref[...])

    @pl.when(j == pl.num_programs(1) - 1)
    def _():
        x1_copy().wait()

        def fin(r, carry):
            rows = pl.ds(pl.multiple_of(r * MLP_SUB, MLP_SUB), MLP_SUB)
            o_ref[rows, :] = x1_buf[rows, :] + g2_ref[...] * (_rms(o_ref[rows, :]) * gpost_ref[...])
            return carry
        lax.fori_loop(0, MLP_TM // MLP_SUB, fin, 0)


def _mlp(u2, w1, w2, x1, g2, gpost, rows_per_mod):
    t = u2.shape[0]
    tiles_per_mod = rows_per_mod // MLP_TM
    return pl.pallas_call(
        _mlp_kernel,
        grid=(t // MLP_TM, D_FF // MLP_TH),
        in_specs=[pl.BlockSpec((MLP_TM, D_MODEL), lambda i, j: (i, 0)),
                  pl.BlockSpec((D_MODEL, MLP_TH), lambda i, j: (0, j)),
                  pl.BlockSpec((MLP_TH, D_MODEL), lambda i, j: (j, 0)),
                  pl.BlockSpec(memory_space=pl.ANY),
                  pl.BlockSpec((None, 1, D_MODEL), lambda i, j: (i // tiles_per_mod, 0, 0)),
                  pl.BlockSpec((1, D_MODEL), lambda i, j: (0, 0))],
        out_specs=pl.BlockSpec((MLP_TM, D_MODEL), lambda i, j: (i, 0)),
        out_shape=jax.ShapeDtypeStruct((t, D_MODEL), F32),
        scratch_shapes=[pltpu.VMEM((MLP_TM, D_MODEL), F32), pltpu.SemaphoreType.DMA(())],
        compiler_params=_params(("parallel", "arbitrary"), 56 * 1024 * 1024),
        name="mlp",
    )(u2, w1, w2, x1, g2, gpost)


def _gate_layout(i_vals, f_vals, dt_vals):
    v = jnp.concatenate([i_vals.reshape(-1), f_vals.reshape(-1), dt_vals.reshape(-1)]).astype(F32)
    v = jnp.pad(v, (0, GATE_W - N_GATES))
    return v.reshape(1, GATE_W), v.reshape(GATE_W, 1)


def _block(x, mods, state, weights, width):
    nb, seq, _ = x.shape
    nc = seq // CHUNK
    t = nb * seq
    x2d = x.reshape(t, D_MODEL)
    sh1, sc1, g1, sh2, sc2, g2 = mods
    rows_per_mod = t // sh1.shape[0]
    w = weights

    p, gc, gt = _in_proj(x2d, sc1, sh1, w["g_pre_mix"], w["w_pack"], w["wg"], rows_per_mod)
    xbc = _conv(p, w["conv_w9"], w["conv_b"], nb, seq, width, 512)

    if state is None:
        m_state = s_state = None
    else:
        c0, n0, m0, s0 = state
        m_state = (c0, n0.reshape(nb, N_DIR * M_HEADS, M_DQK),
                   jnp.broadcast_to(m0.reshape(nb, N_DIR * M_HEADS, 1), (nb, N_DIR * M_HEADS, M_DQK)))
        s_state = s0.reshape(nb, N_DIR, S_GROUPS, GROUP_P, S_STATE)
    hm, c_new, n_new, m_new = _mlstm(p, gc, gt, w["gate_bias_c"], w["gate_bias_r"], w["g_mlstm_norm"],
                                     m_state, nb, nc)
    ys, s_new = _ssd(p, xbc, gc, gt, w["gate_bias_c"], w["gate_bias_r"], w["alog_c"], w["alog_r"],
                     w["dskip"], w["g_ssd_norm"], s_state, nb, nc)

    x1, u2 = _out_proj(hm.reshape(t, MIX_M), ys.reshape(t, MIX_S), w["w_out"], x2d, g1,
                       w["g_post_mix"], w["g_pre_mlp"], sc2, sh2, rows_per_mod)
    y = _mlp(u2, w["w_mlp_in"], w["w_mlp_out"], x1, g2, w["g_post_mlp"], rows_per_mod)
    new_state = (c_new.reshape(nb, 1, N_DIR, M_HEADS, M_DQK, M_DV),
                 n_new.reshape(nb, 1, N_DIR, M_HEADS, M_DQK),
                 m_new[:, :, 0].reshape(nb, 1, N_DIR, M_HEADS),
                 s_new.reshape(nb, 1, N_DIR, S_HEADS, S_HEADDIM, S_STATE))
    return y.reshape(nb, seq, D_MODEL), new_state


def kernel(x_prompt, x_sample, state_mlstm_c, state_mlstm_n, state_mlstm_m, state_ssd, c, c_ctx, w_mod, b_mod,
           g_pre_mix, g_post_mix, w_in, b_igate, b_fgate, conv_w, conv_b, dt_bias, a_log, d_skip, g_mlstm_norm,
           g_ssd_norm, w_out, g_pre_mlp, g_post_mlp, w_mlp_in, w_mlp_out):
    assert w_mod.shape[0] == 1, "one layer"
    nb_s = x_sample.shape[0]

    cond8 = jnp.zeros((8, D_MODEL), F32).at[0].set(c_ctx).at[1:1 + nb_s].set(c)
    mod = _modulation(cond8, w_mod[0], b_mod[0].reshape(1, -1))
    mods = [mod[:, k * D_MODEL:(k + 1) * D_MODEL] for k in range(6)]
    mods_p = [m[0:1].reshape(1, 1, D_MODEL) for m in mods]
    mods_s = [m[1:1 + nb_s].reshape(nb_s, 1, D_MODEL) for m in mods]

    wi = w_in[0]
    sec = P_Z
    gate_cols = jnp.concatenate([wi[:, sec:sec + G_DT], wi[:, -N_DIR * S_HEADS:]], axis=1)
    gate_cols = jnp.pad(gate_cols, ((0, 0), (0, GATE_W - N_GATES))).astype(BF16)
    w_pack = jnp.concatenate([wi[:, :sec], wi[:, sec + G_DT:sec + G_DT + MIX_S + XBC]], axis=1).astype(BF16)
    zeros_i = jnp.zeros_like(b_igate[0])
    bias_c, bias_r = _gate_layout(b_igate[0], b_fgate[0], dt_bias[0])
    alog_c, alog_r = _gate_layout(zeros_i, zeros_i, a_log[0])
    row = lambda v: v.reshape(1, -1)
    weights = dict(
        w_pack=w_pack, wg=gate_cols,
        g_pre_mix=row(g_pre_mix[0]), g_post_mix=row(g_post_mix[0]),
        g_pre_mlp=row(g_pre_mlp[0]), g_post_mlp=row(g_post_mlp[0]),
        conv_w9=conv_w[0].reshape(9, XBC), conv_b=row(conv_b[0]),
        gate_bias_c=bias_c, gate_bias_r=bias_r, alog_c=alog_c, alog_r=alog_r,
        dskip=row(jnp.repeat(d_skip[0], S_HEADDIM)),
        g_mlstm_norm=row(g_mlstm_norm[0]), g_ssd_norm=row(g_ssd_norm[0]),
        w_out=w_out[0].astype(BF16), w_mlp_in=w_mlp_in[0].astype(BF16), w_mlp_out=w_mlp_out[0].astype(BF16))

    y_p, st = _block(x_prompt, mods_p, None, weights, x_prompt.shape[1])
    cache = (state_mlstm_c[:, 0], state_mlstm_n[:, 0], state_mlstm_m[:, 0], state_ssd[:, 0])
    y_s, _ = _block(x_sample, mods_s, cache, weights, GRID_W)
    return (y_p, y_s) + st
```

```python
import functools

import jax
import jax.numpy as jnp
from jax import lax
from jax.experimental import pallas as pl
from jax.experimental.pallas import tpu as pltpu

F32 = jnp.float32
BF16 = jnp.bfloat16

D_MODEL = 2048
CHUNK = 128
N_DIR = 2
M_HEADS = 4
M_DQK = 128
M_DV = 256
MIX_M = M_HEADS * M_DV
S_HEADS = 16
S_HEADDIM = 64
S_STATE = 128
S_GROUPS = 4
S_REP = S_HEADS // S_GROUPS
MIX_S = S_HEADS * S_HEADDIM
XBC = MIX_S + 2 * S_GROUPS * S_STATE
D_FF = 4 * D_MODEL
GRID_W = 64
EPS = 1e-6
LANES = 128

P_Q = 0
P_K = M_HEADS * M_DQK
P_V = 2 * M_HEADS * M_DQK
P_O = P_V + MIX_M
P_Z = P_O + MIX_M
P_XBC = P_Z + MIX_S
P_MAIN = P_XBC + XBC
GATE_W = LANES
N_UNITS = N_DIR * M_HEADS
G_I = 0
G_F = N_UNITS
G_DT = 2 * N_UNITS
N_GATES = G_DT + N_DIR * S_HEADS
G_WK = G_DT + N_DIR * S_HEADS
assert G_WK + N_DIR * S_HEADS <= GATE_W

PAIR = LANES // S_HEADDIM
S_PAIRS = S_HEADS // PAIR
PAIRS_PER_GROUP = S_REP // PAIR

VMEM_LIMIT = 48 * 1024 * 1024

_NT = (((1,), (1,)), ((), ()))


def _params(sem, limit=VMEM_LIMIT):
    return pltpu.CompilerParams(dimension_semantics=sem, vmem_limit_bytes=limit)


def _silu(x):
    return x / (1.0 + jnp.exp(-x))


def _sigmoid(x):
    return 1.0 / (1.0 + jnp.exp(-x))


def _rms(x):
    return x * lax.rsqrt(jnp.mean(x * x, axis=-1, keepdims=True) + EPS)


def _dot(a, b, dims=None, precision=None):
    if dims is None:
        dims = (((a.ndim - 1,), (0,)), ((), ()))
    return lax.dot_general(a, b, dims, precision=precision, preferred_element_type=F32)


def _lane_bcast(tile, lane):
    return jnp.broadcast_to(tile[:, lane:lane + 1], tile.shape)


def _mod_kernel(c_ref, w_ref, b_ref, o_ref):
    a = _silu(c_ref[...]).astype(BF16)
    o_ref[...] = _dot(a, w_ref[...].astype(BF16)) + b_ref[...]


def _modulation(cond8, w_mod, b_mod):
    n = w_mod.shape[1]
    tn = 1024
    return pl.pallas_call(
        _mod_kernel,
        grid=(n // tn,),
        in_specs=[pl.BlockSpec((8, D_MODEL), lambda j: (0, 0)),
                  pl.BlockSpec((D_MODEL, tn), lambda j: (0, j)),
                  pl.BlockSpec((1, tn), lambda j: (0, j))],
        out_specs=pl.BlockSpec((8, tn), lambda j: (0, j)),
        out_shape=jax.ShapeDtypeStruct((8, n), F32),
        compiler_params=_params(("parallel",)),
        name="modulation",
    )(cond8, w_mod, b_mod)


IN_TM = 1024
IN_TN = 1024
IN_SUB = 128


def _inproj_kernel(x_ref, sc_ref, sh_ref, g_ref, wa_ref, wb_ref, wg_ref, o_ref, gc_ref, u_ref):
    j = pl.program_id(1)
    n_a = P_Z // IN_TN

    @pl.when(j == 0)
    def _():
        def body(r, carry):
            rows = pl.ds(pl.multiple_of(r * IN_SUB, IN_SUB), IN_SUB)
            y = _rms(x_ref[rows, :]) * g_ref[...]
            u_ref[rows, :] = (y * (1.0 + sc_ref[...]) + sh_ref[...]).astype(BF16)
            return carry
        lax.fori_loop(0, IN_TM // IN_SUB, body, 0)

    @pl.when(j < n_a)
    def _():
        o_ref[...] = _dot(u_ref[...], wa_ref[...])

    @pl.when(j >= n_a)
    def _():
        o_ref[...] = _dot(u_ref[...], wb_ref[...])

    @pl.when(j == pl.num_programs(1) - 1)
    def _():
        gc_ref[...] = _dot(u_ref[...], wg_ref[...])


def _in_proj(x2d, sc, sh, g, w_a, w_b, wg, rows_per_mod):
    t = x2d.shape[0]
    tiles_per_mod = rows_per_mod // IN_TM
    n_a = P_Z // IN_TN
    return pl.pallas_call(
        _inproj_kernel,
        grid=(t // IN_TM, P_MAIN // IN_TN),
        in_specs=[pl.BlockSpec((IN_TM, D_MODEL), lambda i, j: (i, 0)),
                  pl.BlockSpec((None, 1, D_MODEL), lambda i, j: (i // tiles_per_mod, 0, 0)),
                  pl.BlockSpec((None, 1, D_MODEL), lambda i, j: (i // tiles_per_mod, 0, 0)),
                  pl.BlockSpec((1, D_MODEL), lambda i, j: (0, 0)),
                  pl.BlockSpec((D_MODEL, IN_TN), lambda i, j: (0, jnp.minimum(j, n_a - 1))),
                  pl.BlockSpec((D_MODEL, IN_TN), lambda i, j: (0, jnp.maximum(j - n_a, 0))),
                  pl.BlockSpec((D_MODEL, GATE_W), lambda i, j: (0, 0))],
        out_specs=[pl.BlockSpec((IN_TM, IN_TN), lambda i, j: (i, j)),
                   pl.BlockSpec((IN_TM, GATE_W), lambda i, j: (i, 0))],
        out_shape=[jax.ShapeDtypeStruct((t, P_MAIN), F32),
                   jax.ShapeDtypeStruct((t, GATE_W), F32)],
        scratch_shapes=[pltpu.VMEM((IN_TM, D_MODEL), BF16)],
        compiler_params=_params(("parallel", "arbitrary"), 56 * 1024 * 1024),
        name="in_proj",
    )(x2d, sc, sh, g, w_a, w_b, wg)


def _conv_kernel(x_ref, w_ref, b_ref, o_ref, *, seq, width):
    x = x_ref[...]
    t = lax.broadcasted_iota(jnp.int32, x.shape, 0)
    c = jnp.bitwise_and(t, width - 1)
    xl = jnp.where(c >= 1, pltpu.roll(x, 1, 0), 0.0)
    xr = jnp.where(c <= width - 2, pltpu.roll(x, seq - 1, 0), 0.0)

    def taps(di):
        return w_ref[3 * di:3 * di + 1, :] * xl + w_ref[3 * di + 1:3 * di + 2, :] * x \
            + w_ref[3 * di + 2:3 * di + 3, :] * xr

    out = taps(1) + b_ref[...]
    if seq > width:
        out = out + jnp.where(t >= width, pltpu.roll(taps(0), width, 0), 0.0)
        out = out + jnp.where(t < seq - width, pltpu.roll(taps(2), seq - width, 0), 0.0)
    o_ref[...] = _silu(out)


def _conv(p, conv_w9, conv_b, nb, seq, width, cn):
    col0 = P_XBC // cn
    return pl.pallas_call(
        functools.partial(_conv_kernel, seq=seq, width=width),
        grid=(nb, XBC // cn),
        in_specs=[pl.BlockSpec((seq, cn), lambda b, j: (b, col0 + j)),
                  pl.BlockSpec((9, cn), lambda b, j: (0, j)),
                  pl.BlockSpec((1, cn), lambda b, j: (0, j))],
        out_specs=pl.BlockSpec((seq, cn), lambda b, j: (b, j)),
        out_shape=jax.ShapeDtypeStruct((nb * seq, XBC), F32),
        compiler_params=_params(("parallel", "parallel")),
        name="grid_conv",
    )(p, conv_w9, conv_b)


def _tri_masks():
    r = lax.broadcasted_iota(jnp.int32, (CHUNK, CHUNK), 0)
    c = lax.broadcasted_iota(jnp.int32, (CHUNK, CHUNK), 1)
    return r >= c, r <= c


SCAN_CHUNKS = 8


def _scan_kernel(gc_ref, bias_ref, alog_ref, col_ref, bcol_ref, row_ref):
    lower, upper = _tri_masks()
    lo, up = lower.astype(F32), upper.astype(F32)
    hi = lax.Precision.HIGHEST
    lane = lax.broadcasted_iota(jnp.int32, (CHUNK, GATE_W), 1)
    time = lax.broadcasted_iota(jnp.int32, (CHUNK, GATE_W), 0)
    lane1 = lax.broadcasted_iota(jnp.int32, (1, GATE_W), 1)

    def backward(l):
        unit_bwd = (l < G_DT) & (jnp.bitwise_and(l, N_UNITS - 1) >= M_HEADS)
        return unit_bwd | ((l >= G_DT + S_HEADS) & (l < N_GATES))

    is_bwd, is_bwd1 = backward(lane), backward(lane1)
    is_i = lane < G_F
    is_f = (lane >= G_F) & (lane < G_DT)
    is_dt = (lane >= G_DT) & (lane < N_GATES)
    neg_a = -jnp.exp(alog_ref[...])
    for c in range(SCAN_CHUNKS):
        rows = slice(c * CHUNK, (c + 1) * CHUNK)
        g = gc_ref[rows, :] + bias_ref[...]
        soft = jnp.log1p(jnp.exp(-jnp.abs(g)))
        logf = jnp.minimum(g, 0.0) - soft
        dt = jnp.maximum(g, 0.0) + soft
        x = jnp.where(is_f, logf, jnp.where(is_dt, dt * neg_a, 0.0))
        cs = jnp.where(is_bwd, _dot(up, x, precision=hi), _dot(lo, x, precision=hi))
        b_units = pltpu.roll(cs, GATE_W - G_F, 1)
        u = g - b_units
        cu = u
        k = 1
        while k < CHUNK:
            prev = jnp.where(time >= k, pltpu.roll(cu, k, 0), -jnp.inf)
            nxt = jnp.where(time < CHUNK - k, pltpu.roll(cu, CHUNK - k, 0), -jnp.inf)
            cu = jnp.maximum(cu, jnp.where(is_bwd, nxt, prev))
            k *= 2
        total = jnp.where(is_bwd1, cs[0:1, :], cs[CHUNK - 1:CHUNK, :])
        wk = jnp.exp(total - cs) * dt
        q = cs - jnp.log(dt)
        col_ref[rows, :] = jnp.where(is_i, cu, cs)
        bcol_ref[rows, :] = b_units
        row_src = jnp.where(is_i, u, jnp.where(is_dt, q, pltpu.roll(wk, G_WK - G_DT, 1)))
        row_ref[rows, :] = row_src.T


def _gate_scans(gc, bias, alog):
    t = gc.shape[0]
    tm = SCAN_CHUNKS * CHUNK
    row = lambda i: (i, 0)
    const = lambda i: (0, 0)
    return pl.pallas_call(
        _scan_kernel,
        grid=(t // tm,),
        in_specs=[pl.BlockSpec((tm, GATE_W), row), pl.BlockSpec((1, GATE_W), const),
                  pl.BlockSpec((1, GATE_W), const)],
        out_specs=[pl.BlockSpec((tm, GATE_W), row)] * 3,
        out_shape=[jax.ShapeDtypeStruct((t, GATE_W), F32)] * 3,
        compiler_params=_params(("parallel",)),
        name="gate_scans",
    )(gc, bias, alog)


def _mlstm_kernel(*refs, nc, zero_init):
    (qf, kf, vf, of, colf, bcolf, rowf, qb, kb, vb, ob, colb, bcolb, rowb, gn) = refs[:15]
    pos = 15
    if not zero_init:
        c0, n0, m0 = refs[pos:pos + 3]
        pos += 3
    hm_out, c_out, n_out, m_out, c_s, n_s, m_s, hpart, hbuf = refs[pos:]
    s = pl.program_id(1)
    half = nc // 2

    @pl.when(s == 0)
    def _():
        if zero_init:
            c_s[...] = jnp.zeros_like(c_s)
            n_s[...] = jnp.zeros_like(n_s)
            m_s[...] = jnp.zeros_like(m_s)
        else:
            c_s[...] = c0[...]
            m_s[...] = m0[...]
            for u in range(N_UNITS):
                n_s[u] = jnp.broadcast_to(n0[u:u + 1, :], (M_DQK, LANES)).T

    lower, upper = _tri_masks()
    lane1 = lax.broadcasted_iota(jnp.int32, (1, LANES), 1)
    ones16 = jnp.ones((CHUNK, LANES), BF16)
    per_dir = ((qf, kf, vf, of, colf, bcolf, rowf, lower, s),
               (qb, kb, vb, ob, colb, bcolb, rowb, upper, nc - 1 - s))
    for d, (q_ref, k_ref, v_ref, o_ref, col_ref, bcol_ref, row_ref, mask, chunk) in enumerate(per_dir):
        last = CHUNK - 1 if d == 0 else 0
        m_prev = m_s[...]
        stab = jnp.maximum(m_prev, col_ref[...])
        floor_c = jnp.exp(-(bcol_ref[...] + stab))
        w_c = jnp.exp(m_prev - stab)
        stab_last = stab[last:last + 1, :]
        m_new = bcol_ref[last:last + 1, :] + stab_last
        decay = w_c[last:last + 1, :]
        for h in range(M_HEADS):
            u = d * M_HEADS + h
            stab_b = _lane_bcast(stab, u)
            u_row = row_ref[u:u + 1, :]
            p = jnp.exp(jnp.where(mask, u_row - stab_b, -jnp.inf))
            qh = q_ref[:, h * M_DQK:(h + 1) * M_DQK] * (M_DQK ** -0.5)
            k_t = k_ref[:, h * M_DQK:(h + 1) * M_DQK].T
            v16 = v_ref[:, h * M_DV:(h + 1) * M_DV].astype(BF16)
            sc = _dot(qh.astype(BF16), k_t.astype(BF16)) * p
            lhs = jnp.concatenate([sc.astype(BF16), (qh * _lane_bcast(w_c, u)).astype(BF16)], axis=1)
            num = _dot(lhs, jnp.concatenate([v16, c_s[d, h].astype(BF16)], axis=0))
            den = _dot(lhs, jnp.concatenate([ones16, n_s[u].astype(BF16)], axis=0))
            inv = 1.0 / jnp.maximum(jnp.abs(den), _lane_bcast(floor_c, u))
            hbuf[:, h * M_DV:(h + 1) * M_DV] = num * jnp.concatenate([inv, inv], axis=1)

            kw_t = (k_t * jnp.exp(u_row - stab_last[:, u:u + 1])).astype(BF16)
            dec = decay[:, u:u + 1]
            c_s[d, h] = dec * c_s[d, h] + _dot(kw_t, v16)
            n_s[u] = dec * n_s[u] + _dot(kw_t, ones16)
        mine = (lane1 >= d * M_HEADS) & (lane1 < (d + 1) * M_HEADS)
        m_s[...] = jnp.where(mine, m_new, m_prev)
        rows = pl.ds(pl.multiple_of(chunk * CHUNK, CHUNK), CHUNK)

        @pl.when(s < half)
        def _():
            hpart[rows, :] = hbuf[...]

        @pl.when(s >= half)
        def _():
            for h in range(M_HEADS):
                cols = slice(h * M_DV, (h + 1) * M_DV)
                tot = hbuf[:, cols] + hpart[rows, cols]
                hm_out[rows, cols] = (_rms(tot) * gn[:, cols] * _sigmoid(o_ref[:, cols])).astype(BF16)

    @pl.when(s == nc - 1)
    def _():
        c_out[...] = c_s[...]
        m_out[...] = m_s[...]
        for u in range(N_UNITS):
            n_out[u:u + 1, :] = n_s[u].T[0:1, :]


def _mlstm(p, scans, g_norm, state, nb, nc):
    zero_init = state is None
    seq = nc * CHUNK
    col, bcol, row = scans

    def fwd(cb):
        return lambda b, s: (b * nc + s, cb)

    def bwd(cb):
        return lambda b, s: (b * nc + nc - 1 - s, cb)

    def stream(mk):
        return [pl.BlockSpec((CHUNK, M_HEADS * M_DQK), mk(P_Q // (M_HEADS * M_DQK))),
                pl.BlockSpec((CHUNK, M_HEADS * M_DQK), mk(P_K // (M_HEADS * M_DQK))),
                pl.BlockSpec((CHUNK, MIX_M), mk(P_V // MIX_M)),
                pl.BlockSpec((CHUNK, MIX_M), mk(P_O // MIX_M)),
                pl.BlockSpec((CHUNK, GATE_W), mk(0)),
                pl.BlockSpec((CHUNK, GATE_W), mk(0)),
                pl.BlockSpec((CHUNK, GATE_W), mk(0))]

    in_specs = stream(fwd) + stream(bwd) + [pl.BlockSpec((1, MIX_M), lambda b, s: (0, 0))]
    args = [p] * 4 + [col, bcol, row] + [p] * 4 + [col, bcol, row, g_norm]
    state_specs = [pl.BlockSpec((None, N_DIR, M_HEADS, M_DQK, M_DV), lambda b, s: (b, 0, 0, 0, 0)),
                   pl.BlockSpec((None, N_UNITS, M_DQK), lambda b, s: (b, 0, 0)),
                   pl.BlockSpec((None, 1, LANES), lambda b, s: (b, 0, 0))]
    if not zero_init:
        in_specs += state_specs
        args += list(state)
    return pl.pallas_call(
        functools.partial(_mlstm_kernel, nc=nc, zero_init=zero_init),
        grid=(nb, nc),
        in_specs=in_specs,
        out_specs=[pl.BlockSpec((None, seq, MIX_M), lambda b, s: (b, 0, 0))] + state_specs,
        out_shape=[jax.ShapeDtypeStruct((nb, seq, MIX_M), BF16),
                   jax.ShapeDtypeStruct((nb, N_DIR, M_HEADS, M_DQK, M_DV), F32),
                   jax.ShapeDtypeStruct((nb, N_UNITS, M_DQK), F32),
                   jax.ShapeDtypeStruct((nb, 1, LANES), F32)],
        scratch_shapes=[pltpu.VMEM((N_DIR, M_HEADS, M_DQK, M_DV), F32),
                        pltpu.VMEM((N_UNITS, M_DQK, LANES), F32),
                        pltpu.VMEM((1, LANES), F32),
                        pltpu.VMEM((seq, MIX_M), F32),
                        pltpu.VMEM((CHUNK, MIX_M), F32)],
        compiler_params=_params(("parallel", "arbitrary")),
        name="mlstm",
    )(*args)


def _ssd_kernel(*refs, nc, zero_init):
    (xf, bf, cf, zf, colf, rowf, xb, bb, cb, zb, colb, rowb, dskip, gn) = refs[:14]
    pos = 14
    if not zero_init:
        s0 = refs[pos]
        pos += 1
    ys_out, s_out, st_s, ypart, ybuf = refs[pos:]
    s = pl.program_id(1)
    half = nc // 2

    @pl.when(s == 0)
    def _():
        if zero_init:
            st_s[...] = jnp.zeros_like(st_s)
        else:
            for d in range(N_DIR):
                for pr in range(S_PAIRS):
                    st_s[d, pr] = s0[d, pr].T

    lower, upper = _tri_masks()
    low_half = lax.broadcasted_iota(jnp.int32, (CHUNK, LANES), 1) < S_HEADDIM
    low_half1 = lax.broadcasted_iota(jnp.int32, (1, LANES), 1) < S_HEADDIM
    per_dir = ((xf, bf, cf, zf, colf, rowf, lower, s), (xb, bb, cb, zb, colb, rowb, upper, nc - 1 - s))
    for d, (x_ref, b_ref, c_ref, z_ref, col_ref, row_ref, mask, chunk) in enumerate(per_dir):
        last = CHUNK - 1 if d == 0 else 0
        cs_c = col_ref[...]
        total = cs_c[last:last + 1, :]
        for g in range(S_GROUPS):
            gs = slice(g * S_STATE, (g + 1) * S_STATE)
            cg = c_ref[:, gs].astype(BF16)
            b_f32 = b_ref[:, gs]
            cbm = _dot(cg, b_f32.astype(BF16), _NT)
            b_t = b_f32.T
            for pg in range(PAIRS_PER_GROUP):
                pr = g * PAIRS_PER_GROUP + pg
                cols = slice(pr * LANES, (pr + 1) * LANES)
                x16 = x_ref[:, cols].astype(BF16)
                zero16 = jnp.zeros_like(x16)
                halves = (jnp.where(low_half, x16, zero16), jnp.where(low_half, zero16, x16))
                state = st_s[d, pr]
                y_acc = jnp.zeros((CHUNK, LANES), F32)
                s_acc = jnp.zeros((S_STATE, LANES), F32)
                cs_b = []
                for e in range(PAIR):
                    gi = G_DT + d * S_HEADS + pr * PAIR + e
                    cs_b.append(_lane_bcast(cs_c, gi))
                    q_row = row_ref[gi:gi + 1, :]
                    wk_row = row_ref[gi + G_WK - G_DT:gi + G_WK - G_DT + 1, :]
                    mix = cbm * jnp.exp(jnp.where(mask, cs_b[e] - q_row, -jnp.inf))
                    y_acc = y_acc + _dot(mix.astype(BF16), halves[e])
                    s_acc = s_acc + _dot((b_t * wk_row).astype(BF16), halves[e])
                carry = jnp.exp(jnp.where(low_half, cs_b[0], cs_b[1]))
                ybuf[:, cols] = y_acc + _dot(cg, state.astype(BF16)) * carry
                gi0 = G_DT + d * S_HEADS + pr * PAIR
                dec = jnp.exp(jnp.where(low_half1, jnp.broadcast_to(total[:, gi0:gi0 + 1], (1, LANES)),
                                        jnp.broadcast_to(total[:, gi0 + 1:gi0 + 2], (1, LANES))))
                st_s[d, pr] = state * dec + s_acc
        rows = pl.ds(pl.multiple_of(chunk * CHUNK, CHUNK), CHUNK)

        @pl.when(s < half)
        def _():
            ypart[rows, :] = ybuf[...]

        @pl.when(s >= half)
        def _():
            y = ybuf[...] + ypart[rows, :] + dskip[...] * x_ref[...]
            ys_out[rows, :] = (_rms(y * _silu(z_ref[...])) * gn[...]).astype(BF16)

    @pl.when(s == nc - 1)
    def _():
        for d in range(N_DIR):
            for pr in range(S_PAIRS):
                s_out[d, pr] = st_s[d, pr].T


def _ssd(p, xbc, scans, dskip, g_norm, state, nb, nc):
    zero_init = state is None
    seq = nc * CHUNK
    bc_w = S_GROUPS * S_STATE
    col, _, row = scans

    def fwd(cb):
        return lambda b, s: (b * nc + s, cb)

    def bwd(cb):
        return lambda b, s: (b * nc + nc - 1 - s, cb)

    def stream(mk):
        return [pl.BlockSpec((CHUNK, MIX_S), mk(0)),
                pl.BlockSpec((CHUNK, bc_w), mk(MIX_S // bc_w)),
                pl.BlockSpec((CHUNK, bc_w), mk(MIX_S // bc_w + 1)),
                pl.BlockSpec((CHUNK, MIX_S), mk(P_Z // MIX_S)),
                pl.BlockSpec((CHUNK, GATE_W), mk(0)),
                pl.BlockSpec((CHUNK, GATE_W), mk(0))]

    const = lambda b, s: (0, 0)
    in_specs = stream(fwd) + stream(bwd) + [pl.BlockSpec((1, MIX_S), const), pl.BlockSpec((1, MIX_S), const)]
    args = [xbc, xbc, xbc, p, col, row] * 2 + [dskip, g_norm]
    state_spec = pl.BlockSpec((None, N_DIR, S_PAIRS, LANES, S_STATE), lambda b, s: (b, 0, 0, 0, 0))
    if not zero_init:
        in_specs.append(state_spec)
        args.append(state)
    return pl.pallas_call(
        functools.partial(_ssd_kernel, nc=nc, zero_init=zero_init),
        grid=(nb, nc),
        in_specs=in_specs,
        out_specs=[pl.BlockSpec((None, seq, MIX_S), lambda b, s: (b, 0, 0)), state_spec],
        out_shape=[jax.ShapeDtypeStruct((nb, seq, MIX_S), BF16),
                   jax.ShapeDtypeStruct((nb, N_DIR, S_PAIRS, LANES, S_STATE), F32)],
        scratch_shapes=[pltpu.VMEM((N_DIR, S_PAIRS, S_STATE, LANES), F32),
                        pltpu.VMEM((seq, MIX_S), F32),
                        pltpu.VMEM((CHUNK, MIX_S), F32)],
        compiler_params=_params(("parallel", "arbitrary")),
        name="ssd",
    )(*args)


OUT_TM = 256


def _outproj_kernel(hm_ref, ys_ref, w_ref, x_ref, g1_ref, gpost_ref, gpre_ref, sc_ref, sh_ref, x1_ref, u2_ref):
    mix = _dot(hm_ref[...], w_ref[0:MIX_M, :]) + _dot(ys_ref[...], w_ref[MIX_M:, :])
    x1 = x_ref[...] + g1_ref[...] * (_rms(mix) * gpost_ref[...])
    x1_ref[...] = x1
    u2_ref[...] = (_rms(x1) * gpre_ref[...] * (1.0 + sc_ref[...]) + sh_ref[...]).astype(BF16)


def _out_proj(hm, ys, w_out16, x2d, g1, gpost, gpre, sc2, sh2, rows_per_mod):
    t = x2d.shape[0]
    tiles_per_mod = rows_per_mod // OUT_TM
    row = lambda i: (i, 0)
    const = lambda i: (0, 0)
    mod = pl.BlockSpec((None, 1, D_MODEL), lambda i: (i // tiles_per_mod, 0, 0))
    return pl.pallas_call(
        _outproj_kernel,
        grid=(t // OUT_TM,),
        in_specs=[pl.BlockSpec((OUT_TM, MIX_M), row), pl.BlockSpec((OUT_TM, MIX_S), row),
                  pl.BlockSpec((D_MODEL, D_MODEL), const), pl.BlockSpec((OUT_TM, D_MODEL), row),
                  mod, pl.BlockSpec((1, D_MODEL), const), pl.BlockSpec((1, D_MODEL), const), mod, mod],
        out_specs=[pl.BlockSpec((OUT_TM, D_MODEL), row), pl.BlockSpec((OUT_TM, D_MODEL), row)],
        out_shape=[jax.ShapeDtypeStruct((t, D_MODEL), F32), jax.ShapeDtypeStruct((t, D_MODEL), BF16)],
        compiler_params=_params(("parallel",)),
        name="out_proj",
    )(hm, ys, w_out16, x2d, g1, gpost, gpre, sc2, sh2)


MLP_TM = 1024
MLP_TH = 512
MLP_SUB = 256


def _mlp_kernel(u_ref, w1_ref, w2_ref, x1_hbm, g2_ref, gpost_ref, o_ref, x1_buf, x1_sem):
    i = pl.program_id(0)
    j = pl.program_id(1)

    def x1_copy():
        rows = pl.ds(pl.multiple_of(i * MLP_TM, MLP_TM), MLP_TM)
        return pltpu.make_async_copy(x1_hbm.at[rows, :], x1_buf, x1_sem)

    @pl.when(j == 0)
    def _():
        x1_copy().start()
        o_ref[...] = jnp.zeros_like(o_ref)

    for r in range(MLP_TM // MLP_SUB):
        rows = slice(r * MLP_SUB, (r + 1) * MLP_SUB)
        hid = jnp.square(jnp.maximum(_dot(u_ref[rows, :], w1_ref[...]), 0.0)).astype(BF16)
        o_ref[rows, :] += _dot(hid, w2_ref[...])

    @pl.when(j == pl.num_programs(1) - 1)
    def _():
        x1_copy().wait()

        def fin(r, carry):
            rows = pl.ds(pl.multiple_of(r * MLP_SUB, MLP_SUB), MLP_SUB)
            o_ref[rows, :] = x1_buf[rows, :] + g2_ref[...] * (_rms(o_ref[rows, :]) * gpost_ref[...])
            return carry
        lax.fori_loop(0, MLP_TM // MLP_SUB, fin, 0)


def _mlp(u2, w1, w2, x1, g2, gpost, rows_per_mod):
    t = u2.shape[0]
    tiles_per_mod = rows_per_mod // MLP_TM
    return pl.pallas_call(
        _mlp_kernel,
        grid=(t // MLP_TM, D_FF // MLP_TH),
        in_specs=[pl.BlockSpec((MLP_TM, D_MODEL), lambda i, j: (i, 0)),
                  pl.BlockSpec((D_MODEL, MLP_TH), lambda i, j: (0, j)),
                  pl.BlockSpec((MLP_TH, D_MODEL), lambda i, j: (j, 0)),
                  pl.BlockSpec(memory_space=pl.ANY),
                  pl.BlockSpec((None, 1, D_MODEL), lambda i, j: (i // tiles_per_mod, 0, 0)),
                  pl.BlockSpec((1, D_MODEL), lambda i, j: (0, 0))],
        out_specs=pl.BlockSpec((MLP_TM, D_MODEL), lambda i, j: (i, 0)),
        out_shape=jax.ShapeDtypeStruct((t, D_MODEL), F32),
        scratch_shapes=[pltpu.VMEM((MLP_TM, D_MODEL), F32), pltpu.SemaphoreType.DMA(())],
        compiler_params=_params(("parallel", "arbitrary"), 56 * 1024 * 1024),
        name="mlp",
    )(u2, w1, w2, x1, g2, gpost)


def _gate_row(i_vals, f_vals, dt_vals):
    v = jnp.concatenate([i_vals.reshape(-1), f_vals.reshape(-1), dt_vals.reshape(-1)]).astype(F32)
    return jnp.pad(v, (0, GATE_W - N_GATES)).reshape(1, GATE_W)


def _block(x, mods, state, weights, width):
    nb, seq, _ = x.shape
    nc = seq // CHUNK
    t = nb * seq
    x2d = x.reshape(t, D_MODEL)
    sh1, sc1, g1, sh2, sc2, g2 = mods
    rows_per_mod = t // sh1.shape[0]
    w = weights

    p, gc = _in_proj(x2d, sc1, sh1, w["g_pre_mix"], w["w_a"], w["w_b"], w["wg"], rows_per_mod)
    scans = _gate_scans(gc, w["gate_bias"], w["gate_alog"])
    xbc = _conv(p, w["conv_w9"], w["conv_b"], nb, seq, width, 512)

    if state is None:
        m_state = s_state = None
    else:
        c0, n0, m0, s0 = state
        m_state = (c0, n0.reshape(nb, N_UNITS, M_DQK),
                   jnp.pad(m0.reshape(nb, 1, N_UNITS), ((0, 0), (0, 0), (0, LANES - N_UNITS))))
        s_state = s0.reshape(nb, N_DIR, S_PAIRS, LANES, S_STATE)
    hm, c_new, n_new, m_new = _mlstm(p, scans, w["g_mlstm_norm"], m_state, nb, nc)
    ys, s_new = _ssd(p, xbc, scans, w["dskip"], w["g_ssd_norm"], s_state, nb, nc)

    x1, u2 = _out_proj(hm.reshape(t, MIX_M), ys.reshape(t, MIX_S), w["w_out"], x2d, g1,
                       w["g_post_mix"], w["g_pre_mlp"], sc2, sh2, rows_per_mod)
    y = _mlp(u2, w["w_mlp_in"], w["w_mlp_out"], x1, g2, w["g_post_mlp"], rows_per_mod)
    new_state = (c_new.reshape(nb, 1, N_DIR, M_HEADS, M_DQK, M_DV),
                 n_new.reshape(nb, 1, N_DIR, M_HEADS, M_DQK),
                 m_new[:, 0, :N_UNITS].reshape(nb, 1, N_DIR, M_HEADS),
                 s_new.reshape(nb, 1, N_DIR, S_HEADS, S_HEADDIM, S_STATE))
    return y.reshape(nb, seq, D_MODEL), new_state


def kernel(x_prompt, x_sample, state_mlstm_c, state_mlstm_n, state_mlstm_m, state_ssd, c, c_ctx, w_mod, b_mod,
           g_pre_mix, g_post_mix, w_in, b_igate, b_fgate, conv_w, conv_b, dt_bias, a_log, d_skip, g_mlstm_norm,
           g_ssd_norm, w_out, g_pre_mlp, g_post_mlp, w_mlp_in, w_mlp_out):
    assert w_mod.shape[0] == 1, "one layer"
    nb_s = x_sample.shape[0]

    cond8 = jnp.zeros((8, D_MODEL), F32).at[0].set(c_ctx).at[1:1 + nb_s].set(c)
    mod = _modulation(cond8, w_mod[0], b_mod[0].reshape(1, -1))
    mods = [mod[:, k * D_MODEL:(k + 1) * D_MODEL] for k in range(6)]
    mods_p = [m[0:1].reshape(1, 1, D_MODEL) for m in mods]
    mods_s = [m[1:1 + nb_s].reshape(nb_s, 1, D_MODEL) for m in mods]

    wi = w_in[0]
    sec = P_Z
    gate_cols = jnp.concatenate([wi[:, sec:sec + G_DT], wi[:, -N_DIR * S_HEADS:]], axis=1)
    gate_cols = jnp.pad(gate_cols, ((0, 0), (0, GATE_W - N_GATES))).astype(BF16)
    w_a = wi[:, :sec].astype(BF16)
    w_b = wi[:, sec + G_DT:sec + G_DT + MIX_S + XBC].astype(BF16)
    zeros_u = jnp.zeros((N_UNITS,), F32)
    row = lambda v: v.reshape(1, -1)
    weights = dict(
        w_a=w_a, w_b=w_b, wg=gate_cols,
        g_pre_mix=row(g_pre_mix[0]), g_post_mix=row(g_post_mix[0]),
        g_pre_mlp=row(g_pre_mlp[0]), g_post_mlp=row(g_post_mlp[0]),
        conv_w9=conv_w[0].reshape(9, XBC), conv_b=row(conv_b[0]),
        gate_bias=_gate_row(b_igate[0], b_fgate[0], dt_bias[0]),
        gate_alog=_gate_row(zeros_u, zeros_u, a_log[0]),
        dskip=row(jnp.repeat(d_skip[0], S_HEADDIM)),
        g_mlstm_norm=row(g_mlstm_norm[0]), g_ssd_norm=row(g_ssd_norm[0]),
        w_out=w_out[0].astype(BF16), w_mlp_in=w_mlp_in[0].astype(BF16), w_mlp_out=w_mlp_out[0].astype(BF16))

    y_p, st = _block(x_prompt, mods_p, None, weights, x_prompt.shape[1])
    cache = (state_mlstm_c[:, 0], state_mlstm_n[:, 0], state_mlstm_m[:, 0], state_ssd[:, 0])
    y_s, _ = _block(x_sample, mods_s, cache, weights, GRID_W)
    return (y_p, y_s) + st
```

```python
import functools

import jax
import jax.numpy as jnp
from jax import lax
from jax.experimental import pallas as pl
from jax.experimental.pallas import tpu as pltpu

F32 = jnp.float32
BF16 = jnp.bfloat16

D_MODEL = 2048
CHUNK = 128
N_DIR = 2
M_HEADS = 4
M_DQK = 128
M_DV = 256
MIX_M = M_HEADS * M_DV
S_HEADS = 16
S_HEADDIM = 64
S_STATE = 128
S_GROUPS = 4
S_REP = S_HEADS // S_GROUPS
MIX_S = S_HEADS * S_HEADDIM
XBC = MIX_S + 2 * S_GROUPS * S_STATE
D_FF = 4 * D_MODEL
GRID_W = 64
EPS = 1e-6
LANES = 128

P_Q = 0
P_K = M_HEADS * M_DQK
P_V = 2 * M_HEADS * M_DQK
P_O = P_V + MIX_M
P_Z = P_O + MIX_M
P_XBC = P_Z + MIX_S
P_MAIN = P_XBC + XBC
GATE_W = LANES
N_UNITS = N_DIR * M_HEADS
G_I = 0
G_F = N_UNITS
G_DT = 2 * N_UNITS
N_GATES = G_DT + N_DIR * S_HEADS
G_WK = G_DT + N_DIR * S_HEADS
assert G_WK + N_DIR * S_HEADS <= GATE_W

PAIR = LANES // S_HEADDIM
S_PAIRS = S_HEADS // PAIR
PAIRS_PER_GROUP = S_REP // PAIR

VMEM_LIMIT = 48 * 1024 * 1024

_NT = (((1,), (1,)), ((), ()))


def _params(sem, limit=VMEM_LIMIT):
    return pltpu.CompilerParams(dimension_semantics=sem, vmem_limit_bytes=limit)


def _silu(x):
    return x / (1.0 + jnp.exp(-x))


def _sigmoid(x):
    return 1.0 / (1.0 + jnp.exp(-x))


def _rms(x):
    return x * lax.rsqrt(jnp.mean(x * x, axis=-1, keepdims=True) + EPS)


def _dot(a, b, dims=None, precision=None):
    if dims is None:
        dims = (((a.ndim - 1,), (0,)), ((), ()))
    return lax.dot_general(a, b, dims, precision=precision, preferred_element_type=F32)


def _lane_bcast(tile, lane):
    return jnp.broadcast_to(tile[:, lane:lane + 1], tile.shape)


def _mod_kernel(c_ref, w_ref, b_ref, o_ref):
    a = _silu(c_ref[...]).astype(BF16)
    o_ref[...] = _dot(a, w_ref[...].astype(BF16)) + b_ref[...]


def _modulation(cond8, w_mod, b_mod):
    n = w_mod.shape[1]
    tn = 1024
    return pl.pallas_call(
        _mod_kernel,
        grid=(n // tn,),
        in_specs=[pl.BlockSpec((8, D_MODEL), lambda j: (0, 0)),
                  pl.BlockSpec((D_MODEL, tn), lambda j: (0, j)),
                  pl.BlockSpec((1, tn), lambda j: (0, j))],
        out_specs=pl.BlockSpec((8, tn), lambda j: (0, j)),
        out_shape=jax.ShapeDtypeStruct((8, n), F32),
        compiler_params=_params(("parallel",)),
        name="modulation",
    )(cond8, w_mod, b_mod)


IN_TM = 1024
IN_TN = 1024
IN_SUB = 128


def _inproj_kernel(x_ref, sc_ref, sh_ref, g_ref, wa_ref, wb_ref, wg_ref, o_ref, gc_ref, u_ref):
    j = pl.program_id(1)
    n_a = P_Z // IN_TN

    @pl.when(j == 0)
    def _():
        def body(r, carry):
            rows = pl.ds(pl.multiple_of(r * IN_SUB, IN_SUB), IN_SUB)
            y = _rms(x_ref[rows, :]) * g_ref[...]
            u_ref[rows, :] = (y * (1.0 + sc_ref[...]) + sh_ref[...]).astype(BF16)
            return carry
        lax.fori_loop(0, IN_TM // IN_SUB, body, 0)

    @pl.when(j < n_a)
    def _():
        o_ref[...] = _dot(u_ref[...], wa_ref[...])

    @pl.when(j >= n_a)
    def _():
        o_ref[...] = _dot(u_ref[...], wb_ref[...])

    @pl.when(j == pl.num_programs(1) - 1)
    def _():
        gc_ref[...] = _dot(u_ref[...], wg_ref[...])


def _in_proj(x2d, sc, sh, g, w_a, w_b, wg, rows_per_mod):
    t = x2d.shape[0]
    tiles_per_mod = rows_per_mod // IN_TM
    n_a = P_Z // IN_TN
    return pl.pallas_call(
        _inproj_kernel,
        grid=(t // IN_TM, P_MAIN // IN_TN),
        in_specs=[pl.BlockSpec((IN_TM, D_MODEL), lambda i, j: (i, 0)),
                  pl.BlockSpec((None, 1, D_MODEL), lambda i, j: (i // tiles_per_mod, 0, 0)),
                  pl.BlockSpec((None, 1, D_MODEL), lambda i, j: (i // tiles_per_mod, 0, 0)),
                  pl.BlockSpec((1, D_MODEL), lambda i, j: (0, 0)),
                  pl.BlockSpec((D_MODEL, IN_TN), lambda i, j: (0, jnp.minimum(j, n_a - 1))),
                  pl.BlockSpec((D_MODEL, IN_TN), lambda i, j: (0, jnp.maximum(j - n_a, 0))),
                  pl.BlockSpec((D_MODEL, GATE_W), lambda i, j: (0, 0))],
        out_specs=[pl.BlockSpec((IN_TM, IN_TN), lambda i, j: (i, j)),
                   pl.BlockSpec((IN_TM, GATE_W), lambda i, j: (i, 0))],
        out_shape=[jax.ShapeDtypeStruct((t, P_MAIN), F32),
                   jax.ShapeDtypeStruct((t, GATE_W), F32)],
        scratch_shapes=[pltpu.VMEM((IN_TM, D_MODEL), BF16)],
        compiler_params=_params(("parallel", "arbitrary"), 56 * 1024 * 1024),
        name="in_proj",
    )(x2d, sc, sh, g, w_a, w_b, wg)


def _conv_kernel(x_ref, w_ref, b_ref, o_ref, *, seq, width):
    x = x_ref[...]
    t = lax.broadcasted_iota(jnp.int32, x.shape, 0)
    c = jnp.bitwise_and(t, width - 1)
    xl = jnp.where(c >= 1, pltpu.roll(x, 1, 0), 0.0)
    xr = jnp.where(c <= width - 2, pltpu.roll(x, seq - 1, 0), 0.0)

    def taps(di):
        return w_ref[3 * di:3 * di + 1, :] * xl + w_ref[3 * di + 1:3 * di + 2, :] * x \
            + w_ref[3 * di + 2:3 * di + 3, :] * xr

    out = taps(1) + b_ref[...]
    if seq > width:
        out = out + jnp.where(t >= width, pltpu.roll(taps(0), width, 0), 0.0)
        out = out + jnp.where(t < seq - width, pltpu.roll(taps(2), seq - width, 0), 0.0)
    o_ref[...] = _silu(out)


CONV_BLOCK_ELEMS = 512 * 1024


def _conv(p, conv_w9, conv_b, nb, seq, width, cn):
    cn = min(cn, XBC)
    col0 = P_XBC // cn
    return pl.pallas_call(
        functools.partial(_conv_kernel, seq=seq, width=width),
        grid=(nb, XBC // cn),
        in_specs=[pl.BlockSpec((seq, cn), lambda b, j: (b, col0 + j)),
                  pl.BlockSpec((9, cn), lambda b, j: (0, j)),
                  pl.BlockSpec((1, cn), lambda b, j: (0, j))],
        out_specs=pl.BlockSpec((seq, cn), lambda b, j: (b, j)),
        out_shape=jax.ShapeDtypeStruct((nb * seq, XBC), F32),
        compiler_params=_params(("parallel", "parallel")),
        name="grid_conv",
    )(p, conv_w9, conv_b)


def _tri_masks():
    r = lax.broadcasted_iota(jnp.int32, (CHUNK, CHUNK), 0)
    c = lax.broadcasted_iota(jnp.int32, (CHUNK, CHUNK), 1)
    return r >= c, r <= c


SCAN_CHUNKS = 8


def _scan_kernel(gc_ref, bias_ref, alog_ref, col_ref, bcol_ref, row_ref):
    lower, upper = _tri_masks()
    lo, up = lower.astype(F32), upper.astype(F32)
    hi = lax.Precision.HIGHEST
    lane = lax.broadcasted_iota(jnp.int32, (CHUNK, GATE_W), 1)
    time = lax.broadcasted_iota(jnp.int32, (CHUNK, GATE_W), 0)
    lane1 = lax.broadcasted_iota(jnp.int32, (1, GATE_W), 1)

    def backward(l):
        unit_bwd = (l < G_DT) & (jnp.bitwise_and(l, N_UNITS - 1) >= M_HEADS)
        return unit_bwd | ((l >= G_DT + S_HEADS) & (l < N_GATES))

    is_bwd, is_bwd1 = backward(lane), backward(lane1)
    is_i = lane < G_F
    is_f = (lane >= G_F) & (lane < G_DT)
    is_dt = (lane >= G_DT) & (lane < N_GATES)
    neg_a = -jnp.exp(alog_ref[...])
    for c in range(SCAN_CHUNKS):
        rows = slice(c * CHUNK, (c + 1) * CHUNK)
        g = gc_ref[rows, :] + bias_ref[...]
        soft = jnp.log1p(jnp.exp(-jnp.abs(g)))
        logf = jnp.minimum(g, 0.0) - soft
        dt = jnp.maximum(g, 0.0) + soft
        x = jnp.where(is_f, logf, jnp.where(is_dt, dt * neg_a, 0.0))
        cs = jnp.where(is_bwd, _dot(up, x, precision=hi), _dot(lo, x, precision=hi))
        b_units = pltpu.roll(cs, GATE_W - G_F, 1)
        u = g - b_units
        cu = u
        k = 1
        while k < CHUNK:
            prev = jnp.where(time >= k, pltpu.roll(cu, k, 0), -jnp.inf)
            nxt = jnp.where(time < CHUNK - k, pltpu.roll(cu, CHUNK - k, 0), -jnp.inf)
            cu = jnp.maximum(cu, jnp.where(is_bwd, nxt, prev))
            k *= 2
        total = jnp.where(is_bwd1, cs[0:1, :], cs[CHUNK - 1:CHUNK, :])
        wk = jnp.exp(total - cs) * dt
        q = cs - jnp.log(dt)
        col_ref[rows, :] = jnp.where(is_i, cu, cs)
        bcol_ref[rows, :] = b_units
        row_src = jnp.where(is_i, u, jnp.where(is_dt, q, pltpu.roll(wk, G_WK - G_DT, 1)))
        row_ref[rows, :] = row_src.T


def _gate_scans(gc, bias, alog):
    t = gc.shape[0]
    tm = SCAN_CHUNKS * CHUNK
    row = lambda i: (i, 0)
    const = lambda i: (0, 0)
    return pl.pallas_call(
        _scan_kernel,
        grid=(t // tm,),
        in_specs=[pl.BlockSpec((tm, GATE_W), row), pl.BlockSpec((1, GATE_W), const),
                  pl.BlockSpec((1, GATE_W), const)],
        out_specs=[pl.BlockSpec((tm, GATE_W), row)] * 3,
        out_shape=[jax.ShapeDtypeStruct((t, GATE_W), F32)] * 3,
        compiler_params=_params(("parallel",)),
        name="gate_scans",
    )(gc, bias, alog)


def _mlstm_kernel(*refs, nc, zero_init):
    (qf, kf, vf, of, colf, bcolf, rowf, qb, kb, vb, ob, colb, bcolb, rowb, gn) = refs[:15]
    pos = 15
    if not zero_init:
        c0, n0, m0 = refs[pos:pos + 3]
        pos += 3
    hm_out, c_out, n_out, m_out, c_s, n_s, m_s, hpart, hbuf = refs[pos:]
    s = pl.program_id(1)
    half = nc // 2

    @pl.when(s == 0)
    def _():
        if zero_init:
            c_s[...] = jnp.zeros_like(c_s)
            n_s[...] = jnp.zeros_like(n_s)
            m_s[...] = jnp.zeros_like(m_s)
        else:
            c_s[...] = c0[...]
            m_s[...] = m0[...]
            for u in range(N_UNITS):
                n_s[u] = jnp.broadcast_to(n0[u:u + 1, :], (M_DQK, LANES)).T

    lower, upper = _tri_masks()
    lane1 = lax.broadcasted_iota(jnp.int32, (1, LANES), 1)
    ones16 = jnp.ones((CHUNK, LANES), BF16)
    per_dir = ((qf, kf, vf, of, colf, bcolf, rowf, lower, s),
               (qb, kb, vb, ob, colb, bcolb, rowb, upper, nc - 1 - s))
    for d, (q_ref, k_ref, v_ref, o_ref, col_ref, bcol_ref, row_ref, mask, chunk) in enumerate(per_dir):
        last = CHUNK - 1 if d == 0 else 0
        m_prev = m_s[...]
        stab = jnp.maximum(m_prev, col_ref[...])
        floor_c = jnp.exp(-(bcol_ref[...] + stab))
        w_c = jnp.exp(m_prev - stab)
        stab_last = stab[last:last + 1, :]
        m_new = bcol_ref[last:last + 1, :] + stab_last
        decay = w_c[last:last + 1, :]
        c_old = [c_s[d, h] for h in range(M_HEADS)]
        n_old = [n_s[d * M_HEADS + h] for h in range(M_HEADS)]
        c_new, n_new = [], []
        for h in range(M_HEADS):
            u = d * M_HEADS + h
            stab_b = _lane_bcast(stab, u)
            u_row = row_ref[u:u + 1, :]
            p = jnp.exp(jnp.where(mask, u_row - stab_b, -jnp.inf))
            qh = q_ref[:, h * M_DQK:(h + 1) * M_DQK] * (M_DQK ** -0.5)
            k_t = k_ref[:, h * M_DQK:(h + 1) * M_DQK].T
            v16 = v_ref[:, h * M_DV:(h + 1) * M_DV].astype(BF16)
            sc = _dot(qh.astype(BF16), k_t.astype(BF16)) * p
            lhs = jnp.concatenate([sc.astype(BF16), (qh * _lane_bcast(w_c, u)).astype(BF16)], axis=1)
            num = _dot(lhs, jnp.concatenate([v16, c_old[h].astype(BF16)], axis=0))
            den = _dot(lhs, jnp.concatenate([ones16, n_old[h].astype(BF16)], axis=0))
            inv = 1.0 / jnp.maximum(jnp.abs(den), _lane_bcast(floor_c, u))
            hbuf[:, h * M_DV:(h + 1) * M_DV] = num * jnp.concatenate([inv, inv], axis=1)

            kw_t = (k_t * jnp.exp(u_row - stab_last[:, u:u + 1])).astype(BF16)
            dec = decay[:, u:u + 1]
            c_new.append(dec * c_old[h] + _dot(kw_t, v16))
            n_new.append(dec * n_old[h] + _dot(kw_t, ones16))
        for h in range(M_HEADS):
            c_s[d, h] = c_new[h]
            n_s[d * M_HEADS + h] = n_new[h]
        mine = (lane1 >= d * M_HEADS) & (lane1 < (d + 1) * M_HEADS)
        m_s[...] = jnp.where(mine, m_new, m_prev)
        rows = pl.ds(pl.multiple_of(chunk * CHUNK, CHUNK), CHUNK)

        @pl.when(s < half)
        def _():
            hpart[rows, :] = hbuf[...]

        @pl.when(s >= half)
        def _():
            for h in range(M_HEADS):
                cols = slice(h * M_DV, (h + 1) * M_DV)
                tot = hbuf[:, cols] + hpart[rows, cols]
                hm_out[rows, cols] = (_rms(tot) * gn[:, cols] * _sigmoid(o_ref[:, cols])).astype(BF16)

    @pl.when(s == nc - 1)
    def _():
        c_out[...] = c_s[...]
        m_out[...] = m_s[...]
        for u in range(N_UNITS):
            n_out[u:u + 1, :] = n_s[u].T[0:1, :]


def _mlstm(p, scans, g_norm, state, nb, nc):
    zero_init = state is None
    seq = nc * CHUNK
    col, bcol, row = scans

    def fwd(cb):
        return lambda b, s: (b * nc + s, cb)

    def bwd(cb):
        return lambda b, s: (b * nc + nc - 1 - s, cb)

    def stream(mk):
        return [pl.BlockSpec((CHUNK, M_HEADS * M_DQK), mk(P_Q // (M_HEADS * M_DQK))),
                pl.BlockSpec((CHUNK, M_HEADS * M_DQK), mk(P_K // (M_HEADS * M_DQK))),
                pl.BlockSpec((CHUNK, MIX_M), mk(P_V // MIX_M)),
                pl.BlockSpec((CHUNK, MIX_M), mk(P_O // MIX_M)),
                pl.BlockSpec((CHUNK, GATE_W), mk(0)),
                pl.BlockSpec((CHUNK, GATE_W), mk(0)),
                pl.BlockSpec((CHUNK, GATE_W), mk(0))]

    in_specs = stream(fwd) + stream(bwd) + [pl.BlockSpec((1, MIX_M), lambda b, s: (0, 0))]
    args = [p] * 4 + [col, bcol, row] + [p] * 4 + [col, bcol, row, g_norm]
    state_specs = [pl.BlockSpec((None, N_DIR, M_HEADS, M_DQK, M_DV), lambda b, s: (b, 0, 0, 0, 0)),
                   pl.BlockSpec((None, N_UNITS, M_DQK), lambda b, s: (b, 0, 0)),
                   pl.BlockSpec((None, 1, LANES), lambda b, s: (b, 0, 0))]
    if not zero_init:
        in_specs += state_specs
        args += list(state)
    return pl.pallas_call(
        functools.partial(_mlstm_kernel, nc=nc, zero_init=zero_init),
        grid=(nb, nc),
        in_specs=in_specs,
        out_specs=[pl.BlockSpec((None, seq, MIX_M), lambda b, s: (b, 0, 0))] + state_specs,
        out_shape=[jax.ShapeDtypeStruct((nb, seq, MIX_M), BF16),
                   jax.ShapeDtypeStruct((nb, N_DIR, M_HEADS, M_DQK, M_DV), F32),
                   jax.ShapeDtypeStruct((nb, N_UNITS, M_DQK), F32),
                   jax.ShapeDtypeStruct((nb, 1, LANES), F32)],
        scratch_shapes=[pltpu.VMEM((N_DIR, M_HEADS, M_DQK, M_DV), F32),
                        pltpu.VMEM((N_UNITS, M_DQK, LANES), F32),
                        pltpu.VMEM((1, LANES), F32),
                        pltpu.VMEM((seq, MIX_M), F32),
                        pltpu.VMEM((CHUNK, MIX_M), F32)],
        compiler_params=_params(("parallel", "arbitrary")),
        name="mlstm",
    )(*args)


def _ssd_kernel(*refs, nc, zero_init):
    (xf, bf, cf, zf, colf, rowf, xb, bb, cb, zb, colb, rowb, dskip, gn) = refs[:14]
    pos = 14
    if not zero_init:
        s0 = refs[pos]
        pos += 1
    ys_out, s_out, st_s, ypart, ybuf = refs[pos:]
    s = pl.program_id(1)
    half = nc // 2

    @pl.when(s == 0)
    def _():
        if zero_init:
            st_s[...] = jnp.zeros_like(st_s)
        else:
            for d in range(N_DIR):
                for pr in range(S_PAIRS):
                    st_s[d, pr] = s0[d, pr].T

    lower, upper = _tri_masks()
    low_half = lax.broadcasted_iota(jnp.int32, (CHUNK, LANES), 1) < S_HEADDIM
    low_half1 = lax.broadcasted_iota(jnp.int32, (1, LANES), 1) < S_HEADDIM
    per_dir = ((xf, bf, cf, zf, colf, rowf, lower, s), (xb, bb, cb, zb, colb, rowb, upper, nc - 1 - s))
    for d, (x_ref, b_ref, c_ref, z_ref, col_ref, row_ref, mask, chunk) in enumerate(per_dir):
        last = CHUNK - 1 if d == 0 else 0
        cs_c = col_ref[...]
        total = cs_c[last:last + 1, :]
        st_old = [st_s[d, pr] for pr in range(S_PAIRS)]
        st_new = []
        for g in range(S_GROUPS):
            gs = slice(g * S_STATE, (g + 1) * S_STATE)
            cg = c_ref[:, gs].astype(BF16)
            b_f32 = b_ref[:, gs]
            cbm = _dot(cg, b_f32.astype(BF16), _NT)
            b_t = b_f32.T
            for pg in range(PAIRS_PER_GROUP):
                pr = g * PAIRS_PER_GROUP + pg
                cols = slice(pr * LANES, (pr + 1) * LANES)
                x16 = x_ref[:, cols].astype(BF16)
                zero16 = jnp.zeros_like(x16)
                halves = (jnp.where(low_half, x16, zero16), jnp.where(low_half, zero16, x16))
                state = st_old[pr]
                y_acc = jnp.zeros((CHUNK, LANES), F32)
                s_acc = jnp.zeros((S_STATE, LANES), F32)
                cs_b = []
                for e in range(PAIR):
                    gi = G_DT + d * S_HEADS + pr * PAIR + e
                    cs_b.append(_lane_bcast(cs_c, gi))
                    q_row = row_ref[gi:gi + 1, :]
                    wk_row = row_ref[gi + G_WK - G_DT:gi + G_WK - G_DT + 1, :]
                    mix = cbm * jnp.exp(jnp.where(mask, cs_b[e] - q_row, -jnp.inf))
                    y_acc = y_acc + _dot(mix.astype(BF16), halves[e])
                    s_acc = s_acc + _dot((b_t * wk_row).astype(BF16), halves[e])
                carry = jnp.exp(jnp.where(low_half, cs_b[0], cs_b[1]))
                ybuf[:, cols] = y_acc + _dot(cg, state.astype(BF16)) * carry
                gi0 = G_DT + d * S_HEADS + pr * PAIR
                dec = jnp.exp(jnp.where(low_half1, jnp.broadcast_to(total[:, gi0:gi0 + 1], (1, LANES)),
                                        jnp.broadcast_to(total[:, gi0 + 1:gi0 + 2], (1, LANES))))
                st_new.append(state * dec + s_acc)
        for pr in range(S_PAIRS):
            st_s[d, pr] = st_new[pr]
        rows = pl.ds(pl.multiple_of(chunk * CHUNK, CHUNK), CHUNK)

        @pl.when(s < half)
        def _():
            ypart[rows, :] = ybuf[...]

        @pl.when(s >= half)
        def _():
            y = ybuf[...] + ypart[rows, :] + dskip[...] * x_ref[...]
            ys_out[rows, :] = (_rms(y * _silu(z_ref[...])) * gn[...]).astype(BF16)

    @pl.when(s == nc - 1)
    def _():
        for d in range(N_DIR):
            for pr in range(S_PAIRS):
                s_out[d, pr] = st_s[d, pr].T


def _ssd(p, xbc, scans, dskip, g_norm, state, nb, nc):
    zero_init = state is None
    seq = nc * CHUNK
    bc_w = S_GROUPS * S_STATE
    col, _, row = scans

    def fwd(cb):
        return lambda b, s: (b * nc + s, cb)

    def bwd(cb):
        return lambda b, s: (b * nc + nc - 1 - s, cb)

    def stream(mk):
        return [pl.BlockSpec((CHUNK, MIX_S), mk(0)),
                pl.BlockSpec((CHUNK, bc_w), mk(MIX_S // bc_w)),
                pl.BlockSpec((CHUNK, bc_w), mk(MIX_S // bc_w + 1)),
                pl.BlockSpec((CHUNK, MIX_S), mk(P_Z // MIX_S)),
                pl.BlockSpec((CHUNK, GATE_W), mk(0)),
                pl.BlockSpec((CHUNK, GATE_W), mk(0))]

    const = lambda b, s: (0, 0)
    in_specs = stream(fwd) + stream(bwd) + [pl.BlockSpec((1, MIX_S), const), pl.BlockSpec((1, MIX_S), const)]
    args = [xbc, xbc, xbc, p, col, row] * 2 + [dskip, g_norm]
    state_spec = pl.BlockSpec((None, N_DIR, S_PAIRS, LANES, S_STATE), lambda b, s: (b, 0, 0, 0, 0))
    if not zero_init:
        in_specs.append(state_spec)
        args.append(state)
    return pl.pallas_call(
        functools.partial(_ssd_kernel, nc=nc, zero_init=zero_init),
        grid=(nb, nc),
        in_specs=in_specs,
        out_specs=[pl.BlockSpec((None, seq, MIX_S), lambda b, s: (b, 0, 0)), state_spec],
        out_shape=[jax.ShapeDtypeStruct((nb, seq, MIX_S), BF16),
                   jax.ShapeDtypeStruct((nb, N_DIR, S_PAIRS, LANES, S_STATE), F32)],
        scratch_shapes=[pltpu.VMEM((N_DIR, S_PAIRS, S_STATE, LANES), F32),
                        pltpu.VMEM((seq, MIX_S), F32),
                        pltpu.VMEM((CHUNK, MIX_S), F32)],
        compiler_params=_params(("parallel", "arbitrary")),
        name="ssd",
    )(*args)


OUT_TM = 512


def _outproj_kernel(hm_ref, ys_ref, w_ref, x_ref, g1_ref, gpost_ref, gpre_ref, sc_ref, sh_ref, x1_ref, u2_ref):
    mix = _dot(hm_ref[...], w_ref[0:MIX_M, :]) + _dot(ys_ref[...], w_ref[MIX_M:, :])
    x1 = x_ref[...] + g1_ref[...] * (_rms(mix) * gpost_ref[...])
    x1_ref[...] = x1
    u2_ref[...] = (_rms(x1) * gpre_ref[...] * (1.0 + sc_ref[...]) + sh_ref[...]).astype(BF16)


def _out_proj(hm, ys, w_out16, x2d, g1, gpost, gpre, sc2, sh2, rows_per_mod):
    t = x2d.shape[0]
    tiles_per_mod = rows_per_mod // OUT_TM
    row = lambda i: (i, 0)
    const = lambda i: (0, 0)
    mod = pl.BlockSpec((None, 1, D_MODEL), lambda i: (i // tiles_per_mod, 0, 0))
    return pl.pallas_call(
        _outproj_kernel,
        grid=(t // OUT_TM,),
        in_specs=[pl.BlockSpec((OUT_TM, MIX_M), row), pl.BlockSpec((OUT_TM, MIX_S), row),
                  pl.BlockSpec((D_MODEL, D_MODEL), const), pl.BlockSpec((OUT_TM, D_MODEL), row),
                  mod, pl.BlockSpec((1, D_MODEL), const), pl.BlockSpec((1, D_MODEL), const), mod, mod],
        out_specs=[pl.BlockSpec((OUT_TM, D_MODEL), row), pl.BlockSpec((OUT_TM, D_MODEL), row)],
        out_shape=[jax.ShapeDtypeStruct((t, D_MODEL), F32), jax.ShapeDtypeStruct((t, D_MODEL), BF16)],
        compiler_params=_params(("parallel",)),
        name="out_proj",
    )(hm, ys, w_out16, x2d, g1, gpost, gpre, sc2, sh2)


MLP_TM = 1024
MLP_TH = 512
MLP_SUB = 256


def _mlp_kernel(u_ref, w1_ref, w2_ref, x1_hbm, g2_ref, gpost_ref, o_ref, x1_buf, x1_sem):
    i = pl.program_id(0)
    j = pl.program_id(1)

    def x1_copy():
        rows = pl.ds(pl.multiple_of(i * MLP_TM, MLP_TM), MLP_TM)
        return pltpu.make_async_copy(x1_hbm.at[rows, :], x1_buf, x1_sem)

    @pl.when(j == 0)
    def _():
        x1_copy().start()
        o_ref[...] = jnp.zeros_like(o_ref)

    w1 = w1_ref[...].astype(BF16)
    w2 = w2_ref[...].astype(BF16)
    for r in range(MLP_TM // MLP_SUB):
        rows = slice(r * MLP_SUB, (r + 1) * MLP_SUB)
        hid = jnp.square(jnp.maximum(_dot(u_ref[rows, :], w1), 0.0)).astype(BF16)
        o_ref[rows, :] += _dot(hid, w2)

    @pl.when(j == pl.num_programs(1) - 1)
    def _():
        x1_copy().wait()

        def fin(r, carry):
            rows = pl.ds(pl.multiple_of(r * MLP_SUB, MLP_SUB), MLP_SUB)
            o_ref[rows, :] = x1_buf[rows, :] + g2_ref[...] * (_rms(o_ref[rows, :]) * gpost_ref[...])
            return carry
        lax.fori_loop(0, MLP_TM // MLP_SUB, fin, 0)


def _mlp(u2, w1, w2, x1, g2, gpost, rows_per_mod):
    t = u2.shape[0]
    tiles_per_mod = rows_per_mod // MLP_TM
    return pl.pallas_call(
        _mlp_kernel,
        grid=(t // MLP_TM, D_FF // MLP_TH),
        in_specs=[pl.BlockSpec((MLP_TM, D_MODEL), lambda i, j: (i, 0)),
                  pl.BlockSpec((D_MODEL, MLP_TH), lambda i, j: (0, j)),
                  pl.BlockSpec((MLP_TH, D_MODEL), lambda i, j: (j, 0)),
                  pl.BlockSpec(memory_space=pl.ANY),
                  pl.BlockSpec((None, 1, D_MODEL), lambda i, j: (i // tiles_per_mod, 0, 0)),
                  pl.BlockSpec((1, D_MODEL), lambda i, j: (0, 0))],
        out_specs=pl.BlockSpec((MLP_TM, D_MODEL), lambda i, j: (i, 0)),
        out_shape=jax.ShapeDtypeStruct((t, D_MODEL), F32),
        scratch_shapes=[pltpu.VMEM((MLP_TM, D_MODEL), F32), pltpu.SemaphoreType.DMA(())],
        compiler_params=_params(("parallel", "arbitrary"), 60 * 1024 * 1024),
        name="mlp",
    )(u2, w1, w2, x1, g2, gpost)


def _gate_row(i_vals, f_vals, dt_vals):
    v = jnp.concatenate([i_vals.reshape(-1), f_vals.reshape(-1), dt_vals.reshape(-1)]).astype(F32)
    return jnp.pad(v, (0, GATE_W - N_GATES)).reshape(1, GATE_W)


def _block(x, mods, state, weights, width):
    nb, seq, _ = x.shape
    nc = seq // CHUNK
    t = nb * seq
    x2d = x.reshape(t, D_MODEL)
    sh1, sc1, g1, sh2, sc2, g2 = mods
    rows_per_mod = t // sh1.shape[0]
    w = weights

    p, gc = _in_proj(x2d, sc1, sh1, w["g_pre_mix"], w["w_a"], w["w_b"], w["wg"], rows_per_mod)
    scans = _gate_scans(gc, w["gate_bias"], w["gate_alog"])
    xbc = _conv(p, w["conv_w9"], w["conv_b"], nb, seq, width, CONV_BLOCK_ELEMS // seq)

    if state is None:
        m_state = s_state = None
    else:
        c0, n0, m0, s0 = state
        m_state = (c0, n0.reshape(nb, N_UNITS, M_DQK),
                   jnp.pad(m0.reshape(nb, 1, N_UNITS), ((0, 0), (0, 0), (0, LANES - N_UNITS))))
        s_state = s0.reshape(nb, N_DIR, S_PAIRS, LANES, S_STATE)
    hm, c_new, n_new, m_new = _mlstm(p, scans, w["g_mlstm_norm"], m_state, nb, nc)
    ys, s_new = _ssd(p, xbc, scans, w["dskip"], w["g_ssd_norm"], s_state, nb, nc)

    x1, u2 = _out_proj(hm.reshape(t, MIX_M), ys.reshape(t, MIX_S), w["w_out"], x2d, g1,
                       w["g_post_mix"], w["g_pre_mlp"], sc2, sh2, rows_per_mod)
    y = _mlp(u2, w["w_mlp_in"], w["w_mlp_out"], x1, g2, w["g_post_mlp"], rows_per_mod)
    new_state = (c_new.reshape(nb, 1, N_DIR, M_HEADS, M_DQK, M_DV),
                 n_new.reshape(nb, 1, N_DIR, M_HEADS, M_DQK),
                 m_new[:, 0, :N_UNITS].reshape(nb, 1, N_DIR, M_HEADS),
                 s_new.reshape(nb, 1, N_DIR, S_HEADS, S_HEADDIM, S_STATE))
    return y.reshape(nb, seq, D_MODEL), new_state


def kernel(x_prompt, x_sample, state_mlstm_c, state_mlstm_n, state_mlstm_m, state_ssd, c, c_ctx, w_mod, b_mod,
           g_pre_mix, g_post_mix, w_in, b_igate, b_fgate, conv_w, conv_b, dt_bias, a_log, d_skip, g_mlstm_norm,
           g_ssd_norm, w_out, g_pre_mlp, g_post_mlp, w_mlp_in, w_mlp_out):
    assert w_mod.shape[0] == 1, "one layer"
    nb_s = x_sample.shape[0]

    cond8 = jnp.zeros((8, D_MODEL), F32).at[0].set(c_ctx).at[1:1 + nb_s].set(c)
    mod = _modulation(cond8, w_mod[0], b_mod[0].reshape(1, -1))
    mods = [mod[:, k * D_MODEL:(k + 1) * D_MODEL] for k in range(6)]
    mods_p = [m[0:1].reshape(1, 1, D_MODEL) for m in mods]
    mods_s = [m[1:1 + nb_s].reshape(nb_s, 1, D_MODEL) for m in mods]

    wi = w_in[0]
    sec = P_Z
    gate_cols = jnp.concatenate([wi[:, sec:sec + G_DT], wi[:, -N_DIR * S_HEADS:]], axis=1)
    gate_cols = jnp.pad(gate_cols, ((0, 0), (0, GATE_W - N_GATES))).astype(BF16)
    w_a = wi[:, :sec].astype(BF16)
    w_b = wi[:, sec + G_DT:sec + G_DT + MIX_S + XBC].astype(BF16)
    zeros_u = jnp.zeros((N_UNITS,), F32)
    row = lambda v: v.reshape(1, -1)
    weights = dict(
        w_a=w_a, w_b=w_b, wg=gate_cols,
        g_pre_mix=row(g_pre_mix[0]), g_post_mix=row(g_post_mix[0]),
        g_pre_mlp=row(g_pre_mlp[0]), g_post_mlp=row(g_post_mlp[0]),
        conv_w9=conv_w[0].reshape(9, XBC), conv_b=row(conv_b[0]),
        gate_bias=_gate_row(b_igate[0], b_fgate[0], dt_bias[0]),
        gate_alog=_gate_row(zeros_u, zeros_u, a_log[0]),
        dskip=row(jnp.repeat(d_skip[0], S_HEADDIM)),
        g_mlstm_norm=row(g_mlstm_norm[0]), g_ssd_norm=row(g_ssd_norm[0]),
        w_out=w_out[0].astype(BF16), w_mlp_in=w_mlp_in[0], w_mlp_out=w_mlp_out[0])

    y_p, st = _block(x_prompt, mods_p, None, weights, x_prompt.shape[1])
    cache = (state_mlstm_c[:, 0], state_mlstm_n[:, 0], state_mlstm_m[:, 0], state_ssd[:, 0])
    y_s, _ = _block(x_sample, mods_s, cache, weights, GRID_W)
    return (y_p, y_s) + st
```

```python
import functools

import jax
import jax.numpy as jnp
from jax import lax
from jax.experimental import pallas as pl
from jax.experimental.pallas import tpu as pltpu

F32 = jnp.float32
BF16 = jnp.bfloat16

D_MODEL = 2048
CHUNK = 128
N_DIR = 2
M_HEADS = 4
M_DQK = 128
M_DV = 256
MIX_M = M_HEADS * M_DV
S_HEADS = 16
S_HEADDIM = 64
S_STATE = 128
S_GROUPS = 4
S_REP = S_HEADS // S_GROUPS
MIX_S = S_HEADS * S_HEADDIM
XBC = MIX_S + 2 * S_GROUPS * S_STATE
D_FF = 4 * D_MODEL
GRID_W = 64
EPS = 1e-6
LANES = 128

P_Q = 0
P_K = M_HEADS * M_DQK
P_V = 2 * M_HEADS * M_DQK
P_O = P_V + MIX_M
P_Z = P_O + MIX_M
P_XBC = P_Z + MIX_S
P_MAIN = P_XBC + XBC
GATE_W = LANES
N_UNITS = N_DIR * M_HEADS
G_I = 0
G_F = N_UNITS
G_DT = 2 * N_UNITS
N_GATES = G_DT + N_DIR * S_HEADS
G_WK = G_DT + N_DIR * S_HEADS
G_CU = G_WK + N_DIR * S_HEADS
G_B = G_CU + N_UNITS
assert G_B + N_UNITS <= GATE_W

PAIR = LANES // S_HEADDIM
S_PAIRS = S_HEADS // PAIR
PAIRS_PER_GROUP = S_REP // PAIR

VMEM_LIMIT = 48 * 1024 * 1024

_NT = (((1,), (1,)), ((), ()))


def _params(sem, limit=VMEM_LIMIT):
    return pltpu.CompilerParams(dimension_semantics=sem, vmem_limit_bytes=limit)


def _silu(x):
    return x / (1.0 + jnp.exp(-x))


def _sigmoid(x):
    return 1.0 / (1.0 + jnp.exp(-x))


def _rms(x):
    return x * lax.rsqrt(jnp.mean(x * x, axis=-1, keepdims=True) + EPS)


def _dot(a, b, dims=None, precision=None):
    if dims is None:
        dims = (((a.ndim - 1,), (0,)), ((), ()))
    return lax.dot_general(a, b, dims, precision=precision, preferred_element_type=F32)


def _lane_bcast(tile, lane):
    return jnp.broadcast_to(tile[:, lane:lane + 1], tile.shape)


def _mod_kernel(c_ref, w_ref, b_ref, o_ref):
    a = _silu(c_ref[...]).astype(BF16)
    o_ref[...] = _dot(a, w_ref[...].astype(BF16)) + b_ref[...]


def _modulation(cond8, w_mod, b_mod):
    n = w_mod.shape[1]
    tn = 1024
    return pl.pallas_call(
        _mod_kernel,
        grid=(n // tn,),
        in_specs=[pl.BlockSpec((8, D_MODEL), lambda j: (0, 0)),
                  pl.BlockSpec((D_MODEL, tn), lambda j: (0, j)),
                  pl.BlockSpec((1, tn), lambda j: (0, j))],
        out_specs=pl.BlockSpec((8, tn), lambda j: (0, j)),
        out_shape=jax.ShapeDtypeStruct((8, n), F32),
        compiler_params=_params(("parallel",)),
        name="modulation",
    )(cond8, w_mod, b_mod)


IN_TM = 1024
IN_TN = 1024
IN_SUB = 128


def _inproj_kernel(x_ref, sc_ref, sh_ref, g_ref, wa_ref, wb_ref, wg_ref, o_ref, gc_ref, u_ref):
    j = pl.program_id(1)
    n_a = P_Z // IN_TN

    @pl.when(j == 0)
    def _():
        def body(r, carry):
            rows = pl.ds(pl.multiple_of(r * IN_SUB, IN_SUB), IN_SUB)
            y = _rms(x_ref[rows, :]) * g_ref[...]
            u_ref[rows, :] = (y * (1.0 + sc_ref[...]) + sh_ref[...]).astype(BF16)
            return carry
        lax.fori_loop(0, IN_TM // IN_SUB, body, 0)

    @pl.when(j < n_a)
    def _():
        o_ref[...] = _dot(u_ref[...], wa_ref[...].astype(BF16))

    @pl.when(j >= n_a)
    def _():
        o_ref[...] = _dot(u_ref[...], wb_ref[...])

    @pl.when(j == pl.num_programs(1) - 1)
    def _():
        gc_ref[...] = _dot(u_ref[...], wg_ref[...])


def _in_proj(x2d, sc, sh, g, w_a, w_b, wg, rows_per_mod):
    t = x2d.shape[0]
    tiles_per_mod = rows_per_mod // IN_TM
    n_a = P_Z // IN_TN
    return pl.pallas_call(
        _inproj_kernel,
        grid=(t // IN_TM, P_MAIN // IN_TN),
        in_specs=[pl.BlockSpec((IN_TM, D_MODEL), lambda i, j: (i, 0)),
                  pl.BlockSpec((None, 1, D_MODEL), lambda i, j: (i // tiles_per_mod, 0, 0)),
                  pl.BlockSpec((None, 1, D_MODEL), lambda i, j: (i // tiles_per_mod, 0, 0)),
                  pl.BlockSpec((1, D_MODEL), lambda i, j: (0, 0)),
                  pl.BlockSpec((D_MODEL, IN_TN), lambda i, j: (0, jnp.minimum(j, n_a - 1))),
                  pl.BlockSpec((D_MODEL, IN_TN), lambda i, j: (0, jnp.maximum(j - n_a, 0))),
                  pl.BlockSpec((D_MODEL, GATE_W), lambda i, j: (0, 0))],
        out_specs=[pl.BlockSpec((IN_TM, IN_TN), lambda i, j: (i, j)),
                   pl.BlockSpec((IN_TM, GATE_W), lambda i, j: (i, 0))],
        out_shape=[jax.ShapeDtypeStruct((t, P_MAIN), F32),
                   jax.ShapeDtypeStruct((t, GATE_W), F32)],
        scratch_shapes=[pltpu.VMEM((IN_TM, D_MODEL), BF16)],
        compiler_params=_params(("parallel", "arbitrary"), 60 * 1024 * 1024),
        name="in_proj",
    )(x2d, sc, sh, g, w_a, w_b, wg)


def _conv_kernel(x_ref, w_ref, b_ref, o_ref, *, seq, width):
    x = x_ref[...]
    t = lax.broadcasted_iota(jnp.int32, x.shape, 0)
    c = jnp.bitwise_and(t, width - 1)
    xl = jnp.where(c >= 1, pltpu.roll(x, 1, 0), 0.0)
    xr = jnp.where(c <= width - 2, pltpu.roll(x, seq - 1, 0), 0.0)

    def taps(di):
        return w_ref[3 * di:3 * di + 1, :] * xl + w_ref[3 * di + 1:3 * di + 2, :] * x \
            + w_ref[3 * di + 2:3 * di + 3, :] * xr

    out = taps(1) + b_ref[...]
    if seq > width:
        out = out + jnp.where(t >= width, pltpu.roll(taps(0), width, 0), 0.0)
        out = out + jnp.where(t < seq - width, pltpu.roll(taps(2), seq - width, 0), 0.0)
    o_ref[...] = _silu(out)


CONV_BLOCK_ELEMS = 512 * 1024


def _conv(p, conv_w9, conv_b, nb, seq, width, cn):
    cn = min(cn, XBC)
    col0 = P_XBC // cn
    return pl.pallas_call(
        functools.partial(_conv_kernel, seq=seq, width=width),
        grid=(nb, XBC // cn),
        in_specs=[pl.BlockSpec((seq, cn), lambda b, j: (b, col0 + j)),
                  pl.BlockSpec((9, cn), lambda b, j: (0, j)),
                  pl.BlockSpec((1, cn), lambda b, j: (0, j))],
        out_specs=pl.BlockSpec((seq, cn), lambda b, j: (b, j)),
        out_shape=jax.ShapeDtypeStruct((nb * seq, XBC), F32),
        compiler_params=_params(("parallel", "parallel")),
        name="grid_conv",
    )(p, conv_w9, conv_b)


def _tri_masks():
    r = lax.broadcasted_iota(jnp.int32, (CHUNK, CHUNK), 0)
    c = lax.broadcasted_iota(jnp.int32, (CHUNK, CHUNK), 1)
    return r >= c, r <= c


SCAN_CHUNKS = 8


def _scan_kernel(gc_ref, bias_ref, alog_ref, col_ref, bcol_ref, row_ref):
    lower, upper = _tri_masks()
    lo, up = lower.astype(F32), upper.astype(F32)
    hi = lax.Precision.HIGHEST
    lane = lax.broadcasted_iota(jnp.int32, (CHUNK, GATE_W), 1)
    time = lax.broadcasted_iota(jnp.int32, (CHUNK, GATE_W), 0)
    lane1 = lax.broadcasted_iota(jnp.int32, (1, GATE_W), 1)

    def backward(l):
        unit_bwd = (l < G_DT) & (jnp.bitwise_and(l, N_UNITS - 1) >= M_HEADS)
        return unit_bwd | ((l >= G_DT + S_HEADS) & (l < N_GATES))

    is_bwd, is_bwd1 = backward(lane), backward(lane1)
    is_i = lane < G_F
    is_f = (lane >= G_F) & (lane < G_DT)
    is_dt = (lane >= G_DT) & (lane < N_GATES)
    neg_a = -jnp.exp(alog_ref[...])
    for c in range(SCAN_CHUNKS):
        rows = slice(c * CHUNK, (c + 1) * CHUNK)
        g = gc_ref[rows, :] + bias_ref[...]
        soft = jnp.log1p(jnp.exp(-jnp.abs(g)))
        logf = jnp.minimum(g, 0.0) - soft
        dt = jnp.maximum(g, 0.0) + soft
        x = jnp.where(is_f, logf, jnp.where(is_dt, dt * neg_a, 0.0))
        cs = jnp.where(is_bwd, _dot(up, x, precision=hi), _dot(lo, x, precision=hi))
        b_units = pltpu.roll(cs, GATE_W - G_F, 1)
        u = g - b_units
        cu = u
        k = 1
        while k < CHUNK:
            prev = jnp.where(time >= k, pltpu.roll(cu, k, 0), -jnp.inf)
            nxt = jnp.where(time < CHUNK - k, pltpu.roll(cu, CHUNK - k, 0), -jnp.inf)
            cu = jnp.maximum(cu, jnp.where(is_bwd, nxt, prev))
            k *= 2
        total = jnp.where(is_bwd1, cs[0:1, :], cs[CHUNK - 1:CHUNK, :])
        wk = jnp.exp(total - cs) * dt
        q = cs - jnp.log(dt)
        col_ref[rows, :] = jnp.where(is_i, cu, cs)
        bcol_ref[rows, :] = b_units
        tail = jnp.where(lane < G_CU, pltpu.roll(wk, G_WK - G_DT, 1),
                         jnp.where(lane < G_B, pltpu.roll(cu, G_CU, 1), pltpu.roll(b_units, G_B, 1)))
        row_ref[rows, :] = jnp.where(is_i, u, jnp.where(is_dt, q, tail)).T


def _gate_scans(gc, bias, alog):
    t = gc.shape[0]
    tm = SCAN_CHUNKS * CHUNK
    row = lambda i: (i, 0)
    const = lambda i: (0, 0)
    return pl.pallas_call(
        _scan_kernel,
        grid=(t // tm,),
        in_specs=[pl.BlockSpec((tm, GATE_W), row), pl.BlockSpec((1, GATE_W), const),
                  pl.BlockSpec((1, GATE_W), const)],
        out_specs=[pl.BlockSpec((tm, GATE_W), row)] * 3,
        out_shape=[jax.ShapeDtypeStruct((t, GATE_W), F32)] * 3,
        compiler_params=_params(("parallel",)),
        name="gate_scans",
    )(gc, bias, alog)


N_ROWS = 16


def _mlstm_kernel(*refs, nc, zero_init):
    (qf, kf, vf, of, colf, bcolf, rowf, qb, kb, vb, ob, colb, bcolb, rowb, gn) = refs[:15]
    pos = 15
    if not zero_init:
        c0, n0, m0 = refs[pos:pos + 3]
        pos += 3
    hm_out, c_out, n_out, m_out, c_s, n_s, m_s, hpart, hbuf = refs[pos:]
    s = pl.program_id(1)
    half = nc // 2

    @pl.when(s == 0)
    def _():
        if zero_init:
            c_s[...] = jnp.zeros_like(c_s)
            n_s[...] = jnp.zeros_like(n_s)
            m_s[...] = jnp.zeros_like(m_s)
        else:
            c_s[...] = c0[...]
            m_s[...] = m0[...]
            for u in range(N_UNITS):
                n_s[u] = jnp.broadcast_to(n0[u:u + 1, :], (N_ROWS, M_DQK))

    lower, upper = _tri_masks()
    lane1 = lax.broadcasted_iota(jnp.int32, (1, LANES), 1)
    ones_rows = jnp.ones((N_ROWS, CHUNK), BF16)
    per_dir = ((qf, kf, vf, of, colf, bcolf, rowf, lower, s),
               (qb, kb, vb, ob, colb, bcolb, rowb, upper, nc - 1 - s))
    m_prev = m_s[...]
    m_next = m_prev
    c_new, n_new = [], []
    for d, (q_ref, k_ref, v_ref, o_ref, col_ref, bcol_ref, row_ref, mask, chunk) in enumerate(per_dir):
        last = CHUNK - 1 if d == 0 else 0
        stab = jnp.maximum(m_prev, col_ref[...])
        w_c = jnp.exp(m_prev - stab)
        stab_last = stab[last:last + 1, :]
        m_new = bcol_ref[last:last + 1, :] + stab_last
        decay = w_c[last:last + 1, :]
        c_old = [c_s[d, h] for h in range(M_HEADS)]
        n_old = [n_s[d * M_HEADS + h] for h in range(M_HEADS)]
        for h in range(M_HEADS):
            u = d * M_HEADS + h
            stab_b = _lane_bcast(stab, u)
            u_row = row_ref[u:u + 1, :]
            p = jnp.exp(jnp.where(mask, u_row - stab_b, -jnp.inf))
            qh = q_ref[:, h * M_DQK:(h + 1) * M_DQK] * (M_DQK ** -0.5)
            k_t = k_ref[:, h * M_DQK:(h + 1) * M_DQK].T
            v16 = v_ref[:, h * M_DV:(h + 1) * M_DV].astype(BF16)
            sc = _dot(qh.astype(BF16), k_t.astype(BF16)) * p
            lhs = jnp.concatenate([sc.astype(BF16), (qh * _lane_bcast(w_c, u)).astype(BF16)], axis=1)
            num = _dot(lhs, jnp.concatenate([v16, c_old[h].astype(BF16)], axis=0))
            den = _dot(jnp.concatenate([ones_rows, n_old[h].astype(BF16)], axis=1), lhs, _NT)[0:1, :]
            m_t = row_ref[G_B + u:G_B + u + 1, :] + jnp.maximum(m_prev[:, u:u + 1],
                                                                row_ref[G_CU + u:G_CU + u + 1, :])
            inv_row = 1.0 / jnp.maximum(jnp.abs(den), jnp.exp(-m_t))
            inv = jnp.broadcast_to(inv_row, (CHUNK, LANES)).T
            hbuf[d, :, h * M_DV:(h + 1) * M_DV] = num * jnp.concatenate([inv, inv], axis=1)

            kw_t = (k_t * jnp.exp(u_row - stab_last[:, u:u + 1])).astype(BF16)
            dec = decay[:, u:u + 1]
            c_new.append(dec * c_old[h] + _dot(kw_t, v16))
            n_new.append(dec * n_old[h] + _dot(ones_rows, kw_t, _NT))
        mine = (lane1 >= d * M_HEADS) & (lane1 < (d + 1) * M_HEADS)
        m_next = jnp.where(mine, m_new, m_next)
    for u in range(N_UNITS):
        c_s[u // M_HEADS, u % M_HEADS] = c_new[u]
        n_s[u] = n_new[u]
    m_s[...] = m_next

    for d, (_, _, _, o_ref, _, _, _, _, chunk) in enumerate(per_dir):
        rows = pl.ds(pl.multiple_of(chunk * CHUNK, CHUNK), CHUNK)

        @pl.when(s < half)
        def _():
            hpart[rows, :] = hbuf[d]

        @pl.when(s >= half)
        def _():
            for h in range(M_HEADS):
                cols = slice(h * M_DV, (h + 1) * M_DV)
                tot = hbuf[d, :, cols] + hpart[rows, cols]
                hm_out[rows, cols] = (_rms(tot) * gn[:, cols] * _sigmoid(o_ref[:, cols])).astype(BF16)

    @pl.when(s == nc - 1)
    def _():
        c_out[...] = c_s[...]
        m_out[...] = m_s[...]
        for u in range(N_UNITS):
            n_out[u:u + 1, :] = n_s[u][0:1, :]


def _mlstm(p, scans, g_norm, state, nb, nc):
    zero_init = state is None
    seq = nc * CHUNK
    col, bcol, row = scans

    def fwd(cb):
        return lambda b, s: (b * nc + s, cb)

    def bwd(cb):
        return lambda b, s: (b * nc + nc - 1 - s, cb)

    def stream(mk):
        return [pl.BlockSpec((CHUNK, M_HEADS * M_DQK), mk(P_Q // (M_HEADS * M_DQK))),
                pl.BlockSpec((CHUNK, M_HEADS * M_DQK), mk(P_K // (M_HEADS * M_DQK))),
                pl.BlockSpec((CHUNK, MIX_M), mk(P_V // MIX_M)),
                pl.BlockSpec((CHUNK, MIX_M), mk(P_O // MIX_M)),
                pl.BlockSpec((CHUNK, GATE_W), mk(0)),
                pl.BlockSpec((CHUNK, GATE_W), mk(0)),
                pl.BlockSpec((CHUNK, GATE_W), mk(0))]

    in_specs = stream(fwd) + stream(bwd) + [pl.BlockSpec((1, MIX_M), lambda b, s: (0, 0))]
    args = [p] * 4 + [col, bcol, row] + [p] * 4 + [col, bcol, row, g_norm]
    state_specs = [pl.BlockSpec((None, N_DIR, M_HEADS, M_DQK, M_DV), lambda b, s: (b, 0, 0, 0, 0)),
                   pl.BlockSpec((None, N_UNITS, M_DQK), lambda b, s: (b, 0, 0)),
                   pl.BlockSpec((None, 1, LANES), lambda b, s: (b, 0, 0))]
    if not zero_init:
        in_specs += state_specs
        args += list(state)
    return pl.pallas_call(
        functools.partial(_mlstm_kernel, nc=nc, zero_init=zero_init),
        grid=(nb, nc),
        in_specs=in_specs,
        out_specs=[pl.BlockSpec((None, seq, MIX_M), lambda b, s: (b, 0, 0))] + state_specs,
        out_shape=[jax.ShapeDtypeStruct((nb, seq, MIX_M), BF16),
                   jax.ShapeDtypeStruct((nb, N_DIR, M_HEADS, M_DQK, M_DV), F32),
                   jax.ShapeDtypeStruct((nb, N_UNITS, M_DQK), F32),
                   jax.ShapeDtypeStruct((nb, 1, LANES), F32)],
        scratch_shapes=[pltpu.VMEM((N_DIR, M_HEADS, M_DQK, M_DV), F32),
                        pltpu.VMEM((N_UNITS, N_ROWS, M_DQK), F32),
                        pltpu.VMEM((1, LANES), F32),
                        pltpu.VMEM((seq, MIX_M), F32),
                        pltpu.VMEM((N_DIR, CHUNK, MIX_M), F32)],
        compiler_params=_params(("parallel", "arbitrary")),
        name="mlstm",
    )(*args)


def _ssd_kernel(*refs, nc, zero_init):
    (xf, bf, cf, zf, colf, rowf, xb, bb, cb, zb, colb, rowb, dskip, gn) = refs[:14]
    pos = 14
    if not zero_init:
        s0 = refs[pos]
        pos += 1
    ys_out, s_out, st_s, ypart, ybuf = refs[pos:]
    s = pl.program_id(1)
    half = nc // 2

    @pl.when(s == 0)
    def _():
        if zero_init:
            st_s[...] = jnp.zeros_like(st_s)
        else:
            for d in range(N_DIR):
                for pr in range(S_PAIRS):
                    st_s[d, pr] = s0[d, pr].T

    lower, upper = _tri_masks()
    low_half = lax.broadcasted_iota(jnp.int32, (CHUNK, LANES), 1) < S_HEADDIM
    low_half1 = lax.broadcasted_iota(jnp.int32, (1, LANES), 1) < S_HEADDIM
    per_dir = ((xf, bf, cf, zf, colf, rowf, lower, s), (xb, bb, cb, zb, colb, rowb, upper, nc - 1 - s))
    st_new = []
    for d, (x_ref, b_ref, c_ref, z_ref, col_ref, row_ref, mask, chunk) in enumerate(per_dir):
        last = CHUNK - 1 if d == 0 else 0
        cs_c = col_ref[...]
        total = cs_c[last:last + 1, :]
        st_old = [st_s[d, pr] for pr in range(S_PAIRS)]
        for g in range(S_GROUPS):
            gs = slice(g * S_STATE, (g + 1) * S_STATE)
            cg = c_ref[:, gs].astype(BF16)
            b_f32 = b_ref[:, gs]
            cbm = _dot(cg, b_f32.astype(BF16), _NT)
            b_t = b_f32.T
            for pg in range(PAIRS_PER_GROUP):
                pr = g * PAIRS_PER_GROUP + pg
                cols = slice(pr * LANES, (pr + 1) * LANES)
                x16 = x_ref[:, cols].astype(BF16)
                zero16 = jnp.zeros_like(x16)
                halves = (jnp.where(low_half, x16, zero16), jnp.where(low_half, zero16, x16))
                state = st_old[pr]
                y_acc = jnp.zeros((CHUNK, LANES), F32)
                s_acc = jnp.zeros((S_STATE, LANES), F32)
                cs_b = []
                for e in range(PAIR):
                    gi = G_DT + d * S_HEADS + pr * PAIR + e
                    cs_b.append(_lane_bcast(cs_c, gi))
                    q_row = row_ref[gi:gi + 1, :]
                    wk_row = row_ref[gi + G_WK - G_DT:gi + G_WK - G_DT + 1, :]
                    mix = cbm * jnp.exp(jnp.where(mask, cs_b[e] - q_row, -jnp.inf))
                    y_acc = y_acc + _dot(mix.astype(BF16), halves[e])
                    s_acc = s_acc + _dot((b_t * wk_row).astype(BF16), halves[e])
                carry = jnp.exp(jnp.where(low_half, cs_b[0], cs_b[1]))
                ybuf[d, :, cols] = y_acc + _dot(cg, state.astype(BF16)) * carry
                gi0 = G_DT + d * S_HEADS + pr * PAIR
                dec = jnp.exp(jnp.where(low_half1, jnp.broadcast_to(total[:, gi0:gi0 + 1], (1, LANES)),
                                        jnp.broadcast_to(total[:, gi0 + 1:gi0 + 2], (1, LANES))))
                st_new.append(state * dec + s_acc)
    for d in range(N_DIR):
        for pr in range(S_PAIRS):
            st_s[d, pr] = st_new[d * S_PAIRS + pr]

    for d, (x_ref, _, _, z_ref, _, _, _, chunk) in enumerate(per_dir):
        rows = pl.ds(pl.multiple_of(chunk * CHUNK, CHUNK), CHUNK)

        @pl.when(s < half)
        def _():
            ypart[rows, :] = ybuf[d]

        @pl.when(s >= half)
        def _():
            y = ybuf[d] + ypart[rows, :] + dskip[...] * x_ref[...]
            ys_out[rows, :] = (_rms(y * _silu(z_ref[...])) * gn[...]).astype(BF16)

    @pl.when(s == nc - 1)
    def _():
        for d in range(N_DIR):
            for pr in range(S_PAIRS):
                s_out[d, pr] = st_s[d, pr].T


def _ssd(p, xbc, scans, dskip, g_norm, state, nb, nc):
    zero_init = state is None
    seq = nc * CHUNK
    bc_w = S_GROUPS * S_STATE
    col, _, row = scans

    def fwd(cb):
        return lambda b, s: (b * nc + s, cb)

    def bwd(cb):
        return lambda b, s: (b * nc + nc - 1 - s, cb)

    def stream(mk):
        return [pl.BlockSpec((CHUNK, MIX_S), mk(0)),
                pl.BlockSpec((CHUNK, bc_w), mk(MIX_S // bc_w)),
                pl.BlockSpec((CHUNK, bc_w), mk(MIX_S // bc_w + 1)),
                pl.BlockSpec((CHUNK, MIX_S), mk(P_Z // MIX_S)),
                pl.BlockSpec((CHUNK, GATE_W), mk(0)),
                pl.BlockSpec((CHUNK, GATE_W), mk(0))]

    const = lambda b, s: (0, 0)
    in_specs = stream(fwd) + stream(bwd) + [pl.BlockSpec((1, MIX_S), const), pl.BlockSpec((1, MIX_S), const)]
    args = [xbc, xbc, xbc, p, col, row] * 2 + [dskip, g_norm]
    state_spec = pl.BlockSpec((None, N_DIR, S_PAIRS, LANES, S_STATE), lambda b, s: (b, 0, 0, 0, 0))
    if not zero_init:
        in_specs.append(state_spec)
        args.append(state)
    return pl.pallas_call(
        functools.partial(_ssd_kernel, nc=nc, zero_init=zero_init),
        grid=(nb, nc),
        in_specs=in_specs,
        out_specs=[pl.BlockSpec((None, seq, MIX_S), lambda b, s: (b, 0, 0)), state_spec],
        out_shape=[jax.ShapeDtypeStruct((nb, seq, MIX_S), BF16),
                   jax.ShapeDtypeStruct((nb, N_DIR, S_PAIRS, LANES, S_STATE), F32)],
        scratch_shapes=[pltpu.VMEM((N_DIR, S_PAIRS, S_STATE, LANES), F32),
                        pltpu.VMEM((seq, MIX_S), F32),
                        pltpu.VMEM((N_DIR, CHUNK, MIX_S), F32)],
        compiler_params=_params(("parallel", "arbitrary")),
        name="ssd",
    )(*args)


OUT_TM = 512


def _outproj_kernel(hm_ref, ys_ref, w_ref, x_ref, g1_ref, gpost_ref, gpre_ref, sc_ref, sh_ref, x1_ref, u2_ref):
    mix = _dot(hm_ref[...], w_ref[0:MIX_M, :]) + _dot(ys_ref[...], w_ref[MIX_M:, :])
    x1 = x_ref[...] + g1_ref[...] * (_rms(mix) * gpost_ref[...])
    x1_ref[...] = x1
    u2_ref[...] = (_rms(x1) * gpre_ref[...] * (1.0 + sc_ref[...]) + sh_ref[...]).astype(BF16)


def _out_proj(hm, ys, w_out16, x2d, g1, gpost, gpre, sc2, sh2, rows_per_mod):
    t = x2d.shape[0]
    tiles_per_mod = rows_per_mod // OUT_TM
    row = lambda i: (i, 0)
    const = lambda i: (0, 0)
    mod = pl.BlockSpec((None, 1, D_MODEL), lambda i: (i // tiles_per_mod, 0, 0))
    return pl.pallas_call(
        _outproj_kernel,
        grid=(t // OUT_TM,),
        in_specs=[pl.BlockSpec((OUT_TM, MIX_M), row), pl.BlockSpec((OUT_TM, MIX_S), row),
                  pl.BlockSpec((D_MODEL, D_MODEL), const), pl.BlockSpec((OUT_TM, D_MODEL), row),
                  mod, pl.BlockSpec((1, D_MODEL), const), pl.BlockSpec((1, D_MODEL), const), mod, mod],
        out_specs=[pl.BlockSpec((OUT_TM, D_MODEL), row), pl.BlockSpec((OUT_TM, D_MODEL), row)],
        out_shape=[jax.ShapeDtypeStruct((t, D_MODEL), F32), jax.ShapeDtypeStruct((t, D_MODEL), BF16)],
        compiler_params=_params(("parallel",)),
        name="out_proj",
    )(hm, ys, w_out16, x2d, g1, gpost, gpre, sc2, sh2)


MLP_TM = 1024
MLP_TH = 512
MLP_SUB = 256


def _mlp_kernel(u_ref, w1_ref, w2_ref, x1_hbm, g2_ref, gpost_ref, o_ref, x1_buf, x1_sem):
    i = pl.program_id(0)
    j = pl.program_id(1)

    def x1_copy():
        rows = pl.ds(pl.multiple_of(i * MLP_TM, MLP_TM), MLP_TM)
        return pltpu.make_async_copy(x1_hbm.at[rows, :], x1_buf, x1_sem)

    @pl.when(j == 0)
    def _():
        x1_copy().start()
        o_ref[...] = jnp.zeros_like(o_ref)

    w1 = w1_ref[...].astype(BF16)
    w2 = w2_ref[...].astype(BF16)
    for r in range(MLP_TM // MLP_SUB):
        rows = slice(r * MLP_SUB, (r + 1) * MLP_SUB)
        hid = jnp.square(jnp.maximum(_dot(u_ref[rows, :], w1), 0.0)).astype(BF16)
        o_ref[rows, :] += _dot(hid, w2)

    @pl.when(j == pl.num_programs(1) - 1)
    def _():
        x1_copy().wait()

        def fin(r, carry):
            rows = pl.ds(pl.multiple_of(r * MLP_SUB, MLP_SUB), MLP_SUB)
            o_ref[rows, :] = x1_buf[rows, :] + g2_ref[...] * (_rms(o_ref[rows, :]) * gpost_ref[...])
            return carry
        lax.fori_loop(0, MLP_TM // MLP_SUB, fin, 0)


def _mlp(u2, w1, w2, x1, g2, gpost, rows_per_mod):
    t = u2.shape[0]
    tiles_per_mod = rows_per_mod // MLP_TM
    return pl.pallas_call(
        _mlp_kernel,
        grid=(t // MLP_TM, D_FF // MLP_TH),
        in_specs=[pl.BlockSpec((MLP_TM, D_MODEL), lambda i, j: (i, 0)),
                  pl.BlockSpec((D_MODEL, MLP_TH), lambda i, j: (0, j)),
                  pl.BlockSpec((MLP_TH, D_MODEL), lambda i, j: (j, 0)),
                  pl.BlockSpec(memory_space=pl.ANY),
                  pl.BlockSpec((None, 1, D_MODEL), lambda i, j: (i // tiles_per_mod, 0, 0)),
                  pl.BlockSpec((1, D_MODEL), lambda i, j: (0, 0))],
        out_specs=pl.BlockSpec((MLP_TM, D_MODEL), lambda i, j: (i, 0)),
        out_shape=jax.ShapeDtypeStruct((t, D_MODEL), F32),
        scratch_shapes=[pltpu.VMEM((MLP_TM, D_MODEL), F32), pltpu.SemaphoreType.DMA(())],
        compiler_params=_params(("parallel", "arbitrary"), 60 * 1024 * 1024),
        name="mlp",
    )(u2, w1, w2, x1, g2, gpost)


def _gate_row(i_vals, f_vals, dt_vals):
    v = jnp.concatenate([i_vals.reshape(-1), f_vals.reshape(-1), dt_vals.reshape(-1)]).astype(F32)
    return jnp.pad(v, (0, GATE_W - N_GATES)).reshape(1, GATE_W)


def _block(x, mods, state, weights, width):
    nb, seq, _ = x.shape
    nc = seq // CHUNK
    t = nb * seq
    x2d = x.reshape(t, D_MODEL)
    sh1, sc1, g1, sh2, sc2, g2 = mods
    rows_per_mod = t // sh1.shape[0]
    w = weights

    p, gc = _in_proj(x2d, sc1, sh1, w["g_pre_mix"], w["w_a"], w["w_b"], w["wg"], rows_per_mod)
    scans = _gate_scans(gc, w["gate_bias"], w["gate_alog"])
    xbc = _conv(p, w["conv_w9"], w["conv_b"], nb, seq, width, CONV_BLOCK_ELEMS // seq)

    if state is None:
        m_state = s_state = None
    else:
        c0, n0, m0, s0 = state
        m_state = (c0, n0.reshape(nb, N_UNITS, M_DQK),
                   jnp.pad(m0.reshape(nb, 1, N_UNITS), ((0, 0), (0, 0), (0, LANES - N_UNITS))))
        s_state = s0.reshape(nb, N_DIR, S_PAIRS, LANES, S_STATE)
    hm, c_new, n_new, m_new = _mlstm(p, scans, w["g_mlstm_norm"], m_state, nb, nc)
    ys, s_new = _ssd(p, xbc, scans, w["dskip"], w["g_ssd_norm"], s_state, nb, nc)

    x1, u2 = _out_proj(hm.reshape(t, MIX_M), ys.reshape(t, MIX_S), w["w_out"], x2d, g1,
                       w["g_post_mix"], w["g_pre_mlp"], sc2, sh2, rows_per_mod)
    y = _mlp(u2, w["w_mlp_in"], w["w_mlp_out"], x1, g2, w["g_post_mlp"], rows_per_mod)
    new_state = (c_new.reshape(nb, 1, N_DIR, M_HEADS, M_DQK, M_DV),
                 n_new.reshape(nb, 1, N_DIR, M_HEADS, M_DQK),
                 m_new[:, 0, :N_UNITS].reshape(nb, 1, N_DIR, M_HEADS),
                 s_new.reshape(nb, 1, N_DIR, S_HEADS, S_HEADDIM, S_STATE))
    return y.reshape(nb, seq, D_MODEL), new_state


def kernel(x_prompt, x_sample, state_mlstm_c, state_mlstm_n, state_mlstm_m, state_ssd, c, c_ctx, w_mod, b_mod,
           g_pre_mix, g_post_mix, w_in, b_igate, b_fgate, conv_w, conv_b, dt_bias, a_log, d_skip, g_mlstm_norm,
           g_ssd_norm, w_out, g_pre_mlp, g_post_mlp, w_mlp_in, w_mlp_out):
    assert w_mod.shape[0] == 1, "one layer"
    nb_s = x_sample.shape[0]

    cond8 = jnp.zeros((8, D_MODEL), F32).at[0].set(c_ctx).at[1:1 + nb_s].set(c)
    mod = _modulation(cond8, w_mod[0], b_mod[0].reshape(1, -1))
    mods = [mod[:, k * D_MODEL:(k + 1) * D_MODEL] for k in range(6)]
    mods_p = [m[0:1].reshape(1, 1, D_MODEL) for m in mods]
    mods_s = [m[1:1 + nb_s].reshape(nb_s, 1, D_MODEL) for m in mods]

    wi = w_in[0]
    sec = P_Z
    gate_cols = jnp.concatenate([wi[:, sec:sec + G_DT], wi[:, -N_DIR * S_HEADS:]], axis=1)
    gate_cols = jnp.pad(gate_cols, ((0, 0), (0, GATE_W - N_GATES))).astype(BF16)
    w_a = wi
    w_b =wi[:, sec + G_DT:sec + G_DT + MIX_S + XBC].astype(BF16)
    zeros_u = jnp.zeros((N_UNITS,), F32)
    row = lambda v: v.reshape(1, -1)
    weights = dict(
        w_a=w_a, w_b=w_b, wg=gate_cols,
        g_pre_mix=row(g_pre_mix[0]), g_post_mix=row(g_post_mix[0]),
        g_pre_mlp=row(g_pre_mlp[0]), g_post_mlp=row(g_post_mlp[0]),
        conv_w9=conv_w[0].reshape(9, XBC), conv_b=row(conv_b[0]),
        gate_bias=_gate_row(b_igate[0], b_fgate[0], dt_bias[0]),
        gate_alog=_gate_row(zeros_u, zeros_u, a_log[0]),
        dskip=row(jnp.repeat(d_skip[0], S_HEADDIM)),
        g_mlstm_norm=row(g_mlstm_norm[0]), g_ssd_norm=row(g_ssd_norm[0]),
        w_out=w_out[0].astype(BF16), w_mlp_in=w_mlp_in[0], w_mlp_out=w_mlp_out[0])

    y_p, st = _block(x_prompt, mods_p, None, weights, x_prompt.shape[1])
    cache = (state_mlstm_c[:, 0], state_mlstm_n[:, 0], state_mlstm_m[:, 0], state_ssd[:, 0])
    y_s, _ = _block(x_sample, mods_s, cache, weights, GRID_W)
    return (y_p, y_s) + st
```

```python
import functools

import jax
import jax.numpy as jnp
from jax import lax
from jax.experimental import pallas as pl
from jax.experimental.pallas import tpu as pltpu

F32 = jnp.float32
BF16 = jnp.bfloat16

D_MODEL = 2048
CHUNK = 128
N_DIR = 2
M_HEADS = 4
M_DQK = 128
M_DV = 256
MIX_M = M_HEADS * M_DV
S_HEADS = 16
S_HEADDIM = 64
S_STATE = 128
S_GROUPS = 4
S_REP = S_HEADS // S_GROUPS
MIX_S = S_HEADS * S_HEADDIM
XBC = MIX_S + 2 * S_GROUPS * S_STATE
D_FF = 4 * D_MODEL
GRID_W = 64
EPS = 1e-6
LANES = 128

P_Q = 0
P_K = M_HEADS * M_DQK
P_V = 2 * M_HEADS * M_DQK
P_O = P_V + MIX_M
P_Z = P_O + MIX_M
P_XBC = P_Z + MIX_S
P_MAIN = P_XBC + XBC
GATE_W = LANES
N_UNITS = N_DIR * M_HEADS
G_I = 0
G_F = N_UNITS
G_DT = 2 * N_UNITS
N_GATES = G_DT + N_DIR * S_HEADS
G_WK = G_DT + N_DIR * S_HEADS
G_CU = G_WK + N_DIR * S_HEADS
G_B = G_CU + N_UNITS
assert G_B + N_UNITS <= GATE_W

PAIR = LANES // S_HEADDIM
S_PAIRS = S_HEADS // PAIR
PAIRS_PER_GROUP = S_REP // PAIR

VMEM_LIMIT = 48 * 1024 * 1024

_NT = (((1,), (1,)), ((), ()))


def _params(sem, limit=VMEM_LIMIT):
    return pltpu.CompilerParams(dimension_semantics=sem, vmem_limit_bytes=limit)


def _silu(x):
    return x / (1.0 + jnp.exp(-x))


def _sigmoid(x):
    return 1.0 / (1.0 + jnp.exp(-x))


def _rms(x):
    return x * lax.rsqrt(jnp.mean(x * x, axis=-1, keepdims=True) + EPS)


def _dot(a, b, dims=None, precision=None):
    if dims is None:
        dims = (((a.ndim - 1,), (0,)), ((), ()))
    return lax.dot_general(a, b, dims, precision=precision, preferred_element_type=F32)


def _lane_bcast(tile, lane):
    return jnp.broadcast_to(tile[:, lane:lane + 1], tile.shape)


def _mod_kernel(c_ref, w_ref, b_ref, o_ref):
    a = _silu(c_ref[...]).astype(BF16)
    o_ref[...] = _dot(a, w_ref[...].astype(BF16)) + b_ref[...]


def _modulation(cond8, w_mod, b_mod):
    n = w_mod.shape[1]
    tn = 1024
    return pl.pallas_call(
        _mod_kernel,
        grid=(n // tn,),
        in_specs=[pl.BlockSpec((8, D_MODEL), lambda j: (0, 0)),
                  pl.BlockSpec((D_MODEL, tn), lambda j: (0, j)),
                  pl.BlockSpec((1, tn), lambda j: (0, j))],
        out_specs=pl.BlockSpec((8, tn), lambda j: (0, j)),
        out_shape=jax.ShapeDtypeStruct((8, n), F32),
        compiler_params=_params(("parallel",)),
        name="modulation",
    )(cond8, w_mod, b_mod)


PACK_TM = 256


def _pack_kernel(w_ref, edge_ref, wa_ref, wb_ref, wg_ref):
    wa_ref[...] = w_ref[:, :P_Z].astype(BF16)
    lane = lax.broadcasted_iota(jnp.int32, (PACK_TM, LANES), 1)
    first = P_Z // LANES
    n_blocks = (P_MAIN - P_Z) // LANES

    def block(k):
        return edge_ref[...] if k == first + n_blocks else w_ref[:, k * LANES:(k + 1) * LANES]

    prev = block(first)
    wg_ref[...] = jnp.where(lane < G_DT, prev, jnp.where(lane < N_GATES, edge_ref[...], 0.0)).astype(BF16)
    for g in range(n_blocks):
        nxt = block(first + g + 1)
        stitched = jnp.where(lane < LANES - G_DT, pltpu.roll(prev, LANES - G_DT, 1),
                             pltpu.roll(nxt, LANES - G_DT, 1))
        wb_ref[:, g * LANES:(g + 1) * LANES] = stitched.astype(BF16)
        prev = nxt


def _pack_w_in(w_in):
    k, n = w_in.shape
    assert n == P_MAIN + N_GATES and n // LANES == P_MAIN // LANES
    width = P_MAIN - P_Z
    return pl.pallas_call(
        _pack_kernel,
        grid=(k // PACK_TM,),
        in_specs=[pl.BlockSpec((PACK_TM, n), lambda i: (i, 0)),
                  pl.BlockSpec((PACK_TM, LANES), lambda i: (i, n // LANES))],
        out_specs=[pl.BlockSpec((PACK_TM, P_Z), lambda i: (i, 0)),
                   pl.BlockSpec((PACK_TM, width), lambda i: (i, 0)),
                   pl.BlockSpec((PACK_TM, GATE_W), lambda i: (i, 0))],
        out_shape=[jax.ShapeDtypeStruct((k, P_Z), BF16), jax.ShapeDtypeStruct((k, width), BF16),
                   jax.ShapeDtypeStruct((k, GATE_W), BF16)],
        compiler_params=_params(("parallel",)),
        name="pack_w_in",
    )(w_in, w_in)


IN_TM = 1024
IN_TN = 1024
IN_SUB = 128


def _inproj_kernel(x_ref, sc_ref, sh_ref, g_ref, wa_ref, wb_ref, wg_ref, o_ref, gc_ref, u_ref):
    j = pl.program_id(1)
    n_a = P_Z // IN_TN

    @pl.when(j == 0)
    def _():
        def body(r, carry):
            rows = pl.ds(pl.multiple_of(r * IN_SUB, IN_SUB), IN_SUB)
            y = _rms(x_ref[rows, :]) * g_ref[...]
            u_ref[rows, :] = (y * (1.0 + sc_ref[...]) + sh_ref[...]).astype(BF16)
            return carry
        lax.fori_loop(0, IN_TM // IN_SUB, body, 0)

    @pl.when(j < n_a)
    def _():
        o_ref[...] = _dot(u_ref[...], wa_ref[...])

    @pl.when(j >= n_a)
    def _():
        o_ref[...] = _dot(u_ref[...], wb_ref[...])

    @pl.when(j == pl.num_programs(1) - 1)
    def _():
        gc_ref[...] = _dot(u_ref[...], wg_ref[...])


def _in_proj(x2d, sc, sh, g, w_a, w_b, wg, rows_per_mod):
    t = x2d.shape[0]
    tiles_per_mod = rows_per_mod // IN_TM
    n_a = P_Z // IN_TN
    return pl.pallas_call(
        _inproj_kernel,
        grid=(t // IN_TM, P_MAIN // IN_TN),
        in_specs=[pl.BlockSpec((IN_TM, D_MODEL), lambda i, j: (i, 0)),
                  pl.BlockSpec((None, 1, D_MODEL), lambda i, j: (i // tiles_per_mod, 0, 0)),
                  pl.BlockSpec((None, 1, D_MODEL), lambda i, j: (i // tiles_per_mod, 0, 0)),
                  pl.BlockSpec((1, D_MODEL), lambda i, j: (0, 0)),
                  pl.BlockSpec((D_MODEL, IN_TN), lambda i, j: (0, jnp.minimum(j, n_a - 1))),
                  pl.BlockSpec((D_MODEL, IN_TN), lambda i, j: (0, jnp.maximum(j - n_a, 0))),
                  pl.BlockSpec((D_MODEL, GATE_W), lambda i, j: (0, 0))],
        out_specs=[pl.BlockSpec((IN_TM, IN_TN), lambda i, j: (i, j)),
                   pl.BlockSpec((IN_TM, GATE_W), lambda i, j: (i, 0))],
        out_shape=[jax.ShapeDtypeStruct((t, P_MAIN), F32),
                   jax.ShapeDtypeStruct((t, GATE_W), F32)],
        scratch_shapes=[pltpu.VMEM((IN_TM, D_MODEL), BF16)],
        compiler_params=_params(("parallel", "arbitrary"), 60 * 1024 * 1024),
        name="in_proj",
    )(x2d, sc, sh, g, w_a, w_b, wg)


def _conv_kernel(x_ref, w_ref, b_ref, o_ref, *, seq, width):
    x = x_ref[...]
    t = lax.broadcasted_iota(jnp.int32, x.shape, 0)
    c = jnp.bitwise_and(t, width - 1)
    xl = jnp.where(c >= 1, pltpu.roll(x, 1, 0), 0.0)
    xr = jnp.where(c <= width - 2, pltpu.roll(x, seq - 1, 0), 0.0)

    def taps(di):
        return w_ref[3 * di:3 * di + 1, :] * xl + w_ref[3 * di + 1:3 * di + 2, :] * x \
            + w_ref[3 * di + 2:3 * di + 3, :] * xr

    out = taps(1) + b_ref[...]
    if seq > width:
        out = out + jnp.where(t >= width, pltpu.roll(taps(0), width, 0), 0.0)
        out = out + jnp.where(t < seq - width, pltpu.roll(taps(2), seq - width, 0), 0.0)
    o_ref[...] = _silu(out)


CONV_BLOCK_ELEMS = 512 * 1024


def _conv(p, conv_w9, conv_b, nb, seq, width, cn):
    cn = min(cn, XBC)
    col0 = P_XBC // cn
    return pl.pallas_call(
        functools.partial(_conv_kernel, seq=seq, width=width),
        grid=(nb, XBC // cn),
        in_specs=[pl.BlockSpec((seq, cn), lambda b, j: (b, col0 + j)),
                  pl.BlockSpec((9, cn), lambda b, j: (0, j)),
                  pl.BlockSpec((1, cn), lambda b, j: (0, j))],
        out_specs=pl.BlockSpec((seq, cn), lambda b, j: (b, j)),
        out_shape=jax.ShapeDtypeStruct((nb * seq, XBC), F32),
        compiler_params=_params(("parallel", "parallel")),
        name="grid_conv",
    )(p, conv_w9, conv_b)


def _tri_masks():
    r = lax.broadcasted_iota(jnp.int32, (CHUNK, CHUNK), 0)
    c = lax.broadcasted_iota(jnp.int32, (CHUNK, CHUNK), 1)
    return r >= c, r <= c


SCAN_CHUNKS = 8


def _scan_kernel(gc_ref, bias_ref, alog_ref, col_ref, bcol_ref, row_ref):
    lower, upper = _tri_masks()
    lo, up = lower.astype(F32), upper.astype(F32)
    hi = lax.Precision.HIGHEST
    lane = lax.broadcasted_iota(jnp.int32, (CHUNK, GATE_W), 1)
    time = lax.broadcasted_iota(jnp.int32, (CHUNK, GATE_W), 0)
    lane1 = lax.broadcasted_iota(jnp.int32, (1, GATE_W), 1)

    def backward(l):
        unit_bwd = (l < G_DT) & (jnp.bitwise_and(l, N_UNITS - 1) >= M_HEADS)
        return unit_bwd | ((l >= G_DT + S_HEADS) & (l < N_GATES))

    is_bwd, is_bwd1 = backward(lane), backward(lane1)
    is_i = lane < G_F
    is_f = (lane >= G_F) & (lane < G_DT)
    is_dt = (lane >= G_DT) & (lane < N_GATES)
    neg_a = -jnp.exp(alog_ref[...])
    for c in range(SCAN_CHUNKS):
        rows = slice(c * CHUNK, (c + 1) * CHUNK)
        g = gc_ref[rows, :] + bias_ref[...]
        soft = jnp.log1p(jnp.exp(-jnp.abs(g)))
        logf = jnp.minimum(g, 0.0) - soft
        dt = jnp.maximum(g, 0.0) + soft
        x = jnp.where(is_f, logf, jnp.where(is_dt, dt * neg_a, 0.0))
        cs = jnp.where(is_bwd, _dot(up, x, precision=hi), _dot(lo, x, precision=hi))
        b_units = pltpu.roll(cs, GATE_W - G_F, 1)
        u = g - b_units
        cu = u
        k = 1
        while k < CHUNK:
            prev = jnp.where(time >= k, pltpu.roll(cu, k, 0), -jnp.inf)
            nxt = jnp.where(time < CHUNK - k, pltpu.roll(cu, CHUNK - k, 0), -jnp.inf)
            cu = jnp.maximum(cu, jnp.where(is_bwd, nxt, prev))
            k *= 2
        total = jnp.where(is_bwd1, cs[0:1, :], cs[CHUNK - 1:CHUNK, :])
        wk = jnp.exp(total - cs) * dt
        q = cs - jnp.log(dt)
        col_ref[rows, :] = jnp.where(is_i, cu, cs)
        bcol_ref[rows, :] = b_units
        tail = jnp.where(lane < G_CU, pltpu.roll(wk, G_WK - G_DT, 1),
                         jnp.where(lane < G_B, pltpu.roll(cu, G_CU, 1), pltpu.roll(b_units, G_B, 1)))
        row_ref[rows, :] = jnp.where(is_i, u, jnp.where(is_dt, q, tail)).T


def _gate_scans(gc, bias, alog):
    t = gc.shape[0]
    tm = SCAN_CHUNKS * CHUNK
    row = lambda i: (i, 0)
    const = lambda i: (0, 0)
    return pl.pallas_call(
        _scan_kernel,
        grid=(t // tm,),
        in_specs=[pl.BlockSpec((tm, GATE_W), row), pl.BlockSpec((1, GATE_W), const),
                  pl.BlockSpec((1, GATE_W), const)],
        out_specs=[pl.BlockSpec((tm, GATE_W), row)] * 3,
        out_shape=[jax.ShapeDtypeStruct((t, GATE_W), F32)] * 3,
        compiler_params=_params(("parallel",)),
        name="gate_scans",
    )(gc, bias, alog)


N_ROWS = 16
_WORK, _DONE = "work", "done"


def _mlstm_phases(refs, nc, zero_init):
    (qf, kf, vf, of, colf, bcolf, rowf, qb, kb, vb, ob, colb, bcolb, rowb, gn) = refs[:15]
    pos = 15
    if not zero_init:
        c0, n0, m0 = refs[pos:pos + 3]
        pos += 3
    hm_out, c_out, n_out, m_out, c_s, n_s, m_s, hpart, hbuf = refs[pos:]
    s = pl.program_id(1)
    half = nc // 2

    @pl.when(s == 0)
    def _():
        if zero_init:
            c_s[...] = jnp.zeros_like(c_s)
            n_s[...] = jnp.zeros_like(n_s)
            m_s[...] = jnp.zeros_like(m_s)
        else:
            c_s[...] = c0[...]
            m_s[...] = m0[...]
            for u in range(N_UNITS):
                n_s[u] = jnp.broadcast_to(n0[u:u + 1, :], (N_ROWS, M_DQK))

    yield _DONE
    lower, upper = _tri_masks()
    lane1 = lax.broadcasted_iota(jnp.int32, (1, LANES), 1)
    ones_rows = jnp.ones((N_ROWS, CHUNK), BF16)
    per_dir = ((qf, kf, vf, of, colf, bcolf, rowf, lower, s),
               (qb, kb, vb, ob, colb, bcolb, rowb, upper, nc - 1 - s))
    m_prev = m_s[...]
    m_next = m_prev
    c_new, n_new = [], []
    for d, (q_ref, k_ref, v_ref, o_ref, col_ref, bcol_ref, row_ref, mask, chunk) in enumerate(per_dir):
        last = CHUNK - 1 if d == 0 else 0
        stab = jnp.maximum(m_prev, col_ref[...])
        w_c = jnp.exp(m_prev - stab)
        stab_last = stab[last:last + 1, :]
        m_new = bcol_ref[last:last + 1, :] + stab_last
        decay = w_c[last:last + 1, :]
        c_old = [c_s[d, h] for h in range(M_HEADS)]
        n_old = [n_s[d * M_HEADS + h] for h in range(M_HEADS)]
        for h in range(M_HEADS):
            u = d * M_HEADS + h
            stab_b = _lane_bcast(stab, u)
            u_row = row_ref[u:u + 1, :]
            p = jnp.exp(jnp.where(mask, u_row - stab_b, -jnp.inf))
            qh = q_ref[:, h * M_DQK:(h + 1) * M_DQK] * (M_DQK ** -0.5)
            k_t = k_ref[:, h * M_DQK:(h + 1) * M_DQK].T
            v16 = v_ref[:, h * M_DV:(h + 1) * M_DV].astype(BF16)
            sc = _dot(qh.astype(BF16), k_t.astype(BF16)) * p
            lhs = jnp.concatenate([sc.astype(BF16), (qh * _lane_bcast(w_c, u)).astype(BF16)], axis=1)
            num = _dot(lhs, jnp.concatenate([v16, c_old[h].astype(BF16)], axis=0))
            den = _dot(jnp.concatenate([ones_rows, n_old[h].astype(BF16)], axis=1), lhs, _NT)[0:1, :]
            m_t = row_ref[G_B + u:G_B + u + 1, :] + jnp.maximum(m_prev[:, u:u + 1],
                                                                row_ref[G_CU + u:G_CU + u + 1, :])
            inv_row = 1.0 / jnp.maximum(jnp.abs(den), jnp.exp(-m_t))
            inv = jnp.broadcast_to(inv_row, (CHUNK, LANES)).T
            hbuf[d, :, h * M_DV:(h + 1) * M_DV] = num * jnp.concatenate([inv, inv], axis=1)

            kw_t = (k_t * jnp.exp(u_row - stab_last[:, u:u + 1])).astype(BF16)
            dec = decay[:, u:u + 1]
            c_new.append(dec * c_old[h] + _dot(kw_t, v16))
            n_new.append(dec * n_old[h] + _dot(ones_rows, kw_t, _NT))
            yield _WORK
        mine = (lane1 >= d * M_HEADS) & (lane1 < (d + 1) * M_HEADS)
        m_next = jnp.where(mine, m_new, m_next)
    for u in range(N_UNITS):
        c_s[u // M_HEADS, u % M_HEADS] = c_new[u]
        n_s[u] = n_new[u]
    m_s[...] = m_next

    yield _DONE
    for d, (_, _, _, o_ref, _, _, _, _, chunk) in enumerate(per_dir):
        rows = pl.ds(pl.multiple_of(chunk * CHUNK, CHUNK), CHUNK)

        @pl.when(s < half)
        def _():
            hpart[rows, :] = hbuf[d]

        @pl.when(s >= half)
        def _():
            for h in range(M_HEADS):
                cols = slice(h * M_DV, (h + 1) * M_DV)
                tot = hbuf[d, :, cols] + hpart[rows, cols]
                hm_out[rows, cols] = (_rms(tot) * gn[:, cols] * _sigmoid(o_ref[:, cols])).astype(BF16)

    @pl.when(s == nc - 1)
    def _():
        c_out[...] = c_s[...]
        m_out[...] = m_s[...]
        for u in range(N_UNITS):
            n_out[u:u + 1, :] = n_s[u][0:1, :]

    yield _DONE


def _sequence_out_spec(seq, width):
    mode = dict(pipeline_mode=pl.Buffered(1)) if seq * width * 2 >= 2 * 1024 * 1024 else {}
    return pl.BlockSpec((None, seq, width), lambda b, s: (b, 0, 0), **mode)


def _mlstm_specs(p, scans, g_norm, state, nb, nc):
    zero_init = state is None
    seq = nc * CHUNK
    col, bcol, row = scans

    def fwd(cb):
        return lambda b, s: (b * nc + s, cb)

    def bwd(cb):
        return lambda b, s: (b * nc + nc - 1 - s, cb)

    def stream(mk):
        return [pl.BlockSpec((CHUNK, M_HEADS * M_DQK), mk(P_Q // (M_HEADS * M_DQK))),
                pl.BlockSpec((CHUNK, M_HEADS * M_DQK), mk(P_K // (M_HEADS * M_DQK))),
                pl.BlockSpec((CHUNK, MIX_M), mk(P_V // MIX_M)),
                pl.BlockSpec((CHUNK, MIX_M), mk(P_O // MIX_M)),
                pl.BlockSpec((CHUNK, GATE_W), mk(0)),
                pl.BlockSpec((CHUNK, GATE_W), mk(0)),
                pl.BlockSpec((CHUNK, GATE_W), mk(0))]

    in_specs = stream(fwd) + stream(bwd) + [pl.BlockSpec((1, MIX_M), lambda b, s: (0, 0))]
    args = [p] * 4 + [col, bcol, row] + [p] * 4 + [col, bcol, row, g_norm]
    state_specs = [pl.BlockSpec((None, N_DIR, M_HEADS, M_DQK, M_DV), lambda b, s: (b, 0, 0, 0, 0)),
                   pl.BlockSpec((None, N_UNITS, M_DQK), lambda b, s: (b, 0, 0)),
                   pl.BlockSpec((None, 1, LANES), lambda b, s: (b, 0, 0))]
    if not zero_init:
        in_specs += state_specs
        args += list(state)
    out_specs = [_sequence_out_spec(seq, MIX_M)] + state_specs
    out_shape = [jax.ShapeDtypeStruct((nb, seq, MIX_M), BF16),
                 jax.ShapeDtypeStruct((nb, N_DIR, M_HEADS, M_DQK, M_DV), F32),
                 jax.ShapeDtypeStruct((nb, N_UNITS, M_DQK), F32),
                 jax.ShapeDtypeStruct((nb, 1, LANES), F32)]
    scratch = [pltpu.VMEM((N_DIR, M_HEADS, M_DQK, M_DV), F32),
               pltpu.VMEM((N_UNITS, N_ROWS, M_DQK), F32),
               pltpu.VMEM((1, LANES), F32),
               pltpu.VMEM((seq, MIX_M), F32),
               pltpu.VMEM((N_DIR, CHUNK, MIX_M), F32)]
    return in_specs, args, out_specs, out_shape, scratch


def _ssd_phases(refs, nc, zero_init):
    (xf, bf, cf, zf, colf, rowf, xb, bb, cb, zb, colb, rowb, dskip, gn) = refs[:14]
    pos = 14
    if not zero_init:
        s0 = refs[pos]
        pos += 1
    ys_out, s_out, st_s, ypart, ybuf = refs[pos:]
    s = pl.program_id(1)
    half = nc // 2

    @pl.when(s == 0)
    def _():
        if zero_init:
            st_s[...] = jnp.zeros_like(st_s)
        else:
            for d in range(N_DIR):
                for pr in range(S_PAIRS):
                    st_s[d, pr] = s0[d, pr].T

    yield _DONE
    lower, upper = _tri_masks()
    low_half = lax.broadcasted_iota(jnp.int32, (CHUNK, LANES), 1) < S_HEADDIM
    low_half1 = lax.broadcasted_iota(jnp.int32, (1, LANES), 1) < S_HEADDIM
    per_dir = ((xf, bf, cf, zf, colf, rowf, lower, s), (xb, bb, cb, zb, colb, rowb, upper, nc - 1 - s))
    st_new = []
    for d, (x_ref, b_ref, c_ref, z_ref, col_ref, row_ref, mask, chunk) in enumerate(per_dir):
        last = CHUNK - 1 if d == 0 else 0
        cs_c = col_ref[...]
        total = cs_c[last:last + 1, :]
        st_old = [st_s[d, pr] for pr in range(S_PAIRS)]
        for g in range(S_GROUPS):
            gs = slice(g * S_STATE, (g + 1) * S_STATE)
            cg = c_ref[:, gs].astype(BF16)
            b_f32 = b_ref[:, gs]
            cbm = _dot(cg, b_f32.astype(BF16), _NT)
            b_t = b_f32.T
            for pg in range(PAIRS_PER_GROUP):
                pr = g * PAIRS_PER_GROUP + pg
                cols = slice(pr * LANES, (pr + 1) * LANES)
                x16 = x_ref[:, cols].astype(BF16)
                zero16 = jnp.zeros_like(x16)
                halves = (jnp.where(low_half, x16, zero16), jnp.where(low_half, zero16, x16))
                state = st_old[pr]
                y_acc = jnp.zeros((CHUNK, LANES), F32)
                s_acc = jnp.zeros((S_STATE, LANES), F32)
                cs_b = []
                for e in range(PAIR):
                    gi = G_DT + d * S_HEADS + pr * PAIR + e
                    cs_b.append(_lane_bcast(cs_c, gi))
                    q_row = row_ref[gi:gi + 1, :]
                    wk_row = row_ref[gi + G_WK - G_DT:gi + G_WK - G_DT + 1, :]
                    mix = cbm * jnp.exp(jnp.where(mask, cs_b[e] - q_row, -jnp.inf))
                    y_acc = y_acc + _dot(mix.astype(BF16), halves[e])
                    s_acc = s_acc + _dot((b_t * wk_row).astype(BF16), halves[e])
                carry = jnp.exp(jnp.where(low_half, cs_b[0], cs_b[1]))
                ybuf[d, :, cols] = y_acc + _dot(cg, state.astype(BF16)) * carry
                gi0 = G_DT + d * S_HEADS + pr * PAIR
                dec = jnp.exp(jnp.where(low_half1, jnp.broadcast_to(total[:, gi0:gi0 + 1], (1, LANES)),
                                        jnp.broadcast_to(total[:, gi0 + 1:gi0 + 2], (1, LANES))))
                st_new.append(state * dec + s_acc)
            yield _WORK
    for d in range(N_DIR):
        for pr in range(S_PAIRS):
            st_s[d, pr] = st_new[d * S_PAIRS + pr]

    yield _DONE
    for d, (x_ref, _, _, z_ref, _, _, _, chunk) in enumerate(per_dir):
        rows = pl.ds(pl.multiple_of(chunk * CHUNK, CHUNK), CHUNK)

        @pl.when(s < half)
        def _():
            ypart[rows, :] = ybuf[d]

        @pl.when(s >= half)
        def _():
            y = ybuf[d] + ypart[rows, :] + dskip[...] * x_ref[...]
            ys_out[rows, :] = (_rms(y * _silu(z_ref[...])) * gn[...]).astype(BF16)

    @pl.when(s == nc - 1)
    def _():
        for d in range(N_DIR):
            for pr in range(S_PAIRS):
                s_out[d, pr] = st_s[d, pr].T

    yield _DONE


def _ssd_specs(p, xbc, scans, dskip, g_norm, state, nb, nc):
    zero_init = state is None
    seq = nc * CHUNK
    bc_w = S_GROUPS * S_STATE
    col, _, row = scans

    def fwd(cb):
        return lambda b, s: (b * nc + s, cb)

    def bwd(cb):
        return lambda b, s: (b * nc + nc - 1 - s, cb)

    def stream(mk):
        return [pl.BlockSpec((CHUNK, MIX_S), mk(0)),
                pl.BlockSpec((CHUNK, bc_w), mk(MIX_S // bc_w)),
                pl.BlockSpec((CHUNK, bc_w), mk(MIX_S // bc_w + 1)),
                pl.BlockSpec((CHUNK, MIX_S), mk(P_Z // MIX_S)),
                pl.BlockSpec((CHUNK, GATE_W), mk(0)),
                pl.BlockSpec((CHUNK, GATE_W), mk(0))]

    const = lambda b, s: (0, 0)
    in_specs = stream(fwd) + stream(bwd) + [pl.BlockSpec((1, MIX_S), const), pl.BlockSpec((1, MIX_S), const)]
    args = [xbc, xbc, xbc, p, col, row] * 2 + [dskip, g_norm]
    state_spec = pl.BlockSpec((None, N_DIR, S_PAIRS, LANES, S_STATE), lambda b, s: (b, 0, 0, 0, 0))
    if not zero_init:
        in_specs.append(state_spec)
        args.append(state)
    out_specs = [_sequence_out_spec(seq, MIX_S), state_spec]
    out_shape = [jax.ShapeDtypeStruct((nb, seq, MIX_S), BF16),
                 jax.ShapeDtypeStruct((nb, N_DIR, S_PAIRS, LANES, S_STATE), F32)]
    scratch = [pltpu.VMEM((N_DIR, S_PAIRS, S_STATE, LANES), F32),
               pltpu.VMEM((seq, MIX_S), F32),
               pltpu.VMEM((N_DIR, CHUNK, MIX_S), F32)]
    return in_specs, args, out_specs, out_shape, scratch


def _mixer_kernel(*refs, nc, zero_init, counts):
    (m_in, s_in), (m_out, s_out), (m_scr, s_scr) = counts
    pos = 0
    parts = []
    for n in (m_in, s_in, m_out, s_out, m_scr, s_scr):
        parts.append(refs[pos:pos + n])
        pos += n
    gens = (_mlstm_phases(parts[0] + parts[2] + parts[4], nc, zero_init),
            _ssd_phases(parts[1] + parts[3] + parts[5], nc, zero_init))
    for _ in range(3):
        live = list(gens)
        while live:
            live = [g for g in live if next(g) is not _DONE]


def _mixers(p, xbc, scans, g_mlstm, dskip, g_ssd, mlstm_state, ssd_state, nb, nc):
    m = _mlstm_specs(p, scans, g_mlstm, mlstm_state, nb, nc)
    s = _ssd_specs(p, xbc, scans, dskip, g_ssd, ssd_state, nb, nc)
    counts = tuple((len(a), len(b)) for a, b in ((m[0], s[0]), (m[2], s[2]), (m[4], s[4])))
    return pl.pallas_call(
        functools.partial(_mixer_kernel, nc=nc, zero_init=mlstm_state is None, counts=counts),
        grid=(nb, nc),
        in_specs=m[0] + s[0],
        out_specs=m[2] + s[2],
        out_shape=m[3] + s[3],
        scratch_shapes=m[4] + s[4],
        compiler_params=_params(("parallel", "arbitrary"), 60 * 1024 * 1024),
        name="mixers",
    )(*(m[1] + s[1]))


OUT_TM = 512


def _outproj_kernel(hm_ref, ys_ref, w_ref, x_ref, g1_ref, gpost_ref, gpre_ref, sc_ref, sh_ref, x1_ref, u2_ref):
    mix = _dot(hm_ref[...], w_ref[0:MIX_M, :]) + _dot(ys_ref[...], w_ref[MIX_M:, :])
    x1 = x_ref[...] + g1_ref[...] * (_rms(mix) * gpost_ref[...])
    x1_ref[...] = x1
    u2_ref[...] = (_rms(x1) * gpre_ref[...] * (1.0 + sc_ref[...]) + sh_ref[...]).astype(BF16)


def _out_proj(hm, ys, w_out16, x2d, g1, gpost, gpre, sc2, sh2, rows_per_mod):
    t = x2d.shape[0]
    tiles_per_mod = rows_per_mod // OUT_TM
    row = lambda i: (i, 0)
    const = lambda i: (0, 0)
    mod = pl.BlockSpec((None, 1, D_MODEL), lambda i: (i // tiles_per_mod, 0, 0))
    return pl.pallas_call(
        _outproj_kernel,
        grid=(t // OUT_TM,),
        in_specs=[pl.BlockSpec((OUT_TM, MIX_M), row), pl.BlockSpec((OUT_TM, MIX_S), row),
                  pl.BlockSpec((D_MODEL, D_MODEL), const), pl.BlockSpec((OUT_TM, D_MODEL), row),
                  mod, pl.BlockSpec((1, D_MODEL), const), pl.BlockSpec((1, D_MODEL), const), mod, mod],
        out_specs=[pl.BlockSpec((OUT_TM, D_MODEL), row), pl.BlockSpec((OUT_TM, D_MODEL), row)],
        out_shape=[jax.ShapeDtypeStruct((t, D_MODEL), F32), jax.ShapeDtypeStruct((t, D_MODEL), BF16)],
        compiler_params=_params(("parallel",)),
        name="out_proj",
    )(hm, ys, w_out16, x2d, g1, gpost, gpre, sc2, sh2)


MLP_TM = 1024
MLP_TH = 512
MLP_SUB = 256


def _mlp_kernel(u_ref, w1_ref, w2_ref, x1_hbm, g2_ref, gpost_ref, o_ref, x1_buf, x1_sem):
    i = pl.program_id(0)
    j = pl.program_id(1)

    def x1_copy():
        rows = pl.ds(pl.multiple_of(i * MLP_TM, MLP_TM), MLP_TM)
        return pltpu.make_async_copy(x1_hbm.at[rows, :], x1_buf, x1_sem)

    @pl.when(j == 0)
    def _():
        x1_copy().start()
        o_ref[...] = jnp.zeros_like(o_ref)

    w1 = w1_ref[...].astype(BF16)
    w2 = w2_ref[...].astype(BF16)
    for r in range(MLP_TM // MLP_SUB):
        rows = slice(r * MLP_SUB, (r + 1) * MLP_SUB)
        hid = jnp.square(jnp.maximum(_dot(u_ref[rows, :], w1), 0.0)).astype(BF16)
        o_ref[rows, :] += _dot(hid, w2)

    @pl.when(j == pl.num_programs(1) - 1)
    def _():
        x1_copy().wait()

        def fin(r, carry):
            rows = pl.ds(pl.multiple_of(r * MLP_SUB, MLP_SUB), MLP_SUB)
            o_ref[rows, :] = x1_buf[rows, :] + g2_ref[...] * (_rms(o_ref[rows, :]) * gpost_ref[...])
            return carry
        lax.fori_loop(0, MLP_TM // MLP_SUB, fin, 0)


def _mlp(u2, w1, w2, x1, g2, gpost, rows_per_mod):
    t = u2.shape[0]
    tiles_per_mod = rows_per_mod // MLP_TM
    return pl.pallas_call(
        _mlp_kernel,
        grid=(t // MLP_TM, D_FF // MLP_TH),
        in_specs=[pl.BlockSpec((MLP_TM, D_MODEL), lambda i, j: (i, 0)),
                  pl.BlockSpec((D_MODEL, MLP_TH), lambda i, j: (0, j)),
                  pl.BlockSpec((MLP_TH, D_MODEL), lambda i, j: (j, 0)),
                  pl.BlockSpec(memory_space=pl.ANY),
                  pl.BlockSpec((None, 1, D_MODEL), lambda i, j: (i // tiles_per_mod, 0, 0)),
                  pl.BlockSpec((1, D_MODEL), lambda i, j: (0, 0))],
        out_specs=pl.BlockSpec((MLP_TM, D_MODEL), lambda i, j: (i, 0)),
        out_shape=jax.ShapeDtypeStruct((t, D_MODEL), F32),
        scratch_shapes=[pltpu.VMEM((MLP_TM, D_MODEL), F32), pltpu.SemaphoreType.DMA(())],
        compiler_params=_params(("parallel", "arbitrary"), 60 * 1024 * 1024),
        name="mlp",
    )(u2, w1, w2, x1, g2, gpost)


def _gate_row(i_vals, f_vals, dt_vals):
    v = jnp.concatenate([i_vals.reshape(-1), f_vals.reshape(-1), dt_vals.reshape(-1)]).astype(F32)
    return jnp.pad(v, (0, GATE_W - N_GATES)).reshape(1, GATE_W)


def _block(x, mods, state, weights, width):
    nb, seq, _ = x.shape
    nc = seq // CHUNK
    t = nb * seq
    x2d = x.reshape(t, D_MODEL)
    sh1, sc1, g1, sh2, sc2, g2 = mods
    rows_per_mod = t // sh1.shape[0]
    w = weights

    p, gc = _in_proj(x2d, sc1, sh1, w["g_pre_mix"], w["w_a"], w["w_b"], w["wg"], rows_per_mod)
    scans = _gate_scans(gc, w["gate_bias"], w["gate_alog"])
    xbc = _conv(p, w["conv_w9"], w["conv_b"], nb, seq, width, CONV_BLOCK_ELEMS // seq)

    if state is None:
        m_state = s_state = None
    else:
        c0, n0, m0, s0 = state
        m_state = (c0, n0.reshape(nb, N_UNITS, M_DQK),
                   jnp.pad(m0.reshape(nb, 1, N_UNITS), ((0, 0), (0, 0), (0, LANES - N_UNITS))))
        s_state = s0.reshape(nb, N_DIR, S_PAIRS, LANES, S_STATE)
    hm, c_new, n_new, m_new, ys, s_new = _mixers(p, xbc, scans, w["g_mlstm_norm"], w["dskip"],
                                                 w["g_ssd_norm"], m_state, s_state, nb, nc)

    x1, u2 = _out_proj(hm.reshape(t, MIX_M), ys.reshape(t, MIX_S), w["w_out"], x2d, g1,
                       w["g_post_mix"], w["g_pre_mlp"], sc2, sh2, rows_per_mod)
    y = _mlp(u2, w["w_mlp_in"], w["w_mlp_out"], x1, g2, w["g_post_mlp"], rows_per_mod)
    new_state = (c_new.reshape(nb, 1, N_DIR, M_HEADS, M_DQK, M_DV),
                 n_new.reshape(nb, 1, N_DIR, M_HEADS, M_DQK),
                 m_new[:, 0, :N_UNITS].reshape(nb, 1, N_DIR, M_HEADS),
                 s_new.reshape(nb, 1, N_DIR, S_HEADS, S_HEADDIM, S_STATE))
    return y.reshape(nb, seq, D_MODEL), new_state


def kernel(x_prompt, x_sample, state_mlstm_c, state_mlstm_n, state_mlstm_m, state_ssd, c, c_ctx, w_mod, b_mod,
           g_pre_mix, g_post_mix, w_in, b_igate, b_fgate, conv_w, conv_b, dt_bias, a_log, d_skip, g_mlstm_norm,
           g_ssd_norm, w_out, g_pre_mlp, g_post_mlp, w_mlp_in, w_mlp_out):
    assert w_mod.shape[0] == 1, "one layer"
    nb_s = x_sample.shape[0]

    cond8 = jnp.zeros((8, D_MODEL), F32).at[0].set(c_ctx).at[1:1 + nb_s].set(c)
    mod = _modulation(cond8, w_mod[0], b_mod[0].reshape(1, -1))
    mods = [mod[:, k * D_MODEL:(k + 1) * D_MODEL] for k in range(6)]
    mods_p = [m[0:1].reshape(1, 1, D_MODEL) for m in mods]
    mods_s = [m[1:1 + nb_s].reshape(nb_s, 1, D_MODEL) for m in mods]

    w_a, w_b, gate_cols = _pack_w_in(w_in[0])
    zeros_u = jnp.zeros((N_UNITS,), F32)
    row = lambda v: v.reshape(1, -1)
    weights = dict(
        w_a=w_a, w_b=w_b, wg=gate_cols,
        g_pre_mix=row(g_pre_mix[0]), g_post_mix=row(g_post_mix[0]),
        g_pre_mlp=row(g_pre_mlp[0]), g_post_mlp=row(g_post_mlp[0]),
        conv_w9=conv_w[0].reshape(9, XBC), conv_b=row(conv_b[0]),
        gate_bias=_gate_row(b_igate[0], b_fgate[0], dt_bias[0]),
        gate_alog=_gate_row(zeros_u, zeros_u, a_log[0]),
        dskip=row(jnp.repeat(d_skip[0], S_HEADDIM)),
        g_mlstm_norm=row(g_mlstm_norm[0]), g_ssd_norm=row(g_ssd_norm[0]),
        w_out=w_out[0].astype(BF16), w_mlp_in=w_mlp_in[0], w_mlp_out=w_mlp_out[0])

    y_p, st = _block(x_prompt, mods_p, None, weights, x_prompt.shape[1])
    cache = (state_mlstm_c[:, 0], state_mlstm_n[:, 0], state_mlstm_m[:, 0], state_ssd[:, 0])
    y_s, _ = _block(x_sample, mods_s, cache, weights, GRID_W)
    return (y_p, y_s) + st
```

```python
import functools

import jax
import jax.numpy as jnp
from jax import lax
from jax.experimental import pallas as pl
from jax.experimental.pallas import tpu as pltpu

F32 = jnp.float32
BF16 = jnp.bfloat16

D_MODEL = 2048
CHUNK = 128
N_DIR = 2
M_HEADS = 4
M_DQK = 128
M_DV = 256
MIX_M = M_HEADS * M_DV
S_HEADS = 16
S_HEADDIM = 64
S_STATE = 128
S_GROUPS = 4
S_REP = S_HEADS // S_GROUPS
MIX_S = S_HEADS * S_HEADDIM
XBC = MIX_S + 2 * S_GROUPS * S_STATE
D_FF = 4 * D_MODEL
GRID_W = 64
EPS = 1e-6
LANES = 128

P_Q = 0
P_K = M_HEADS * M_DQK
P_V = 2 * M_HEADS * M_DQK
P_O = P_V + MIX_M
P_Z = P_O + MIX_M
P_XBC = P_Z + MIX_S
P_MAIN = P_XBC + XBC
GATE_W = LANES
N_UNITS = N_DIR * M_HEADS
G_I = 0
G_F = N_UNITS
G_DT = 2 * N_UNITS
N_GATES = G_DT + N_DIR * S_HEADS
G_WK = G_DT + N_DIR * S_HEADS
G_CU = G_WK + N_DIR * S_HEADS
G_B = G_CU + N_UNITS
assert G_B + N_UNITS <= GATE_W

PAIR = LANES // S_HEADDIM
S_PAIRS = S_HEADS // PAIR
PAIRS_PER_GROUP = S_REP // PAIR

VMEM_LIMIT = 48 * 1024 * 1024

_NT = (((1,), (1,)), ((), ()))


def _params(sem, limit=VMEM_LIMIT):
    return pltpu.CompilerParams(dimension_semantics=sem, vmem_limit_bytes=limit)


def _silu(x):
    return x / (1.0 + jnp.exp(-x))


def _sigmoid(x):
    return 1.0 / (1.0 + jnp.exp(-x))


def _rms(x):
    return x * lax.rsqrt(jnp.mean(x * x, axis=-1, keepdims=True) + EPS)


def _dot(a, b, dims=None, precision=None):
    if dims is None:
        dims = (((a.ndim - 1,), (0,)), ((), ()))
    return lax.dot_general(a, b, dims, precision=precision, preferred_element_type=F32)


def _lane_bcast(tile, lane):
    return jnp.broadcast_to(tile[:, lane:lane + 1], tile.shape)


def _mod_kernel(c_ref, w_ref, b_ref, o_ref):
    a = _silu(c_ref[...]).astype(BF16)
    o_ref[...] = _dot(a, w_ref[...].astype(BF16)) + b_ref[...]


def _modulation(cond8, w_mod, b_mod):
    n = w_mod.shape[1]
    tn = 1024
    return pl.pallas_call(
        _mod_kernel,
        grid=(n // tn,),
        in_specs=[pl.BlockSpec((8, D_MODEL), lambda j: (0, 0)),
                  pl.BlockSpec((D_MODEL, tn), lambda j: (0, j)),
                  pl.BlockSpec((1, tn), lambda j: (0, j))],
        out_specs=pl.BlockSpec((8, tn), lambda j: (0, j)),
        out_shape=jax.ShapeDtypeStruct((8, n), F32),
        compiler_params=_params(("parallel",)),
        name="modulation",
    )(cond8, w_mod, b_mod)


IN_TM = 1024
IN_TN = 1024
IN_SUB = 128


def _inproj_kernel(x_ref, sc_ref, sh_ref, g_ref, w_ref, wg_ref, o_ref, gc_ref, u_ref):
    j = pl.program_id(1)

    @pl.when(j == 0)
    def _():
        def body(r, carry):
            rows = pl.ds(pl.multiple_of(r * IN_SUB, IN_SUB), IN_SUB)
            y = _rms(x_ref[rows, :]) * g_ref[...]
            u_ref[rows, :] = (y * (1.0 + sc_ref[...]) + sh_ref[...]).astype(BF16)
            return carry
        lax.fori_loop(0, IN_TM // IN_SUB, body, 0)

    o_ref[...] = _dot(u_ref[...], w_ref[...], _NT)

    @pl.when(j == pl.num_programs(1) - 1)
    def _():
        gc_ref[...] = _dot(u_ref[...], wg_ref[...], _NT)


def _in_proj(x2d, sc, sh, g, w_t, wg, rows_per_mod):
    t = x2d.shape[0]
    tiles_per_mod = rows_per_mod // IN_TM
    n_a = P_Z // IN_TN

    def first_row(i, j):
        return (j * (IN_TN // G_DT) + (j >= n_a).astype(jnp.int32)) * G_DT, 0
    return pl.pallas_call(
        _inproj_kernel,
        grid=(t // IN_TM, P_MAIN // IN_TN),
        in_specs=[pl.BlockSpec((IN_TM, D_MODEL), lambda i, j: (i, 0)),
                  pl.BlockSpec((None, 1, D_MODEL), lambda i, j: (i // tiles_per_mod, 0, 0)),
                  pl.BlockSpec((None, 1, D_MODEL), lambda i, j: (i // tiles_per_mod, 0, 0)),
                  pl.BlockSpec((1, D_MODEL), lambda i, j: (0, 0)),
                  pl.BlockSpec((pl.Element(IN_TN), pl.Element(D_MODEL)), first_row),
                  pl.BlockSpec((GATE_W, D_MODEL), lambda i, j: (0, 0))],
        out_specs=[pl.BlockSpec((IN_TM, IN_TN), lambda i, j: (i, j)),
                   pl.BlockSpec((IN_TM, GATE_W), lambda i, j: (i, 0))],
        out_shape=[jax.ShapeDtypeStruct((t, P_MAIN), F32),
                   jax.ShapeDtypeStruct((t, GATE_W), F32)],
        scratch_shapes=[pltpu.VMEM((IN_TM, D_MODEL), BF16)],
        compiler_params=_params(("parallel", "arbitrary"), 60 * 1024 * 1024),
        name="in_proj",
    )(x2d, sc, sh, g, w_t, wg)


def _conv_kernel(x_ref, w_ref, b_ref, o_ref, *, seq, width):
    x = x_ref[...]
    t = lax.broadcasted_iota(jnp.int32, x.shape, 0)
    c = jnp.bitwise_and(t, width - 1)
    xl = jnp.where(c >= 1, pltpu.roll(x, 1, 0), 0.0)
    xr = jnp.where(c <= width - 2, pltpu.roll(x, seq - 1, 0), 0.0)

    def taps(di):
        return w_ref[3 * di:3 * di + 1, :] * xl + w_ref[3 * di + 1:3 * di + 2, :] * x \
            + w_ref[3 * di + 2:3 * di + 3, :] * xr

    out = taps(1) + b_ref[...]
    if seq > width:
        out = out + jnp.where(t >= width, pltpu.roll(taps(0), width, 0), 0.0)
        out = out + jnp.where(t < seq - width, pltpu.roll(taps(2), seq - width, 0), 0.0)
    o_ref[...] = _silu(out)


CONV_BLOCK_ELEMS = 512 * 1024


def _conv(p, conv_w9, conv_b, nb, seq, width, cn):
    cn = min(cn, XBC)
    col0 = P_XBC // cn
    return pl.pallas_call(
        functools.partial(_conv_kernel, seq=seq, width=width),
        grid=(nb, XBC // cn),
        in_specs=[pl.BlockSpec((seq, cn), lambda b, j: (b, col0 + j)),
                  pl.BlockSpec((9, cn), lambda b, j: (0, j)),
                  pl.BlockSpec((1, cn), lambda b, j: (0, j))],
        out_specs=pl.BlockSpec((seq, cn), lambda b, j: (b, j)),
        out_shape=jax.ShapeDtypeStruct((nb * seq, XBC), F32),
        compiler_params=_params(("parallel", "parallel")),
        name="grid_conv",
    )(p, conv_w9, conv_b)


def _tri_masks():
    r = lax.broadcasted_iota(jnp.int32, (CHUNK, CHUNK), 0)
    c = lax.broadcasted_iota(jnp.int32, (CHUNK, CHUNK), 1)
    return r >= c, r <= c


SCAN_CHUNKS = 8


def _scan_kernel(gc_ref, bias_ref, alog_ref, col_ref, bcol_ref, row_ref):
    lower, upper = _tri_masks()
    lo, up = lower.astype(F32), upper.astype(F32)
    hi = lax.Precision.HIGHEST
    lane = lax.broadcasted_iota(jnp.int32, (CHUNK, GATE_W), 1)
    time = lax.broadcasted_iota(jnp.int32, (CHUNK, GATE_W), 0)
    lane1 = lax.broadcasted_iota(jnp.int32, (1, GATE_W), 1)

    def backward(l):
        unit_bwd = (l < G_DT) & (jnp.bitwise_and(l, N_UNITS - 1) >= M_HEADS)
        return unit_bwd | ((l >= G_DT + S_HEADS) & (l < N_GATES))

    is_bwd, is_bwd1 = backward(lane), backward(lane1)
    is_i = lane < G_F
    is_f = (lane >= G_F) & (lane < G_DT)
    is_dt = (lane >= G_DT) & (lane < N_GATES)
    neg_a = -jnp.exp(alog_ref[...])
    for c in range(SCAN_CHUNKS):
        rows = slice(c * CHUNK, (c + 1) * CHUNK)
        g = gc_ref[rows, :] + bias_ref[...]
        soft = jnp.log1p(jnp.exp(-jnp.abs(g)))
        logf = jnp.minimum(g, 0.0) - soft
        dt = jnp.maximum(g, 0.0) + soft
        x = jnp.where(is_f, logf, jnp.where(is_dt, dt * neg_a, 0.0))
        cs = jnp.where(is_bwd, _dot(up, x, precision=hi), _dot(lo, x, precision=hi))
        b_units = pltpu.roll(cs, GATE_W - G_F, 1)
        u = g - b_units
        cu = u
        k = 1
        while k < CHUNK:
            prev = jnp.where(time >= k, pltpu.roll(cu, k, 0), -jnp.inf)
            nxt = jnp.where(time < CHUNK - k, pltpu.roll(cu, CHUNK - k, 0), -jnp.inf)
            cu = jnp.maximum(cu, jnp.where(is_bwd, nxt, prev))
            k *= 2
        total = jnp.where(is_bwd1, cs[0:1, :], cs[CHUNK - 1:CHUNK, :])
        wk = jnp.exp(total - cs) * dt
        q = cs - jnp.log(dt)
        col_ref[rows, :] = jnp.where(is_i, cu, cs)
        bcol_ref[rows, :] = b_units
        tail = jnp.where(lane < G_CU, pltpu.roll(wk, G_WK - G_DT, 1),
                         jnp.where(lane < G_B, pltpu.roll(cu, G_CU, 1), pltpu.roll(b_units, G_B, 1)))
        row_ref[rows, :] = jnp.where(is_i, u, jnp.where(is_dt, q, tail)).T


def _gate_scans(gc, bias, alog):
    t = gc.shape[0]
    tm = SCAN_CHUNKS * CHUNK
    row = lambda i: (i, 0)
    const = lambda i: (0, 0)
    return pl.pallas_call(
        _scan_kernel,
        grid=(t // tm,),
        in_specs=[pl.BlockSpec((tm, GATE_W), row), pl.BlockSpec((1, GATE_W), const),
                  pl.BlockSpec((1, GATE_W), const)],
        out_specs=[pl.BlockSpec((tm, GATE_W), row)] * 3,
        out_shape=[jax.ShapeDtypeStruct((t, GATE_W), F32)] * 3,
        compiler_params=_params(("parallel",)),
        name="gate_scans",
    )(gc, bias, alog)


N_ROWS = 16
_WORK, _DONE = "work", "done"


def _mlstm_phases(refs, nc, zero_init):
    (qf, kf, vf, of, colf, bcolf, rowf, qb, kb, vb, ob, colb, bcolb, rowb, gn) = refs[:15]
    pos = 15
    if not zero_init:
        c0, n0, m0 = refs[pos:pos + 3]
        pos += 3
    hm_out, c_out, n_out, m_out, c_s, n_s, m_s, hpart, hbuf = refs[pos:]
    s = pl.program_id(1)
    half = nc // 2

    @pl.when(s == 0)
    def _():
        if zero_init:
            c_s[...] = jnp.zeros_like(c_s)
            n_s[...] = jnp.zeros_like(n_s)
            m_s[...] = jnp.zeros_like(m_s)
        else:
            c_s[...] = c0[...]
            m_s[...] = m0[...]
            for u in range(N_UNITS):
                n_s[u] = jnp.broadcast_to(n0[u:u + 1, :], (N_ROWS, M_DQK))

    yield _DONE
    lower, upper = _tri_masks()
    lane1 = lax.broadcasted_iota(jnp.int32, (1, LANES), 1)
    ones_rows = jnp.ones((N_ROWS, CHUNK), BF16)
    per_dir = ((qf, kf, vf, of, colf, bcolf, rowf, lower, s),
               (qb, kb, vb, ob, colb, bcolb, rowb, upper, nc - 1 - s))
    m_prev = m_s[...]
    m_next = m_prev
    c_new, n_new = [], []
    for d, (q_ref, k_ref, v_ref, o_ref, col_ref, bcol_ref, row_ref, mask, chunk) in enumerate(per_dir):
        last = CHUNK - 1 if d == 0 else 0
        stab = jnp.maximum(m_prev, col_ref[...])
        w_c = jnp.exp(m_prev - stab)
        stab_last = stab[last:last + 1, :]
        m_new = bcol_ref[last:last + 1, :] + stab_last
        decay = w_c[last:last + 1, :]
        c_old = [c_s[d, h] for h in range(M_HEADS)]
        n_old = [n_s[d * M_HEADS + h] for h in range(M_HEADS)]
        for h in range(M_HEADS):
            u = d * M_HEADS + h
            stab_b = _lane_bcast(stab, u)
            u_row = row_ref[u:u + 1, :]
            p = jnp.exp(jnp.where(mask, u_row - stab_b, -jnp.inf))
            qh = q_ref[:, h * M_DQK:(h + 1) * M_DQK] * (M_DQK ** -0.5)
            k_t = k_ref[:, h * M_DQK:(h + 1) * M_DQK].T
            v16 = v_ref[:, h * M_DV:(h + 1) * M_DV].astype(BF16)
            sc = _dot(qh.astype(BF16), k_t.astype(BF16)) * p
            lhs = jnp.concatenate([sc.astype(BF16), (qh * _lane_bcast(w_c, u)).astype(BF16)], axis=1)
            num = _dot(lhs, jnp.concatenate([v16, c_old[h].astype(BF16)], axis=0))
            den = _dot(jnp.concatenate([ones_rows, n_old[h].astype(BF16)], axis=1), lhs, _NT)[0:1, :]
            m_t = row_ref[G_B + u:G_B + u + 1, :] + jnp.maximum(m_prev[:, u:u + 1],
                                                                row_ref[G_CU + u:G_CU + u + 1, :])
            inv_row = 1.0 / jnp.maximum(jnp.abs(den), jnp.exp(-m_t))
            inv = jnp.broadcast_to(inv_row, (CHUNK, LANES)).T
            hbuf[d, :, h * M_DV:(h + 1) * M_DV] = num * jnp.concatenate([inv, inv], axis=1)

            kw_t = (k_t * jnp.exp(u_row - stab_last[:, u:u + 1])).astype(BF16)
            dec = decay[:, u:u + 1]
            c_new.append(dec * c_old[h] + _dot(kw_t, v16))
            n_new.append(dec * n_old[h] + _dot(ones_rows, kw_t, _NT))
            yield _WORK
        mine = (lane1 >= d * M_HEADS) & (lane1 < (d + 1) * M_HEADS)
        m_next = jnp.where(mine, m_new, m_next)
    for u in range(N_UNITS):
        c_s[u // M_HEADS, u % M_HEADS] = c_new[u]
        n_s[u] = n_new[u]
    m_s[...] = m_next

    yield _DONE
    for d, (_, _, _, o_ref, _, _, _, _, chunk) in enumerate(per_dir):
        rows = pl.ds(pl.multiple_of(chunk * CHUNK, CHUNK), CHUNK)

        @pl.when(s < half)
        def _():
            hpart[rows, :] = hbuf[d]

        @pl.when(s >= half)
        def _():
            for h in range(M_HEADS):
                cols = slice(h * M_DV, (h + 1) * M_DV)
                tot = hbuf[d, :, cols] + hpart[rows, cols]
                hm_out[rows, cols] = (_rms(tot) * gn[:, cols] * _sigmoid(o_ref[:, cols])).astype(BF16)

    @pl.when(s == nc - 1)
    def _():
        c_out[...] = c_s[...]
        m_out[...] = m_s[...]
        for u in range(N_UNITS):
            n_out[u:u + 1, :] = n_s[u][0:1, :]

    yield _DONE


def _sequence_out_spec(seq, width):
    mode = dict(pipeline_mode=pl.Buffered(1)) if seq * width * 2 >= 2 * 1024 * 1024 else {}
    return pl.BlockSpec((None, seq, width), lambda b, s: (b, 0, 0), **mode)


def _mlstm_specs(p, scans, g_norm, state, nb, nc):
    zero_init = state is None
    seq = nc * CHUNK
    col, bcol, row = scans

    def fwd(cb):
        return lambda b, s: (b * nc + s, cb)

    def bwd(cb):
        return lambda b, s: (b * nc + nc - 1 - s, cb)

    def stream(mk):
        return [pl.BlockSpec((CHUNK, M_HEADS * M_DQK), mk(P_Q // (M_HEADS * M_DQK))),
                pl.BlockSpec((CHUNK, M_HEADS * M_DQK), mk(P_K // (M_HEADS * M_DQK))),
                pl.BlockSpec((CHUNK, MIX_M), mk(P_V // MIX_M)),
                pl.BlockSpec((CHUNK, MIX_M), mk(P_O // MIX_M)),
                pl.BlockSpec((CHUNK, GATE_W), mk(0)),
                pl.BlockSpec((CHUNK, GATE_W), mk(0)),
                pl.BlockSpec((CHUNK, GATE_W), mk(0))]

    in_specs = stream(fwd) + stream(bwd) + [pl.BlockSpec((1, MIX_M), lambda b, s: (0, 0))]
    args = [p] * 4 + [col, bcol, row] + [p] * 4 + [col, bcol, row, g_norm]
    state_specs = [pl.BlockSpec((None, N_DIR, M_HEADS, M_DQK, M_DV), lambda b, s: (b, 0, 0, 0, 0)),
                   pl.BlockSpec((None, N_UNITS, M_DQK), lambda b, s: (b, 0, 0)),
                   pl.BlockSpec((None, 1, LANES), lambda b, s: (b, 0, 0))]
    if not zero_init:
        in_specs += state_specs
        args += list(state)
    out_specs = [_sequence_out_spec(seq, MIX_M)] + state_specs
    out_shape = [jax.ShapeDtypeStruct((nb, seq, MIX_M), BF16),
                 jax.ShapeDtypeStruct((nb, N_DIR, M_HEADS, M_DQK, M_DV), F32),
                 jax.ShapeDtypeStruct((nb, N_UNITS, M_DQK), F32),
                 jax.ShapeDtypeStruct((nb, 1, LANES), F32)]
    scratch = [pltpu.VMEM((N_DIR, M_HEADS, M_DQK, M_DV), F32),
               pltpu.VMEM((N_UNITS, N_ROWS, M_DQK), F32),
               pltpu.VMEM((1, LANES), F32),
               pltpu.VMEM((seq, MIX_M), F32),
               pltpu.VMEM((N_DIR, CHUNK, MIX_M), F32)]
    return in_specs, args, out_specs, out_shape, scratch


def _ssd_phases(refs, nc, zero_init):
    (xf, bf, cf, zf, colf, rowf, xb, bb, cb, zb, colb, rowb, dskip, gn) = refs[:14]
    pos = 14
    if not zero_init:
        s0 = refs[pos]
        pos += 1
    ys_out, s_out, st_s, ypart, ybuf = refs[pos:]
    s = pl.program_id(1)
    half = nc // 2

    @pl.when(s == 0)
    def _():
        if zero_init:
            st_s[...] = jnp.zeros_like(st_s)
        else:
            for d in range(N_DIR):
                for pr in range(S_PAIRS):
                    st_s[d, pr] = s0[d, pr].T

    yield _DONE
    lower, upper = _tri_masks()
    low_half = lax.broadcasted_iota(jnp.int32, (CHUNK, LANES), 1) < S_HEADDIM
    low_half1 = lax.broadcasted_iota(jnp.int32, (1, LANES), 1) < S_HEADDIM
    per_dir = ((xf, bf, cf, zf, colf, rowf, lower, s), (xb, bb, cb, zb, colb, rowb, upper, nc - 1 - s))
    st_new = []
    for d, (x_ref, b_ref, c_ref, z_ref, col_ref, row_ref, mask, chunk) in enumerate(per_dir):
        last = CHUNK - 1 if d == 0 else 0
        cs_c = col_ref[...]
        total = cs_c[last:last + 1, :]
        st_old = [st_s[d, pr] for pr in range(S_PAIRS)]
        for g in range(S_GROUPS):
            gs = slice(g * S_STATE, (g + 1) * S_STATE)
            cg = c_ref[:, gs].astype(BF16)
            b_f32 = b_ref[:, gs]
            cbm = _dot(cg, b_f32.astype(BF16), _NT)
            b_t = b_f32.T
            for pg in range(PAIRS_PER_GROUP):
                pr = g * PAIRS_PER_GROUP + pg
                cols = slice(pr * LANES, (pr + 1) * LANES)
                x16 = x_ref[:, cols].astype(BF16)
                zero16 = jnp.zeros_like(x16)
                halves = (jnp.where(low_half, x16, zero16), jnp.where(low_half, zero16, x16))
                state = st_old[pr]
                y_acc = jnp.zeros((CHUNK, LANES), F32)
                s_acc = jnp.zeros((S_STATE, LANES), F32)
                cs_b = []
                for e in range(PAIR):
                    gi = G_DT + d * S_HEADS + pr * PAIR + e
                    cs_b.append(_lane_bcast(cs_c, gi))
                    q_row = row_ref[gi:gi + 1, :]
                    wk_row = row_ref[gi + G_WK - G_DT:gi + G_WK - G_DT + 1, :]
                    mix = cbm * jnp.exp(jnp.where(mask, cs_b[e] - q_row, -jnp.inf))
                    y_acc = y_acc + _dot(mix.astype(BF16), halves[e])
                    s_acc = s_acc + _dot((b_t * wk_row).astype(BF16), halves[e])
                carry = jnp.exp(jnp.where(low_half, cs_b[0], cs_b[1]))
                ybuf[d, :, cols] = y_acc + _dot(cg, state.astype(BF16)) * carry
                gi0 = G_DT + d * S_HEADS + pr * PAIR
                dec = jnp.exp(jnp.where(low_half1, jnp.broadcast_to(total[:, gi0:gi0 + 1], (1, LANES)),
                                        jnp.broadcast_to(total[:, gi0 + 1:gi0 + 2], (1, LANES))))
                st_new.append(state * dec + s_acc)
            yield _WORK
    for d in range(N_DIR):
        for pr in range(S_PAIRS):
            st_s[d, pr] = st_new[d * S_PAIRS + pr]

    yield _DONE
    for d, (x_ref, _, _, z_ref, _, _, _, chunk) in enumerate(per_dir):
        rows = pl.ds(pl.multiple_of(chunk * CHUNK, CHUNK), CHUNK)

        @pl.when(s < half)
        def _():
            ypart[rows, :] = ybuf[d]

        @pl.when(s >= half)
        def _():
            y = ybuf[d] + ypart[rows, :] + dskip[...] * x_ref[...]
            ys_out[rows, :] = (_rms(y * _silu(z_ref[...])) * gn[...]).astype(BF16)

    @pl.when(s == nc - 1)
    def _():
        for d in range(N_DIR):
            for pr in range(S_PAIRS):
                s_out[d, pr] = st_s[d, pr].T

    yield _DONE


def _ssd_specs(p, xbc, scans, dskip, g_norm, state, nb, nc):
    zero_init = state is None
    seq = nc * CHUNK
    bc_w = S_GROUPS * S_STATE
    col, _, row = scans

    def fwd(cb):
        return lambda b, s: (b * nc + s, cb)

    def bwd(cb):
        return lambda b, s: (b * nc + nc - 1 - s, cb)

    def stream(mk):
        return [pl.BlockSpec((CHUNK, MIX_S), mk(0)),
                pl.BlockSpec((CHUNK, bc_w), mk(MIX_S // bc_w)),
                pl.BlockSpec((CHUNK, bc_w), mk(MIX_S // bc_w + 1)),
                pl.BlockSpec((CHUNK, MIX_S), mk(P_Z // MIX_S)),
                pl.BlockSpec((CHUNK, GATE_W), mk(0)),
                pl.BlockSpec((CHUNK, GATE_W), mk(0))]

    const = lambda b, s: (0, 0)
    in_specs = stream(fwd) + stream(bwd) + [pl.BlockSpec((1, MIX_S), const), pl.BlockSpec((1, MIX_S), const)]
    args = [xbc, xbc, xbc, p, col, row] * 2 + [dskip, g_norm]
    state_spec = pl.BlockSpec((None, N_DIR, S_PAIRS, LANES, S_STATE), lambda b, s: (b, 0, 0, 0, 0))
    if not zero_init:
        in_specs.append(state_spec)
        args.append(state)
    out_specs = [_sequence_out_spec(seq, MIX_S), state_spec]
    out_shape = [jax.ShapeDtypeStruct((nb, seq, MIX_S), BF16),
                 jax.ShapeDtypeStruct((nb, N_DIR, S_PAIRS, LANES, S_STATE), F32)]
    scratch = [pltpu.VMEM((N_DIR, S_PAIRS, S_STATE, LANES), F32),
               pltpu.VMEM((seq, MIX_S), F32),
               pltpu.VMEM((N_DIR, CHUNK, MIX_S), F32)]
    return in_specs, args, out_specs, out_shape, scratch


def _mixer_kernel(*refs, nc, zero_init, counts):
    (m_in, s_in), (m_out, s_out), (m_scr, s_scr) = counts
    pos = 0
    parts = []
    for n in (m_in, s_in, m_out, s_out, m_scr, s_scr):
        parts.append(refs[pos:pos + n])
        pos += n
    gens = (_mlstm_phases(parts[0] + parts[2] + parts[4], nc, zero_init),
            _ssd_phases(parts[1] + parts[3] + parts[5], nc, zero_init))
    for _ in range(3):
        live = list(gens)
        while live:
            live = [g for g in live if next(g) is not _DONE]


def _mixers(p, xbc, scans, g_mlstm, dskip, g_ssd, mlstm_state, ssd_state, nb, nc):
    m = _mlstm_specs(p, scans, g_mlstm, mlstm_state, nb, nc)
    s = _ssd_specs(p, xbc, scans, dskip, g_ssd, ssd_state, nb, nc)
    counts = tuple((len(a), len(b)) for a, b in ((m[0], s[0]), (m[2], s[2]), (m[4], s[4])))
    return pl.pallas_call(
        functools.partial(_mixer_kernel, nc=nc, zero_init=mlstm_state is None, counts=counts),
        grid=(nb, nc),
        in_specs=m[0] + s[0],
        out_specs=m[2] + s[2],
        out_shape=m[3] + s[3],
        scratch_shapes=m[4] + s[4],
        compiler_params=_params(("parallel", "arbitrary"), 60 * 1024 * 1024),
        name="mixers",
    )(*(m[1] + s[1]))


OUT_TM = 512


def _outproj_kernel(hm_ref, ys_ref, w_ref, x_ref, g1_ref, gpost_ref, gpre_ref, sc_ref, sh_ref, x1_ref, u2_ref):
    mix = _dot(hm_ref[...], w_ref[0:MIX_M, :]) + _dot(ys_ref[...], w_ref[MIX_M:, :])
    x1 = x_ref[...] + g1_ref[...] * (_rms(mix) * gpost_ref[...])
    x1_ref[...] = x1
    u2_ref[...] = (_rms(x1) * gpre_ref[...] * (1.0 + sc_ref[...]) + sh_ref[...]).astype(BF16)


def _out_proj(hm, ys, w_out16, x2d, g1, gpost, gpre, sc2, sh2, rows_per_mod):
    t = x2d.shape[0]
    tiles_per_mod = rows_per_mod // OUT_TM
    row = lambda i: (i, 0)
    const = lambda i: (0, 0)
    mod = pl.BlockSpec((None, 1, D_MODEL), lambda i: (i // tiles_per_mod, 0, 0))
    return pl.pallas_call(
        _outproj_kernel,
        grid=(t // OUT_TM,),
        in_specs=[pl.BlockSpec((OUT_TM, MIX_M), row), pl.BlockSpec((OUT_TM, MIX_S), row),
                  pl.BlockSpec((D_MODEL, D_MODEL), const), pl.BlockSpec((OUT_TM, D_MODEL), row),
                  mod, pl.BlockSpec((1, D_MODEL), const), pl.BlockSpec((1, D_MODEL), const), mod, mod],
        out_specs=[pl.BlockSpec((OUT_TM, D_MODEL), row), pl.BlockSpec((OUT_TM, D_MODEL), row)],
        out_shape=[jax.ShapeDtypeStruct((t, D_MODEL), F32), jax.ShapeDtypeStruct((t, D_MODEL), BF16)],
        compiler_params=_params(("parallel",)),
        name="out_proj",
    )(hm, ys, w_out16, x2d, g1, gpost, gpre, sc2, sh2)


MLP_TM = 1024
MLP_TH = 512
MLP_SUB = 256


def _mlp_kernel(u_ref, w1_ref, w2_ref, x1_hbm, g2_ref, gpost_ref, o_ref, x1_buf, x1_sem):
    i = pl.program_id(0)
    j = pl.program_id(1)

    def x1_copy():
        rows = pl.ds(pl.multiple_of(i * MLP_TM, MLP_TM), MLP_TM)
        return pltpu.make_async_copy(x1_hbm.at[rows, :], x1_buf, x1_sem)

    @pl.when(j == 0)
    def _():
        x1_copy().start()
        o_ref[...] = jnp.zeros_like(o_ref)

    w1 = w1_ref[...].astype(BF16)
    w2 = w2_ref[...].astype(BF16)
    for r in range(MLP_TM // MLP_SUB):
        rows = slice(r * MLP_SUB, (r + 1) * MLP_SUB)
        hid = jnp.square(jnp.maximum(_dot(u_ref[rows, :], w1), 0.0)).astype(BF16)
        o_ref[rows, :] += _dot(hid, w2)

    @pl.when(j == pl.num_programs(1) - 1)
    def _():
        x1_copy().wait()

        def fin(r, carry):
            rows = pl.ds(pl.multiple_of(r * MLP_SUB, MLP_SUB), MLP_SUB)
            o_ref[rows, :] = x1_buf[rows, :] + g2_ref[...] * (_rms(o_ref[rows, :]) * gpost_ref[...])
            return carry
        lax.fori_loop(0, MLP_TM // MLP_SUB, fin, 0)


def _mlp(u2, w1, w2, x1, g2, gpost, rows_per_mod):
    t = u2.shape[0]
    tiles_per_mod = rows_per_mod // MLP_TM
    return pl.pallas_call(
        _mlp_kernel,
        grid=(t // MLP_TM, D_FF // MLP_TH),
        in_specs=[pl.BlockSpec((MLP_TM, D_MODEL), lambda i, j: (i, 0)),
                  pl.BlockSpec((D_MODEL, MLP_TH), lambda i, j: (0, j)),
                  pl.BlockSpec((MLP_TH, D_MODEL), lambda i, j: (j, 0)),
                  pl.BlockSpec(memory_space=pl.ANY),
                  pl.BlockSpec((None, 1, D_MODEL), lambda i, j: (i // tiles_per_mod, 0, 0)),
                  pl.BlockSpec((1, D_MODEL), lambda i, j: (0, 0))],
        out_specs=pl.BlockSpec((MLP_TM, D_MODEL), lambda i, j: (i, 0)),
        out_shape=jax.ShapeDtypeStruct((t, D_MODEL), F32),
        scratch_shapes=[pltpu.VMEM((MLP_TM, D_MODEL), F32), pltpu.SemaphoreType.DMA(())],
        compiler_params=_params(("parallel", "arbitrary"), 60 * 1024 * 1024),
        name="mlp",
    )(u2, w1, w2, x1, g2, gpost)


def _gate_row(i_vals, f_vals, dt_vals):
    v = jnp.concatenate([i_vals.reshape(-1), f_vals.reshape(-1), dt_vals.reshape(-1)]).astype(F32)
    return jnp.pad(v, (0, GATE_W - N_GATES)).reshape(1, GATE_W)


def _block(x, mods, state, weights, width):
    nb, seq, _ = x.shape
    nc = seq // CHUNK
    t = nb * seq
    x2d = x.reshape(t, D_MODEL)
    sh1, sc1, g1, sh2, sc2, g2 = mods
    rows_per_mod = t // sh1.shape[0]
    w = weights

    p, gc = _in_proj(x2d, sc1, sh1, w["g_pre_mix"], w["w_t"], w["wg"], rows_per_mod)
    scans = _gate_scans(gc, w["gate_bias"], w["gate_alog"])
    xbc = _conv(p, w["conv_w9"], w["conv_b"], nb, seq, width, CONV_BLOCK_ELEMS // seq)

    if state is None:
        m_state = s_state = None
    else:
        c0, n0, m0, s0 = state
        m_state = (c0, n0.reshape(nb, N_UNITS, M_DQK),
                   jnp.pad(m0.reshape(nb, 1, N_UNITS), ((0, 0), (0, 0), (0, LANES - N_UNITS))))
        s_state = s0.reshape(nb, N_DIR, S_PAIRS, LANES, S_STATE)
    hm, c_new, n_new, m_new, ys, s_new = _mixers(p, xbc, scans, w["g_mlstm_norm"], w["dskip"],
                                                 w["g_ssd_norm"], m_state, s_state, nb, nc)

    x1, u2 = _out_proj(hm.reshape(t, MIX_M), ys.reshape(t, MIX_S), w["w_out"], x2d, g1,
                       w["g_post_mix"], w["g_pre_mlp"], sc2, sh2, rows_per_mod)
    y = _mlp(u2, w["w_mlp_in"], w["w_mlp_out"], x1, g2, w["g_post_mlp"], rows_per_mod)
    new_state = (c_new.reshape(nb, 1, N_DIR, M_HEADS, M_DQK, M_DV),
                 n_new.reshape(nb, 1, N_DIR, M_HEADS, M_DQK),
                 m_new[:, 0, :N_UNITS].reshape(nb, 1, N_DIR, M_HEADS),
                 s_new.reshape(nb, 1, N_DIR, S_HEADS, S_HEADDIM, S_STATE))
    return y.reshape(nb, seq, D_MODEL), new_state


def kernel(x_prompt, x_sample, state_mlstm_c, state_mlstm_n, state_mlstm_m, state_ssd, c, c_ctx, w_mod, b_mod,
           g_pre_mix, g_post_mix, w_in, b_igate, b_fgate, conv_w, conv_b, dt_bias, a_log, d_skip, g_mlstm_norm,
           g_ssd_norm, w_out, g_pre_mlp, g_post_mlp, w_mlp_in, w_mlp_out):
    assert w_mod.shape[0] == 1, "one layer"
    nb_s = x_sample.shape[0]

    cond8 = jnp.zeros((8, D_MODEL), F32).at[0].set(c_ctx).at[1:1 + nb_s].set(c)
    mod = _modulation(cond8, w_mod[0], b_mod[0].reshape(1, -1))
    mods = [mod[:, k * D_MODEL:(k + 1) * D_MODEL] for k in range(6)]
    mods_p = [m[0:1].reshape(1, 1, D_MODEL) for m in mods]
    mods_s = [m[1:1 + nb_s].reshape(nb_s, 1, D_MODEL) for m in mods]

    w_t = w_in[0].T.astype(BF16)
    assert w_t.shape[0] == P_MAIN + N_GATES
    gate_rows = jnp.concatenate([w_t[P_Z:P_Z + G_DT], w_t[P_MAIN + G_DT:]], axis=0)
    gate_rows = jnp.pad(gate_rows, ((0, GATE_W - N_GATES), (0, 0)))
    zeros_u = jnp.zeros((N_UNITS,), F32)
    row = lambda v: v.reshape(1, -1)
    weights = dict(
        w_t=w_t, wg=gate_rows,
        g_pre_mix=row(g_pre_mix[0]), g_post_mix=row(g_post_mix[0]),
        g_pre_mlp=row(g_pre_mlp[0]), g_post_mlp=row(g_post_mlp[0]),
        conv_w9=conv_w[0].reshape(9, XBC), conv_b=row(conv_b[0]),
        gate_bias=_gate_row(b_igate[0], b_fgate[0], dt_bias[0]),
        gate_alog=_gate_row(zeros_u, zeros_u, a_log[0]),
        dskip=row(jnp.repeat(d_skip[0], S_HEADDIM)),
        g_mlstm_norm=row(g_mlstm_norm[0]), g_ssd_norm=row(g_ssd_norm[0]),
        w_out=w_out[0].astype(BF16), w_mlp_in=w_mlp_in[0], w_mlp_out=w_mlp_out[0])

    y_p, st = _block(x_prompt, mods_p, None, weights, x_prompt.shape[1])
    cache = (state_mlstm_c[:, 0], state_mlstm_n[:, 0], state_mlstm_m[:, 0], state_ssd[:, 0])
    y_s, _ = _block(x_sample, mods_s, cache, weights, GRID_W)
    return (y_p, y_s) + st
```

```python
import functools

import jax
import jax.numpy as jnp
from jax import lax
from jax.experimental import pallas as pl
from jax.experimental.pallas import tpu as pltpu

F32 = jnp.float32
BF16 = jnp.bfloat16

D_MODEL = 2048
CHUNK = 128
N_DIR = 2
M_HEADS = 4
M_DQK = 128
M_DV = 256
MIX_M = M_HEADS * M_DV
S_HEADS = 16
S_HEADDIM = 64
S_STATE = 128
S_GROUPS = 4
S_REP = S_HEADS // S_GROUPS
MIX_S = S_HEADS * S_HEADDIM
XBC = MIX_S + 2 * S_GROUPS * S_STATE
D_FF = 4 * D_MODEL
GRID_W = 64
EPS = 1e-6
LANES = 128

P_Q = 0
P_K = M_HEADS * M_DQK
P_V = 2 * M_HEADS * M_DQK
P_O = P_V + MIX_M
P_Z = P_O + MIX_M
P_XBC = P_Z + MIX_S
P_MAIN = P_XBC + XBC
GATE_W = LANES
N_UNITS = N_DIR * M_HEADS
G_I = 0
G_F = N_UNITS
G_DT = 2 * N_UNITS
N_GATES = G_DT + N_DIR * S_HEADS
G_WK = G_DT + N_DIR * S_HEADS
G_CU = G_WK + N_DIR * S_HEADS
G_B = G_CU + N_UNITS
assert G_B + N_UNITS <= GATE_W

PAIR = LANES // S_HEADDIM
S_PAIRS = S_HEADS // PAIR
PAIRS_PER_GROUP = S_REP // PAIR

VMEM_LIMIT = 48 * 1024 * 1024

_NT = (((1,), (1,)), ((), ()))


def _params(sem, limit=VMEM_LIMIT):
    return pltpu.CompilerParams(dimension_semantics=sem, vmem_limit_bytes=limit)


def _silu(x):
    return x / (1.0 + jnp.exp(-x))


def _sigmoid(x):
    return 1.0 / (1.0 + jnp.exp(-x))


def _rms(x):
    return x * lax.rsqrt(jnp.mean(x * x, axis=-1, keepdims=True) + EPS)


def _dot(a, b, dims=None, precision=None):
    if dims is None:
        dims = (((a.ndim - 1,), (0,)), ((), ()))
    return lax.dot_general(a, b, dims, precision=precision, preferred_element_type=F32)


def _lane_bcast(tile, lane):
    return jnp.broadcast_to(tile[:, lane:lane + 1], tile.shape)


def _mod_kernel(c_ref, w_ref, b_ref, o_ref):
    a = _silu(c_ref[...]).astype(BF16)
    o_ref[...] = _dot(a, w_ref[...].astype(BF16)) + b_ref[...]


def _modulation(cond8, w_mod, b_mod):
    n = w_mod.shape[1]
    tn = 1024
    return pl.pallas_call(
        _mod_kernel,
        grid=(n // tn,),
        in_specs=[pl.BlockSpec((8, D_MODEL), lambda j: (0, 0)),
                  pl.BlockSpec((D_MODEL, tn), lambda j: (0, j)),
                  pl.BlockSpec((1, tn), lambda j: (0, j))],
        out_specs=pl.BlockSpec((8, tn), lambda j: (0, j)),
        out_shape=jax.ShapeDtypeStruct((8, n), F32),
        compiler_params=_params(("parallel",)),
        name="modulation",
    )(cond8, w_mod, b_mod)


IN_TM = 1024
IN_TN = 1024
IN_SUB = 256


def _inproj_kernel(x_ref, sc_ref, sh_ref, g_ref, w_ref, wg_ref, o_ref, gc_ref, u_ref):
    j = pl.program_id(1)

    @pl.when(j == 0)
    def _():
        for r in range(IN_TM // IN_SUB):
            rows = slice(r * IN_SUB, (r + 1) * IN_SUB)
            y = _rms(x_ref[rows, :]) * g_ref[...]
            u = (y * (1.0 + sc_ref[...]) + sh_ref[...]).astype(BF16)
            u_ref[rows, :] = u
            o_ref[rows, :] = _dot(u, w_ref[...], _NT)

    @pl.when(j > 0)
    def _():
        o_ref[...] = _dot(u_ref[...], w_ref[...], _NT)

    @pl.when(j == pl.num_programs(1) - 1)
    def _():
        gc_ref[...] = _dot(u_ref[...], wg_ref[...], _NT)


def _in_proj(x2d, sc, sh, g, w_t, wg, rows_per_mod):
    t = x2d.shape[0]
    tiles_per_mod = rows_per_mod // IN_TM
    n_a = P_Z // IN_TN

    def first_row(i, j):
        return (j * (IN_TN // G_DT) + (j >= n_a).astype(jnp.int32)) * G_DT, 0
    return pl.pallas_call(
        _inproj_kernel,
        grid=(t // IN_TM, P_MAIN // IN_TN),
        in_specs=[pl.BlockSpec((IN_TM, D_MODEL), lambda i, j: (i, 0)),
                  pl.BlockSpec((None, 1, D_MODEL), lambda i, j: (i // tiles_per_mod, 0, 0)),
                  pl.BlockSpec((None, 1, D_MODEL), lambda i, j: (i // tiles_per_mod, 0, 0)),
                  pl.BlockSpec((1, D_MODEL), lambda i, j: (0, 0)),
                  pl.BlockSpec((pl.Element(IN_TN), pl.Element(D_MODEL)), first_row),
                  pl.BlockSpec((GATE_W, D_MODEL), lambda i, j: (0, 0))],
        out_specs=[pl.BlockSpec((IN_TM, IN_TN), lambda i, j: (i, j)),
                   pl.BlockSpec((IN_TM, GATE_W), lambda i, j: (i, 0))],
        out_shape=[jax.ShapeDtypeStruct((t, P_MAIN), F32),
                   jax.ShapeDtypeStruct((t, GATE_W), F32)],
        scratch_shapes=[pltpu.VMEM((IN_TM, D_MODEL), BF16)],
        compiler_params=_params(("parallel", "arbitrary"), 60 * 1024 * 1024),
        name="in_proj",
    )(x2d, sc, sh, g, w_t, wg)


def _conv_kernel(x_ref, w_ref, b_ref, o_ref, *, seq, width):
    x = x_ref[...]
    t = lax.broadcasted_iota(jnp.int32, x.shape, 0)
    c = jnp.bitwise_and(t, width - 1)
    xl = jnp.where(c >= 1, pltpu.roll(x, 1, 0), 0.0)
    xr = jnp.where(c <= width - 2, pltpu.roll(x, seq - 1, 0), 0.0)

    def taps(di):
        return w_ref[3 * di:3 * di + 1, :] * xl + w_ref[3 * di + 1:3 * di + 2, :] * x \
            + w_ref[3 * di + 2:3 * di + 3, :] * xr

    out = taps(1) + b_ref[...]
    if seq > width:
        out = out + jnp.where(t >= width, pltpu.roll(taps(0), width, 0), 0.0)
        out = out + jnp.where(t < seq - width, pltpu.roll(taps(2), seq - width, 0), 0.0)
    o_ref[...] = _silu(out)


CONV_BLOCK_ELEMS = 512 * 1024


def _conv(p, conv_w9, conv_b, nb, seq, width, cn):
    cn = min(cn, XBC)
    col0 = P_XBC // cn
    return pl.pallas_call(
        functools.partial(_conv_kernel, seq=seq, width=width),
        grid=(nb, XBC // cn),
        in_specs=[pl.BlockSpec((seq, cn), lambda b, j: (b, col0 + j)),
                  pl.BlockSpec((9, cn), lambda b, j: (0, j)),
                  pl.BlockSpec((1, cn), lambda b, j: (0, j))],
        out_specs=pl.BlockSpec((seq, cn), lambda b, j: (b, j)),
        out_shape=jax.ShapeDtypeStruct((nb * seq, XBC), F32),
        compiler_params=_params(("parallel", "parallel")),
        name="grid_conv",
    )(p, conv_w9, conv_b)


def _tri_masks():
    r = lax.broadcasted_iota(jnp.int32, (CHUNK, CHUNK), 0)
    c = lax.broadcasted_iota(jnp.int32, (CHUNK, CHUNK), 1)
    return r >= c, r <= c


SCAN_CHUNKS = 8


def _scan_kernel(gc_ref, bias_ref, alog_ref, col_ref, bcol_ref, row_ref):
    lower, upper = _tri_masks()
    lo, up = lower.astype(F32), upper.astype(F32)
    hi = lax.Precision.HIGHEST
    lane = lax.broadcasted_iota(jnp.int32, (CHUNK, GATE_W), 1)
    time = lax.broadcasted_iota(jnp.int32, (CHUNK, GATE_W), 0)
    lane1 = lax.broadcasted_iota(jnp.int32, (1, GATE_W), 1)

    def backward(l):
        unit_bwd = (l < G_DT) & (jnp.bitwise_and(l, N_UNITS - 1) >= M_HEADS)
        return unit_bwd | ((l >= G_DT + S_HEADS) & (l < N_GATES))

    is_bwd, is_bwd1 = backward(lane), backward(lane1)
    is_i = lane < G_F
    is_f = (lane >= G_F) & (lane < G_DT)
    is_dt = (lane >= G_DT) & (lane < N_GATES)
    neg_a = -jnp.exp(alog_ref[...])
    for c in range(SCAN_CHUNKS):
        rows = slice(c * CHUNK, (c + 1) * CHUNK)
        g = gc_ref[rows, :] + bias_ref[...]
        soft = jnp.log1p(jnp.exp(-jnp.abs(g)))
        logf = jnp.minimum(g, 0.0) - soft
        dt = jnp.maximum(g, 0.0) + soft
        x = jnp.where(is_f, logf, jnp.where(is_dt, dt * neg_a, 0.0))
        cs = jnp.where(is_bwd, _dot(up, x, precision=hi), _dot(lo, x, precision=hi))
        b_units = pltpu.roll(cs, GATE_W - G_F, 1)
        u = g - b_units
        cu = u
        k = 1
        while k < CHUNK:
            prev = jnp.where(time >= k, pltpu.roll(cu, k, 0), -jnp.inf)
            nxt = jnp.where(time < CHUNK - k, pltpu.roll(cu, CHUNK - k, 0), -jnp.inf)
            cu = jnp.maximum(cu, jnp.where(is_bwd, nxt, prev))
            k *= 2
        total = jnp.where(is_bwd1, cs[0:1, :], cs[CHUNK - 1:CHUNK, :])
        wk = jnp.exp(total - cs) * dt
        q = cs - jnp.log(dt)
        col_ref[rows, :] = jnp.where(is_i, cu, cs)
        bcol_ref[rows, :] = b_units
        tail = jnp.where(lane < G_CU, pltpu.roll(wk, G_WK - G_DT, 1),
                         jnp.where(lane < G_B, pltpu.roll(cu, G_CU, 1), pltpu.roll(b_units, G_B, 1)))
        row_ref[rows, :] = jnp.where(is_i, u, jnp.where(is_dt, q, tail)).T


def _gate_scans(gc, bias, alog):
    t = gc.shape[0]
    tm = SCAN_CHUNKS * CHUNK
    row = lambda i: (i, 0)
    const = lambda i: (0, 0)
    return pl.pallas_call(
        _scan_kernel,
        grid=(t // tm,),
        in_specs=[pl.BlockSpec((tm, GATE_W), row), pl.BlockSpec((1, GATE_W), const),
                  pl.BlockSpec((1, GATE_W), const)],
        out_specs=[pl.BlockSpec((tm, GATE_W), row)] * 3,
        out_shape=[jax.ShapeDtypeStruct((t, GATE_W), F32)] * 3,
        compiler_params=_params(("parallel",)),
        name="gate_scans",
    )(gc, bias, alog)


N_ROWS = 16
_WORK, _DONE = "work", "done"


def _mlstm_phases(refs, nc, zero_init):
    (qf, kf, vf, of, colf, bcolf, rowf, qb, kb, vb, ob, colb, bcolb, rowb, gn) = refs[:15]
    pos = 15
    if not zero_init:
        c0, n0, m0 = refs[pos:pos + 3]
        pos += 3
    hm_out, c_out, n_out, m_out, c_s, n_s, m_s, hpart, hbuf = refs[pos:]
    s = pl.program_id(1)
    half = nc // 2

    @pl.when(s == 0)
    def _():
        hpart[...] = jnp.zeros_like(hpart)
        if zero_init:
            c_s[...] = jnp.zeros_like(c_s)
            n_s[...] = jnp.zeros_like(n_s)
            m_s[...] = jnp.zeros_like(m_s)
        else:
            c_s[...] = c0[...]
            m_s[...] = m0[...]
            for u in range(N_UNITS):
                n_s[u] = jnp.broadcast_to(n0[u:u + 1, :], (N_ROWS, M_DQK))

    yield _DONE
    lower, upper = _tri_masks()
    lane1 = lax.broadcasted_iota(jnp.int32, (1, LANES), 1)
    ones_rows = jnp.ones((N_ROWS, CHUNK), BF16)
    per_dir = ((qf, kf, vf, of, colf, bcolf, rowf, lower, s),
               (qb, kb, vb, ob, colb, bcolb, rowb, upper, nc - 1 - s))
    m_prev = m_s[...]
    m_next = m_prev
    c_new, n_new = [], []
    for d, (q_ref, k_ref, v_ref, o_ref, col_ref, bcol_ref, row_ref, mask, chunk) in enumerate(per_dir):
        last = CHUNK - 1 if d == 0 else 0
        stab = jnp.maximum(m_prev, col_ref[...])
        w_c = jnp.exp(m_prev - stab)
        stab_last = stab[last:last + 1, :]
        m_new = bcol_ref[last:last + 1, :] + stab_last
        decay = w_c[last:last + 1, :]
        c_old = [c_s[d, h] for h in range(M_HEADS)]
        n_old = [n_s[d * M_HEADS + h] for h in range(M_HEADS)]
        for h in range(M_HEADS):
            u = d * M_HEADS + h
            stab_b = _lane_bcast(stab, u)
            u_row = row_ref[u:u + 1, :]
            p = jnp.exp(jnp.where(mask, u_row - stab_b, -jnp.inf))
            qh = q_ref[:, h * M_DQK:(h + 1) * M_DQK] * (M_DQK ** -0.5)
            k_t = k_ref[:, h * M_DQK:(h + 1) * M_DQK].T
            v16 = v_ref[:, h * M_DV:(h + 1) * M_DV].astype(BF16)
            sc = _dot(qh.astype(BF16), k_t.astype(BF16)) * p
            lhs = jnp.concatenate([sc.astype(BF16), (qh * _lane_bcast(w_c, u)).astype(BF16)], axis=1)
            num = _dot(lhs, jnp.concatenate([v16, c_old[h].astype(BF16)], axis=0))
            den = _dot(jnp.concatenate([ones_rows, n_old[h].astype(BF16)], axis=1), lhs, _NT)[0:1, :]
            m_t = row_ref[G_B + u:G_B + u + 1, :] + jnp.maximum(m_prev[:, u:u + 1],
                                                                row_ref[G_CU + u:G_CU + u + 1, :])
            inv_row = 1.0 / jnp.maximum(jnp.abs(den), jnp.exp(-m_t))
            inv = jnp.broadcast_to(inv_row, (CHUNK, LANES)).T
            hbuf[d, :, h * M_DV:(h + 1) * M_DV] = num * jnp.concatenate([inv, inv], axis=1)

            kw_t = (k_t * jnp.exp(u_row - stab_last[:, u:u + 1])).astype(BF16)
            dec = decay[:, u:u + 1]
            c_new.append(dec * c_old[h] + _dot(kw_t, v16))
            n_new.append(dec * n_old[h] + _dot(ones_rows, kw_t, _NT))
            yield _WORK
        mine = (lane1 >= d * M_HEADS) & (lane1 < (d + 1) * M_HEADS)
        m_next = jnp.where(mine, m_new, m_next)
    for u in range(N_UNITS):
        c_s[u // M_HEADS, u % M_HEADS] = c_new[u]
        n_s[u] = n_new[u]
    m_s[...] = m_next

    yield _DONE
    for d, (_, _, _, o_ref, _, _, _, _, chunk) in enumerate(per_dir):
        rows = pl.ds(pl.multiple_of(chunk * CHUNK, CHUNK), CHUNK)
        for h in range(M_HEADS):
            cols = slice(h * M_DV, (h + 1) * M_DV)
            tot = hbuf[d, :, cols] + hpart[rows, cols]
            hm_out[rows, cols] = (_rms(tot) * gn[:, cols] * _sigmoid(o_ref[:, cols])).astype(BF16)
        hpart[rows, :] = hbuf[d]

    @pl.when(s == nc - 1)
    def _():
        c_out[...] = c_s[...]
        m_out[...] = m_s[...]
        for u in range(N_UNITS):
            n_out[u:u + 1, :] = n_s[u][0:1, :]

    yield _DONE


def _sequence_out_spec(seq, width):
    mode = dict(pipeline_mode=pl.Buffered(1)) if seq * width * 2 >= 2 * 1024 * 1024 else {}
    return pl.BlockSpec((None, seq, width), lambda b, s: (b, 0, 0), **mode)


def _mlstm_specs(p, scans, g_norm, state, nb, nc):
    zero_init = state is None
    seq = nc * CHUNK
    col, bcol, row = scans

    def fwd(cb):
        return lambda b, s: (b * nc + s, cb)

    def bwd(cb):
        return lambda b, s: (b * nc + nc - 1 - s, cb)

    def stream(mk):
        return [pl.BlockSpec((CHUNK, M_HEADS * M_DQK), mk(P_Q // (M_HEADS * M_DQK))),
                pl.BlockSpec((CHUNK, M_HEADS * M_DQK), mk(P_K // (M_HEADS * M_DQK))),
                pl.BlockSpec((CHUNK, MIX_M), mk(P_V // MIX_M)),
                pl.BlockSpec((CHUNK, MIX_M), mk(P_O // MIX_M)),
                pl.BlockSpec((CHUNK, GATE_W), mk(0)),
                pl.BlockSpec((CHUNK, GATE_W), mk(0)),
                pl.BlockSpec((CHUNK, GATE_W), mk(0))]

    in_specs = stream(fwd) + stream(bwd) + [pl.BlockSpec((1, MIX_M), lambda b, s: (0, 0))]
    args = [p] * 4 + [col, bcol, row] + [p] * 4 + [col, bcol, row, g_norm]
    state_specs = [pl.BlockSpec((None, N_DIR, M_HEADS, M_DQK, M_DV), lambda b, s: (b, 0, 0, 0, 0)),
                   pl.BlockSpec((None, N_UNITS, M_DQK), lambda b, s: (b, 0, 0)),
                   pl.BlockSpec((None, 1, LANES), lambda b, s: (b, 0, 0))]
    if not zero_init:
        in_specs += state_specs
        args += list(state)
    out_specs = [_sequence_out_spec(seq, MIX_M)] + state_specs
    out_shape = [jax.ShapeDtypeStruct((nb, seq, MIX_M), BF16),
                 jax.ShapeDtypeStruct((nb, N_DIR, M_HEADS, M_DQK, M_DV), F32),
                 jax.ShapeDtypeStruct((nb, N_UNITS, M_DQK), F32),
                 jax.ShapeDtypeStruct((nb, 1, LANES), F32)]
    scratch = [pltpu.VMEM((N_DIR, M_HEADS, M_DQK, M_DV), F32),
               pltpu.VMEM((N_UNITS, N_ROWS, M_DQK), F32),
               pltpu.VMEM((1, LANES), F32),
               pltpu.VMEM((seq, MIX_M), F32),
               pltpu.VMEM((N_DIR, CHUNK, MIX_M), F32)]
    return in_specs, args, out_specs, out_shape, scratch


def _ssd_phases(refs, nc, zero_init):
    (xf, bf, cf, zf, colf, rowf, xb, bb, cb, zb, colb, rowb, dskip, gn) = refs[:14]
    pos = 14
    if not zero_init:
        s0 = refs[pos]
        pos += 1
    ys_out, s_out, st_s, ypart, ybuf = refs[pos:]
    s = pl.program_id(1)
    half = nc // 2

    @pl.when(s == 0)
    def _():
        ypart[...] = jnp.zeros_like(ypart)
        if zero_init:
            st_s[...] = jnp.zeros_like(st_s)
        else:
            for d in range(N_DIR):
                for pr in range(S_PAIRS):
                    st_s[d, pr] = s0[d, pr].T

    yield _DONE
    lower, upper = _tri_masks()
    low_half = lax.broadcasted_iota(jnp.int32, (CHUNK, LANES), 1) < S_HEADDIM
    low_half1 = lax.broadcasted_iota(jnp.int32, (1, LANES), 1) < S_HEADDIM
    per_dir = ((xf, bf, cf, zf, colf, rowf, lower, s), (xb, bb, cb, zb, colb, rowb, upper, nc - 1 - s))
    st_new = []
    for d, (x_ref, b_ref, c_ref, z_ref, col_ref, row_ref, mask, chunk) in enumerate(per_dir):
        last = CHUNK - 1 if d == 0 else 0
        cs_c = col_ref[...]
        total = cs_c[last:last + 1, :]
        st_old = [st_s[d, pr] for pr in range(S_PAIRS)]
        for g in range(S_GROUPS):
            gs = slice(g * S_STATE, (g + 1) * S_STATE)
            cg = c_ref[:, gs].astype(BF16)
            b_f32 = b_ref[:, gs]
            cbm = _dot(cg, b_f32.astype(BF16), _NT)
            b_t = b_f32.T
            for pg in range(PAIRS_PER_GROUP):
                pr = g * PAIRS_PER_GROUP + pg
                cols = slice(pr * LANES, (pr + 1) * LANES)
                x16 = x_ref[:, cols].astype(BF16)
                zero16 = jnp.zeros_like(x16)
                halves = (jnp.where(low_half, x16, zero16), jnp.where(low_half, zero16, x16))
                state = st_old[pr]
                y_acc = jnp.zeros((CHUNK, LANES), F32)
                s_acc = jnp.zeros((S_STATE, LANES), F32)
                cs_b = []
                for e in range(PAIR):
                    gi = G_DT + d * S_HEADS + pr * PAIR + e
                    cs_b.append(_lane_bcast(cs_c, gi))
                    q_row = row_ref[gi:gi + 1, :]
                    wk_row = row_ref[gi + G_WK - G_DT:gi + G_WK - G_DT + 1, :]
                    mix = cbm * jnp.exp(jnp.where(mask, cs_b[e] - q_row, -jnp.inf))
                    y_acc = y_acc + _dot(mix.astype(BF16), halves[e])
                    s_acc = s_acc + _dot((b_t * wk_row).astype(BF16), halves[e])
                carry = jnp.exp(jnp.where(low_half, cs_b[0], cs_b[1]))
                ybuf[d, :, cols] = y_acc + _dot(cg, state.astype(BF16)) * carry
                gi0 = G_DT + d * S_HEADS + pr * PAIR
                dec = jnp.exp(jnp.where(low_half1, jnp.broadcast_to(total[:, gi0:gi0 + 1], (1, LANES)),
                                        jnp.broadcast_to(total[:, gi0 + 1:gi0 + 2], (1, LANES))))
                st_new.append(state * dec + s_acc)
            yield _WORK
    for d in range(N_DIR):
        for pr in range(S_PAIRS):
            st_s[d, pr] = st_new[d * S_PAIRS + pr]

    yield _DONE
    for d, (x_ref, _, _, z_ref, _, _, _, chunk) in enumerate(per_dir):
        rows = pl.ds(pl.multiple_of(chunk * CHUNK, CHUNK), CHUNK)
        y = ybuf[d] + ypart[rows, :] + dskip[...] * x_ref[...]
        ys_out[rows, :] = (_rms(y * _silu(z_ref[...])) * gn[...]).astype(BF16)
        ypart[rows, :] = ybuf[d]

    @pl.when(s == nc - 1)
    def _():
        for d in range(N_DIR):
            for pr in range(S_PAIRS):
                s_out[d, pr] = st_s[d, pr].T

    yield _DONE


def _ssd_specs(p, xbc, scans, dskip, g_norm, state, nb, nc):
    zero_init = state is None
    seq = nc * CHUNK
    bc_w = S_GROUPS * S_STATE
    col, _, row = scans

    def fwd(cb):
        return lambda b, s: (b * nc + s, cb)

    def bwd(cb):
        return lambda b, s: (b * nc + nc - 1 - s, cb)

    def stream(mk):
        return [pl.BlockSpec((CHUNK, MIX_S), mk(0)),
                pl.BlockSpec((CHUNK, bc_w), mk(MIX_S // bc_w)),
                pl.BlockSpec((CHUNK, bc_w), mk(MIX_S // bc_w + 1)),
                pl.BlockSpec((CHUNK, MIX_S), mk(P_Z // MIX_S)),
                pl.BlockSpec((CHUNK, GATE_W), mk(0)),
                pl.BlockSpec((CHUNK, GATE_W), mk(0))]

    const = lambda b, s: (0, 0)
    in_specs = stream(fwd) + stream(bwd) + [pl.BlockSpec((1, MIX_S), const), pl.BlockSpec((1, MIX_S), const)]
    args = [xbc, xbc, xbc, p, col, row] * 2 + [dskip, g_norm]
    state_spec = pl.BlockSpec((None, N_DIR, S_PAIRS, LANES, S_STATE), lambda b, s: (b, 0, 0, 0, 0))
    if not zero_init:
        in_specs.append(state_spec)
        args.append(state)
    out_specs = [_sequence_out_spec(seq, MIX_S), state_spec]
    out_shape = [jax.ShapeDtypeStruct((nb, seq, MIX_S), BF16),
                 jax.ShapeDtypeStruct((nb, N_DIR, S_PAIRS, LANES, S_STATE), F32)]
    scratch = [pltpu.VMEM((N_DIR, S_PAIRS, S_STATE, LANES), F32),
               pltpu.VMEM((seq, MIX_S), F32),
               pltpu.VMEM((N_DIR, CHUNK, MIX_S), F32)]
    return in_specs, args, out_specs, out_shape, scratch


def _mixer_kernel(*refs, nc, zero_init, counts):
    (m_in, s_in), (m_out, s_out), (m_scr, s_scr) = counts
    pos = 0
    parts = []
    for n in (m_in, s_in, m_out, s_out, m_scr, s_scr):
        parts.append(refs[pos:pos + n])
        pos += n
    gens = (_mlstm_phases(parts[0] + parts[2] + parts[4], nc, zero_init),
            _ssd_phases(parts[1] + parts[3] + parts[5], nc, zero_init))
    for _ in range(3):
        live = list(gens)
        while live:
            live = [g for g in live if next(g) is not _DONE]


def _mixers(p, xbc, scans, g_mlstm, dskip, g_ssd, mlstm_state, ssd_state, nb, nc):
    m = _mlstm_specs(p, scans, g_mlstm, mlstm_state, nb, nc)
    s = _ssd_specs(p, xbc, scans, dskip, g_ssd, ssd_state, nb, nc)
    counts = tuple((len(a), len(b)) for a, b in ((m[0], s[0]), (m[2], s[2]), (m[4], s[4])))
    return pl.pallas_call(
        functools.partial(_mixer_kernel, nc=nc, zero_init=mlstm_state is None, counts=counts),
        grid=(nb, nc),
        in_specs=m[0] + s[0],
        out_specs=m[2] + s[2],
        out_shape=m[3] + s[3],
        scratch_shapes=m[4] + s[4],
        compiler_params=_params(("parallel", "arbitrary"), 60 * 1024 * 1024),
        name="mixers",
    )(*(m[1] + s[1]))


OUT_TM = 512


OUT_SUB = 256


def _outproj_kernel(hm_ref, ys_ref, w_ref, x_ref, g1_ref, gpost_ref, gpre_ref, sc_ref, sh_ref, x1_ref, u2_ref):
    for r in range(OUT_TM // OUT_SUB):
        rows = slice(r * OUT_SUB, (r + 1) * OUT_SUB)
        mix = _dot(hm_ref[rows, :], w_ref[0:MIX_M, :]) + _dot(ys_ref[rows, :], w_ref[MIX_M:, :])
        x1 = x_ref[rows, :] + g1_ref[...] * (_rms(mix) * gpost_ref[...])
        x1_ref[rows, :] = x1
        u2_ref[rows, :] = (_rms(x1) * gpre_ref[...] * (1.0 + sc_ref[...]) + sh_ref[...]).astype(BF16)


def _out_proj(hm, ys, w_out16, x2d, g1, gpost, gpre, sc2, sh2, rows_per_mod):
    t = x2d.shape[0]
    tiles_per_mod = rows_per_mod // OUT_TM
    row = lambda i: (i, 0)
    const = lambda i: (0, 0)
    mod = pl.BlockSpec((None, 1, D_MODEL), lambda i: (i // tiles_per_mod, 0, 0))
    return pl.pallas_call(
        _outproj_kernel,
        grid=(t // OUT_TM,),
        in_specs=[pl.BlockSpec((OUT_TM, MIX_M), row), pl.BlockSpec((OUT_TM, MIX_S), row),
                  pl.BlockSpec((D_MODEL, D_MODEL), const), pl.BlockSpec((OUT_TM, D_MODEL), row),
                  mod, pl.BlockSpec((1, D_MODEL), const), pl.BlockSpec((1, D_MODEL), const), mod, mod],
        out_specs=[pl.BlockSpec((OUT_TM, D_MODEL), row), pl.BlockSpec((OUT_TM, D_MODEL), row)],
        out_shape=[jax.ShapeDtypeStruct((t, D_MODEL), F32), jax.ShapeDtypeStruct((t, D_MODEL), BF16)],
        compiler_params=_params(("parallel",)),
        name="out_proj",
    )(hm, ys, w_out16, x2d, g1, gpost, gpre, sc2, sh2)


MLP_TM = 1024
MLP_TH = 512
MLP_SUB = 256


def _mlp_kernel(u_ref, w1_ref, w2_ref, x1_hbm, g2_ref, gpost_ref, o_ref, x1_buf, x1_sem):
    i = pl.program_id(0)
    j = pl.program_id(1)

    def x1_copy():
        rows = pl.ds(pl.multiple_of(i * MLP_TM, MLP_TM), MLP_TM)
        return pltpu.make_async_copy(x1_hbm.at[rows, :], x1_buf, x1_sem)

    @pl.when(j == 0)
    def _():
        x1_copy().start()
        o_ref[...] = jnp.zeros_like(o_ref)

    w1 = w1_ref[...].astype(BF16)
    w2 = w2_ref[...].astype(BF16)
    for r in range(MLP_TM // MLP_SUB):
        rows = slice(r * MLP_SUB, (r + 1) * MLP_SUB)
        hid = jnp.square(jnp.maximum(_dot(u_ref[rows, :], w1), 0.0)).astype(BF16)
        o_ref[rows, :] += _dot(hid, w2)

    @pl.when(j == pl.num_programs(1) - 1)
    def _():
        x1_copy().wait()

        def fin(r, carry):
            rows = pl.ds(pl.multiple_of(r * MLP_SUB, MLP_SUB), MLP_SUB)
            o_ref[rows, :] = x1_buf[rows, :] + g2_ref[...] * (_rms(o_ref[rows, :]) * gpost_ref[...])
            return carry
        lax.fori_loop(0, MLP_TM // MLP_SUB, fin, 0)


def _mlp(u2, w1, w2, x1, g2, gpost, rows_per_mod):
    t = u2.shape[0]
    tiles_per_mod = rows_per_mod // MLP_TM
    return pl.pallas_call(
        _mlp_kernel,
        grid=(t // MLP_TM, D_FF // MLP_TH),
        in_specs=[pl.BlockSpec((MLP_TM, D_MODEL), lambda i, j: (i, 0)),
                  pl.BlockSpec((D_MODEL, MLP_TH), lambda i, j: (0, j)),
                  pl.BlockSpec((MLP_TH, D_MODEL), lambda i, j: (j, 0)),
                  pl.BlockSpec(memory_space=pl.ANY),
                  pl.BlockSpec((None, 1, D_MODEL), lambda i, j: (i // tiles_per_mod, 0, 0)),
                  pl.BlockSpec((1, D_MODEL), lambda i, j: (0, 0))],
        out_specs=pl.BlockSpec((MLP_TM, D_MODEL), lambda i, j: (i, 0)),
        out_shape=jax.ShapeDtypeStruct((t, D_MODEL), F32),
        scratch_shapes=[pltpu.VMEM((MLP_TM, D_MODEL), F32), pltpu.SemaphoreType.DMA(())],
        compiler_params=_params(("parallel", "arbitrary"), 60 * 1024 * 1024),
        name="mlp",
    )(u2, w1, w2, x1, g2, gpost)


def _gate_row(i_vals, f_vals, dt_vals):
    v = jnp.concatenate([i_vals.reshape(-1), f_vals.reshape(-1), dt_vals.reshape(-1)]).astype(F32)
    return jnp.pad(v, (0, GATE_W - N_GATES)).reshape(1, GATE_W)


def _block(x, mods, state, weights, width):
    nb, seq, _ = x.shape
    nc = seq // CHUNK
    t = nb * seq
    x2d = x.reshape(t, D_MODEL)
    sh1, sc1, g1, sh2, sc2, g2 = mods
    rows_per_mod = t // sh1.shape[0]
    w = weights

    p, gc = _in_proj(x2d, sc1, sh1, w["g_pre_mix"], w["w_t"], w["wg"], rows_per_mod)
    scans = _gate_scans(gc, w["gate_bias"], w["gate_alog"])
    xbc = _conv(p, w["conv_w9"], w["conv_b"], nb, seq, width, CONV_BLOCK_ELEMS // seq)

    if state is None:
        m_state = s_state = None
    else:
        c0, n0, m0, s0 = state
        m_state = (c0, n0.reshape(nb, N_UNITS, M_DQK),
                   jnp.pad(m0.reshape(nb, 1, N_UNITS), ((0, 0), (0, 0), (0, LANES - N_UNITS))))
        s_state = s0.reshape(nb, N_DIR, S_PAIRS, LANES, S_STATE)
    hm, c_new, n_new, m_new, ys, s_new = _mixers(p, xbc, scans, w["g_mlstm_norm"], w["dskip"],
                                                 w["g_ssd_norm"], m_state, s_state, nb, nc)

    x1, u2 = _out_proj(hm.reshape(t, MIX_M), ys.reshape(t, MIX_S), w["w_out"], x2d, g1,
                       w["g_post_mix"], w["g_pre_mlp"], sc2, sh2, rows_per_mod)
    y = _mlp(u2, w["w_mlp_in"], w["w_mlp_out"], x1, g2, w["g_post_mlp"], rows_per_mod)
    new_state = (c_new.reshape(nb, 1, N_DIR, M_HEADS, M_DQK, M_DV),
                 n_new.reshape(nb, 1, N_DIR, M_HEADS, M_DQK),
                 m_new[:, 0, :N_UNITS].reshape(nb, 1, N_DIR, M_HEADS),
                 s_new.reshape(nb, 1, N_DIR, S_HEADS, S_HEADDIM, S_STATE))
    return y.reshape(nb, seq, D_MODEL), new_state


def kernel(x_prompt, x_sample, state_mlstm_c, state_mlstm_n, state_mlstm_m, state_ssd, c, c_ctx, w_mod, b_mod,
           g_pre_mix, g_post_mix, w_in, b_igate, b_fgate, conv_w, conv_b, dt_bias, a_log, d_skip, g_mlstm_norm,
           g_ssd_norm, w_out, g_pre_mlp, g_post_mlp, w_mlp_in, w_mlp_out):
    assert w_mod.shape[0] == 1, "one layer"
    nb_s = x_sample.shape[0]

    cond8 = jnp.zeros((8, D_MODEL), F32).at[0].set(c_ctx).at[1:1 + nb_s].set(c)
    mod = _modulation(cond8, w_mod[0], b_mod[0].reshape(1, -1))
    mods = [mod[:, k * D_MODEL:(k + 1) * D_MODEL] for k in range(6)]
    mods_p = [m[0:1].reshape(1, 1, D_MODEL) for m in mods]
    mods_s = [m[1:1 + nb_s].reshape(nb_s, 1, D_MODEL) for m in mods]

    w_t = w_in[0].T.astype(BF16)
    assert w_t.shape[0] == P_MAIN + N_GATES
    gate_rows = jnp.concatenate([w_t[P_Z:P_Z + G_DT], w_t[P_MAIN + G_DT:]], axis=0)
    gate_rows = jnp.pad(gate_rows, ((0, GATE_W - N_GATES), (0, 0)))
    zeros_u = jnp.zeros((N_UNITS,), F32)
    row = lambda v: v.reshape(1, -1)
    weights = dict(
        w_t=w_t, wg=gate_rows,
        g_pre_mix=row(g_pre_mix[0]), g_post_mix=row(g_post_mix[0]),
        g_pre_mlp=row(g_pre_mlp[0]), g_post_mlp=row(g_post_mlp[0]),
        conv_w9=conv_w[0].reshape(9, XBC), conv_b=row(conv_b[0]),
        gate_bias=_gate_row(b_igate[0], b_fgate[0], dt_bias[0]),
        gate_alog=_gate_row(zeros_u, zeros_u, a_log[0]),
        dskip=row(jnp.repeat(d_skip[0], S_HEADDIM)),
        g_mlstm_norm=row(g_mlstm_norm[0]), g_ssd_norm=row(g_ssd_norm[0]),
        w_out=w_out[0].astype(BF16), w_mlp_in=w_mlp_in[0], w_mlp_out=w_mlp_out[0])

    y_p, st = _block(x_prompt, mods_p, None, weights, x_prompt.shape[1])
    cache = (state_mlstm_c[:, 0], state_mlstm_n[:, 0], state_mlstm_m[:, 0], state_ssd[:, 0])
    y_s, _ = _block(x_sample, mods_s, cache, weights, GRID_W)
    return (y_p, y_s) + st
```

```python
import functools

import jax
import jax.numpy as jnp
from jax import lax
from jax.experimental import pallas as pl
from jax.experimental.pallas import tpu as pltpu

F32 = jnp.float32
BF16 = jnp.bfloat16

D_MODEL = 2048
CHUNK = 128
N_DIR = 2
M_HEADS = 4
M_DQK = 128
M_DV = 256
MIX_M = M_HEADS * M_DV
S_HEADS = 16
S_HEADDIM = 64
S_STATE = 128
S_GROUPS = 4
S_REP = S_HEADS // S_GROUPS
MIX_S = S_HEADS * S_HEADDIM
XBC = MIX_S + 2 * S_GROUPS * S_STATE
D_FF = 4 * D_MODEL
GRID_W = 64
EPS = 1e-6
LANES = 128

P_Q = 0
P_K = M_HEADS * M_DQK
P_V = 2 * M_HEADS * M_DQK
P_O = P_V + MIX_M
P_Z = P_O + MIX_M
P_XBC = P_Z + MIX_S
P_MAIN = P_XBC + XBC
GATE_W = LANES
N_UNITS = N_DIR * M_HEADS
G_I = 0
G_F = N_UNITS
G_DT = 2 * N_UNITS
N_GATES = G_DT + N_DIR * S_HEADS
G_WK = G_DT + N_DIR * S_HEADS
G_CU = G_WK + N_DIR * S_HEADS
G_B = G_CU + N_UNITS
assert G_B + N_UNITS <= GATE_W

PAIR = LANES // S_HEADDIM
S_PAIRS = S_HEADS // PAIR
PAIRS_PER_GROUP = S_REP // PAIR

VMEM_LIMIT = 48 * 1024 * 1024

_NT = (((1,), (1,)), ((), ()))


def _params(sem, limit=VMEM_LIMIT):
    return pltpu.CompilerParams(dimension_semantics=sem, vmem_limit_bytes=limit)


def _silu(x):
    return x / (1.0 + jnp.exp(-x))


def _sigmoid(x):
    return 1.0 / (1.0 + jnp.exp(-x))


def _rms(x):
    return x * lax.rsqrt(jnp.mean(x * x, axis=-1, keepdims=True) + EPS)


def _dot(a, b, dims=None, precision=None):
    if dims is None:
        dims = (((a.ndim - 1,), (0,)), ((), ()))
    return lax.dot_general(a, b, dims, precision=precision, preferred_element_type=F32)


def _lane_bcast(tile, lane):
    return jnp.broadcast_to(tile[:, lane:lane + 1], tile.shape)


def _mod_kernel(c_ref, w_ref, b_ref, o_ref):
    a = _silu(c_ref[...]).astype(BF16)
    o_ref[...] = _dot(a, w_ref[...].astype(BF16)) + b_ref[...]


def _modulation(cond8, w_mod, b_mod):
    n = w_mod.shape[1]
    tn = 1024
    return pl.pallas_call(
        _mod_kernel,
        grid=(n // tn,),
        in_specs=[pl.BlockSpec((8, D_MODEL), lambda j: (0, 0)),
                  pl.BlockSpec((D_MODEL, tn), lambda j: (0, j)),
                  pl.BlockSpec((1, tn), lambda j: (0, j))],
        out_specs=pl.BlockSpec((8, tn), lambda j: (0, j)),
        out_shape=jax.ShapeDtypeStruct((8, n), F32),
        compiler_params=_params(("parallel",)),
        name="modulation",
    )(cond8, w_mod, b_mod)


IN_TM = 1024
IN_TN = 1024
IN_SUB = 256


def _inproj_kernel(x_ref, sc_ref, sh_ref, g_ref, w_ref, wg_ref, o_ref, gc_ref, u_ref):
    j = pl.program_id(1)

    j_o, j_z = P_O // IN_TN, P_Z // IN_TN

    @pl.when(j == 0)
    def _():
        for r in range(IN_TM // IN_SUB):
            rows = slice(r * IN_SUB, (r + 1) * IN_SUB)
            y = _rms(x_ref[rows, :]) * g_ref[...]
            u = (y * (1.0 + sc_ref[...]) + sh_ref[...]).astype(BF16)
            u_ref[rows, :] = u
            o_ref[rows, :] = _dot(u, w_ref[...], _NT)

    @pl.when((j > 0) & (j != j_o) & (j != j_z))
    def _():
        o_ref[...] = _dot(u_ref[...], w_ref[...], _NT)

    @pl.when((j == j_o) | (j == j_z))
    def _():
        acc = _dot(u_ref[...], w_ref[...], _NT)
        sig = _sigmoid(acc)
        o_ref[...] = jnp.where(j == j_o, sig, acc * sig)

    @pl.when(j == pl.num_programs(1) - 1)
    def _():
        gc_ref[...] = _dot(u_ref[...], wg_ref[...], _NT)


def _in_proj(x2d, sc, sh, g, w_t, wg, rows_per_mod):
    t = x2d.shape[0]
    tiles_per_mod = rows_per_mod // IN_TM
    n_a = P_Z // IN_TN
    assert P_O % IN_TN == 0 and MIX_M == IN_TN and MIX_S == IN_TN

    def first_row(i, j):
        return (j * (IN_TN // G_DT) + (j >= n_a).astype(jnp.int32)) * G_DT, 0
    return pl.pallas_call(
        _inproj_kernel,
        grid=(t // IN_TM, P_MAIN // IN_TN),
        in_specs=[pl.BlockSpec((IN_TM, D_MODEL), lambda i, j: (i, 0)),
                  pl.BlockSpec((None, 1, D_MODEL), lambda i, j: (i // tiles_per_mod, 0, 0)),
                  pl.BlockSpec((None, 1, D_MODEL), lambda i, j: (i // tiles_per_mod, 0, 0)),
                  pl.BlockSpec((1, D_MODEL), lambda i, j: (0, 0)),
                  pl.BlockSpec((pl.Element(IN_TN), pl.Element(D_MODEL)), first_row),
                  pl.BlockSpec((GATE_W, D_MODEL), lambda i, j: (0, 0))],
        out_specs=[pl.BlockSpec((IN_TM, IN_TN), lambda i, j: (i, j)),
                   pl.BlockSpec((IN_TM, GATE_W), lambda i, j: (i, 0))],
        out_shape=[jax.ShapeDtypeStruct((t, P_MAIN), F32),
                   jax.ShapeDtypeStruct((t, GATE_W), F32)],
        scratch_shapes=[pltpu.VMEM((IN_TM, D_MODEL), BF16)],
        compiler_params=_params(("parallel", "arbitrary"), 60 * 1024 * 1024),
        name="in_proj",
    )(x2d, sc, sh, g, w_t, wg)


def _conv_kernel(x_ref, w_ref, b_ref, o_ref, *, seq, width):
    x = x_ref[...]
    t = lax.broadcasted_iota(jnp.int32, x.shape, 0)
    c = jnp.bitwise_and(t, width - 1)
    xl = jnp.where(c >= 1, pltpu.roll(x, 1, 0), 0.0)
    xr = jnp.where(c <= width - 2, pltpu.roll(x, seq - 1, 0), 0.0)

    def taps(di):
        return w_ref[3 * di:3 * di + 1, :] * xl + w_ref[3 * di + 1:3 * di + 2, :] * x \
            + w_ref[3 * di + 2:3 * di + 3, :] * xr

    out = taps(1) + b_ref[...]
    if seq > width:
        out = out + jnp.where(t >= width, pltpu.roll(taps(0), width, 0), 0.0)
        out = out + jnp.where(t < seq - width, pltpu.roll(taps(2), seq - width, 0), 0.0)
    o_ref[...] = _silu(out)


CONV_BLOCK_ELEMS = 512 * 1024


def _conv(p, conv_w9, conv_b, nb, seq, width, cn):
    cn = min(cn, XBC)
    col0 = P_XBC // cn
    return pl.pallas_call(
        functools.partial(_conv_kernel, seq=seq, width=width),
        grid=(nb, XBC // cn),
        in_specs=[pl.BlockSpec((seq, cn), lambda b, j: (b, col0 + j)),
                  pl.BlockSpec((9, cn), lambda b, j: (0, j)),
                  pl.BlockSpec((1, cn), lambda b, j: (0, j))],
        out_specs=pl.BlockSpec((seq, cn), lambda b, j: (b, j)),
        out_shape=jax.ShapeDtypeStruct((nb * seq, XBC), F32),
        compiler_params=_params(("parallel", "parallel")),
        name="grid_conv",
    )(p, conv_w9, conv_b)


def _tri_masks():
    r = lax.broadcasted_iota(jnp.int32, (CHUNK, CHUNK), 0)
    c = lax.broadcasted_iota(jnp.int32, (CHUNK, CHUNK), 1)
    return r >= c, r <= c


SCAN_CHUNKS = 8


def _scan_kernel(gc_ref, bias_ref, alog_ref, col_ref, bcol_ref, row_ref):
    lower, upper = _tri_masks()
    lo, up = lower.astype(F32), upper.astype(F32)
    hi = lax.Precision.HIGHEST
    lane = lax.broadcasted_iota(jnp.int32, (CHUNK, GATE_W), 1)
    time = lax.broadcasted_iota(jnp.int32, (CHUNK, GATE_W), 0)
    lane1 = lax.broadcasted_iota(jnp.int32, (1, GATE_W), 1)

    def backward(l):
        unit_bwd = (l < G_DT) & (jnp.bitwise_and(l, N_UNITS - 1) >= M_HEADS)
        return unit_bwd | ((l >= G_DT + S_HEADS) & (l < N_GATES))

    is_bwd, is_bwd1 = backward(lane), backward(lane1)
    is_i = lane < G_F
    is_f = (lane >= G_F) & (lane < G_DT)
    is_dt = (lane >= G_DT) & (lane < N_GATES)
    neg_a = -jnp.exp(alog_ref[...])
    for c in range(SCAN_CHUNKS):
        rows = slice(c * CHUNK, (c + 1) * CHUNK)
        g = gc_ref[rows, :] + bias_ref[...]
        soft = jnp.log1p(jnp.exp(-jnp.abs(g)))
        logf = jnp.minimum(g, 0.0) - soft
        dt = jnp.maximum(g, 0.0) + soft
        x = jnp.where(is_f, logf, jnp.where(is_dt, dt * neg_a, 0.0))
        cs = jnp.where(is_bwd, _dot(up, x, precision=hi), _dot(lo, x, precision=hi))
        b_units = pltpu.roll(cs, GATE_W - G_F, 1)
        u = g - b_units
        cu = u
        k = 1
        while k < CHUNK:
            prev = jnp.where(time >= k, pltpu.roll(cu, k, 0), -jnp.inf)
            nxt = jnp.where(time < CHUNK - k, pltpu.roll(cu, CHUNK - k, 0), -jnp.inf)
            cu = jnp.maximum(cu, jnp.where(is_bwd, nxt, prev))
            k *= 2
        total = jnp.where(is_bwd1, cs[0:1, :], cs[CHUNK - 1:CHUNK, :])
        wk = jnp.exp(total - cs) * dt
        q = cs - jnp.log(dt)
        col_ref[rows, :] = jnp.where(is_i, cu, cs)
        bcol_ref[rows, :] = b_units
        tail = jnp.where(lane < G_CU, pltpu.roll(wk, G_WK - G_DT, 1),
                         jnp.where(lane < G_B, pltpu.roll(cu, G_CU, 1), pltpu.roll(b_units, G_B, 1)))
        row_ref[rows, :] = jnp.where(is_i, u, jnp.where(is_dt, q, tail)).T


def _gate_scans(gc, bias, alog):
    t = gc.shape[0]
    tm = SCAN_CHUNKS * CHUNK
    row = lambda i: (i, 0)
    const = lambda i: (0, 0)
    return pl.pallas_call(
        _scan_kernel,
        grid=(t // tm,),
        in_specs=[pl.BlockSpec((tm, GATE_W), row), pl.BlockSpec((1, GATE_W), const),
                  pl.BlockSpec((1, GATE_W), const)],
        out_specs=[pl.BlockSpec((tm, GATE_W), row)] * 3,
        out_shape=[jax.ShapeDtypeStruct((t, GATE_W), F32)] * 3,
        compiler_params=_params(("parallel",)),
        name="gate_scans",
    )(gc, bias, alog)


N_ROWS = 16
_WORK, _DONE = "work", "done"


def _mlstm_phases(refs, nc, zero_init):
    (qf, kf, vf, of, colf, bcolf, rowf, qb, kb, vb, ob, colb, bcolb, rowb, gn) = refs[:15]
    pos = 15
    if not zero_init:
        c0, n0, m0 = refs[pos:pos + 3]
        pos += 3
    hm_out, c_out, n_out, m_out, c_s, n_s, m_s, hpart, hbuf = refs[pos:]
    s = pl.program_id(1)
    half = nc // 2

    @pl.when(s == 0)
    def _():
        hpart[...] = jnp.zeros_like(hpart)
        if zero_init:
            c_s[...] = jnp.zeros_like(c_s)
            n_s[...] = jnp.zeros_like(n_s)
            m_s[...] = jnp.zeros_like(m_s)
        else:
            c_s[...] = c0[...]
            m_s[...] = m0[...]
            for u in range(N_UNITS):
                n_s[u] = jnp.broadcast_to(n0[u:u + 1, :], (N_ROWS, M_DQK))

    yield _DONE
    lower, upper = _tri_masks()
    lane1 = lax.broadcasted_iota(jnp.int32, (1, LANES), 1)
    ones_rows = jnp.ones((N_ROWS, CHUNK), BF16)
    per_dir = ((qf, kf, vf, of, colf, bcolf, rowf, lower, s),
               (qb, kb, vb, ob, colb, bcolb, rowb, upper, nc - 1 - s))
    m_prev = m_s[...]
    m_next = m_prev
    c_new, n_new = [], []
    for d, (q_ref, k_ref, v_ref, o_ref, col_ref, bcol_ref, row_ref, mask, chunk) in enumerate(per_dir):
        last = CHUNK - 1 if d == 0 else 0
        stab = jnp.maximum(m_prev, col_ref[...])
        w_c = jnp.exp(m_prev - stab)
        stab_last = stab[last:last + 1, :]
        m_new = bcol_ref[last:last + 1, :] + stab_last
        decay = w_c[last:last + 1, :]
        c_old = [c_s[d, h] for h in range(M_HEADS)]
        n_old = [n_s[d * M_HEADS + h] for h in range(M_HEADS)]
        for h in range(M_HEADS):
            u = d * M_HEADS + h
            stab_b = _lane_bcast(stab, u)
            u_row = row_ref[u:u + 1, :]
            p = jnp.exp(jnp.where(mask, u_row - stab_b, -jnp.inf))
            qh = q_ref[:, h * M_DQK:(h + 1) * M_DQK] * (M_DQK ** -0.5)
            k_t = k_ref[:, h * M_DQK:(h + 1) * M_DQK].T
            v16 = v_ref[:, h * M_DV:(h + 1) * M_DV].astype(BF16)
            sc = _dot(qh.astype(BF16), k_t.astype(BF16)) * p
            lhs = jnp.concatenate([sc.astype(BF16), (qh * _lane_bcast(w_c, u)).astype(BF16)], axis=1)
            num = _dot(lhs, jnp.concatenate([v16, c_old[h].astype(BF16)], axis=0))
            den = _dot(jnp.concatenate([ones_rows, n_old[h].astype(BF16)], axis=1), lhs, _NT)[0:1, :]
            m_t = row_ref[G_B + u:G_B + u + 1, :] + jnp.maximum(m_prev[:, u:u + 1],
                                                                row_ref[G_CU + u:G_CU + u + 1, :])
            inv_row = 1.0 / jnp.maximum(jnp.abs(den), jnp.exp(-m_t))
            inv = jnp.broadcast_to(inv_row, (CHUNK, LANES)).T
            hbuf[d, :, h * M_DV:(h + 1) * M_DV] = num * jnp.concatenate([inv, inv], axis=1)

            kw_t = (k_t * jnp.exp(u_row - stab_last[:, u:u + 1])).astype(BF16)
            dec = decay[:, u:u + 1]
            c_new.append(dec * c_old[h] + _dot(kw_t, v16))
            n_new.append(dec * n_old[h] + _dot(ones_rows, kw_t, _NT))
            yield _WORK
        mine = (lane1 >= d * M_HEADS) & (lane1 < (d + 1) * M_HEADS)
        m_next = jnp.where(mine, m_new, m_next)
    for u in range(N_UNITS):
        c_s[u // M_HEADS, u % M_HEADS] = c_new[u]
        n_s[u] = n_new[u]
    m_s[...] = m_next

    yield _DONE
    for d, (_, _, _, o_ref, _, _, _, _, chunk) in enumerate(per_dir):
        rows = pl.ds(pl.multiple_of(chunk * CHUNK, CHUNK), CHUNK)
        for h in range(M_HEADS):
            cols = slice(h * M_DV, (h + 1) * M_DV)
            tot = hbuf[d, :, cols] + hpart[rows, cols]
            hm_out[rows, cols] = (_rms(tot) * gn[:, cols] * o_ref[:, cols]).astype(BF16)
        hpart[rows, :] = hbuf[d]

    @pl.when(s == nc - 1)
    def _():
        c_out[...] = c_s[...]
        m_out[...] = m_s[...]
        for u in range(N_UNITS):
            n_out[u:u + 1, :] = n_s[u][0:1, :]

    yield _DONE


def _sequence_out_spec(seq, width):
    mode = dict(pipeline_mode=pl.Buffered(1)) if seq * width * 2 >= 2 * 1024 * 1024 else {}
    return pl.BlockSpec((None, seq, width), lambda b, s: (b, 0, 0), **mode)


def _mlstm_specs(p, scans, g_norm, state, nb, nc):
    zero_init = state is None
    seq = nc * CHUNK
    col, bcol, row = scans

    def fwd(cb):
        return lambda b, s: (b * nc + s, cb)

    def bwd(cb):
        return lambda b, s: (b * nc + nc - 1 - s, cb)

    def stream(mk):
        return [pl.BlockSpec((CHUNK, M_HEADS * M_DQK), mk(P_Q // (M_HEADS * M_DQK))),
                pl.BlockSpec((CHUNK, M_HEADS * M_DQK), mk(P_K // (M_HEADS * M_DQK))),
                pl.BlockSpec((CHUNK, MIX_M), mk(P_V // MIX_M)),
                pl.BlockSpec((CHUNK, MIX_M), mk(P_O // MIX_M)),
                pl.BlockSpec((CHUNK, GATE_W), mk(0)),
                pl.BlockSpec((CHUNK, GATE_W), mk(0)),
                pl.BlockSpec((CHUNK, GATE_W), mk(0))]

    in_specs = stream(fwd) + stream(bwd) + [pl.BlockSpec((1, MIX_M), lambda b, s: (0, 0))]
    args = [p] * 4 + [col, bcol, row] + [p] * 4 + [col, bcol, row, g_norm]
    state_specs = [pl.BlockSpec((None, N_DIR, M_HEADS, M_DQK, M_DV), lambda b, s: (b, 0, 0, 0, 0)),
                   pl.BlockSpec((None, N_UNITS, M_DQK), lambda b, s: (b, 0, 0)),
                   pl.BlockSpec((None, 1, LANES), lambda b, s: (b, 0, 0))]
    if not zero_init:
        in_specs += state_specs
        args += list(state)
    out_specs = [_sequence_out_spec(seq, MIX_M)] + state_specs
    out_shape = [jax.ShapeDtypeStruct((nb, seq, MIX_M), BF16),
                 jax.ShapeDtypeStruct((nb, N_DIR, M_HEADS, M_DQK, M_DV), F32),
                 jax.ShapeDtypeStruct((nb, N_UNITS, M_DQK), F32),
                 jax.ShapeDtypeStruct((nb, 1, LANES), F32)]
    scratch = [pltpu.VMEM((N_DIR, M_HEADS, M_DQK, M_DV), F32),
               pltpu.VMEM((N_UNITS, N_ROWS, M_DQK), F32),
               pltpu.VMEM((1, LANES), F32),
               pltpu.VMEM((seq, MIX_M), F32),
               pltpu.VMEM((N_DIR, CHUNK, MIX_M), F32)]
    return in_specs, args, out_specs, out_shape, scratch


def _ssd_phases(refs, nc, zero_init):
    (xf, bf, cf, zf, colf, rowf, xb, bb, cb, zb, colb, rowb, dskip, gn) = refs[:14]
    pos = 14
    if not zero_init:
        s0 = refs[pos]
        pos += 1
    ys_out, s_out, st_s, ypart, ybuf = refs[pos:]
    s = pl.program_id(1)
    half = nc // 2

    @pl.when(s == 0)
    def _():
        ypart[...] = jnp.zeros_like(ypart)
        if zero_init:
            st_s[...] = jnp.zeros_like(st_s)
        else:
            for d in range(N_DIR):
                for pr in range(S_PAIRS):
                    st_s[d, pr] = s0[d, pr].T

    yield _DONE
    lower, upper = _tri_masks()
    low_half = lax.broadcasted_iota(jnp.int32, (CHUNK, LANES), 1) < S_HEADDIM
    low_half1 = lax.broadcasted_iota(jnp.int32, (1, LANES), 1) < S_HEADDIM
    per_dir = ((xf, bf, cf, zf, colf, rowf, lower, s), (xb, bb, cb, zb, colb, rowb, upper, nc - 1 - s))
    st_new = []
    for d, (x_ref, b_ref, c_ref, z_ref, col_ref, row_ref, mask, chunk) in enumerate(per_dir):
        last = CHUNK - 1 if d == 0 else 0
        cs_c = col_ref[...]
        total = cs_c[last:last + 1, :]
        st_old = [st_s[d, pr] for pr in range(S_PAIRS)]
        for g in range(S_GROUPS):
            gs = slice(g * S_STATE, (g + 1) * S_STATE)
            cg = c_ref[:, gs].astype(BF16)
            b_f32 = b_ref[:, gs]
            cbm = _dot(cg, b_f32.astype(BF16), _NT)
            b_t = b_f32.T
            for pg in range(PAIRS_PER_GROUP):
                pr = g * PAIRS_PER_GROUP + pg
                cols = slice(pr * LANES, (pr + 1) * LANES)
                x16 = x_ref[:, cols].astype(BF16)
                zero16 = jnp.zeros_like(x16)
                halves = (jnp.where(low_half, x16, zero16), jnp.where(low_half, zero16, x16))
                state = st_old[pr]
                y_acc = jnp.zeros((CHUNK, LANES), F32)
                s_acc = jnp.zeros((S_STATE, LANES), F32)
                cs_b = []
                for e in range(PAIR):
                    gi = G_DT + d * S_HEADS + pr * PAIR + e
                    cs_b.append(_lane_bcast(cs_c, gi))
                    q_row = row_ref[gi:gi + 1, :]
                    wk_row = row_ref[gi + G_WK - G_DT:gi + G_WK - G_DT + 1, :]
                    mix = cbm * jnp.exp(jnp.where(mask, cs_b[e] - q_row, -jnp.inf))
                    y_acc = y_acc + _dot(mix.astype(BF16), halves[e])
                    s_acc = s_acc + _dot((b_t * wk_row).astype(BF16), halves[e])
                carry = jnp.exp(jnp.where(low_half, cs_b[0], cs_b[1]))
                ybuf[d, :, cols] = y_acc + _dot(cg, state.astype(BF16)) * carry
                gi0 = G_DT + d * S_HEADS + pr * PAIR
                dec = jnp.exp(jnp.where(low_half1, jnp.broadcast_to(total[:, gi0:gi0 + 1], (1, LANES)),
                                        jnp.broadcast_to(total[:, gi0 + 1:gi0 + 2], (1, LANES))))
                st_new.append(state * dec + s_acc)
            yield _WORK
    for d in range(N_DIR):
        for pr in range(S_PAIRS):
            st_s[d, pr] = st_new[d * S_PAIRS + pr]

    yield _DONE
    for d, (x_ref, _, _, z_ref, _, _, _, chunk) in enumerate(per_dir):
        rows = pl.ds(pl.multiple_of(chunk * CHUNK, CHUNK), CHUNK)
        y = ybuf[d] + ypart[rows, :] + dskip[...] * x_ref[...]
        ys_out[rows, :] = (_rms(y * z_ref[...]) * gn[...]).astype(BF16)
        ypart[rows, :] = ybuf[d]

    @pl.when(s == nc - 1)
    def _():
        for d in range(N_DIR):
            for pr in range(S_PAIRS):
                s_out[d, pr] = st_s[d, pr].T

    yield _DONE


def _ssd_specs(p, xbc, scans, dskip, g_norm, state, nb, nc):
    zero_init = state is None
    seq = nc * CHUNK
    bc_w = S_GROUPS * S_STATE
    col, _, row = scans

    def fwd(cb):
        return lambda b, s: (b * nc + s, cb)

    def bwd(cb):
        return lambda b, s: (b * nc + nc - 1 - s, cb)

    def stream(mk):
        return [pl.BlockSpec((CHUNK, MIX_S), mk(0)),
                pl.BlockSpec((CHUNK, bc_w), mk(MIX_S // bc_w)),
                pl.BlockSpec((CHUNK, bc_w), mk(MIX_S // bc_w + 1)),
                pl.BlockSpec((CHUNK, MIX_S), mk(P_Z // MIX_S)),
                pl.BlockSpec((CHUNK, GATE_W), mk(0)),
                pl.BlockSpec((CHUNK, GATE_W), mk(0))]

    const = lambda b, s: (0, 0)
    in_specs = stream(fwd) + stream(bwd) + [pl.BlockSpec((1, MIX_S), const), pl.BlockSpec((1, MIX_S), const)]
    args = [xbc, xbc, xbc, p, col, row] * 2 + [dskip, g_norm]
    state_spec = pl.BlockSpec((None, N_DIR, S_PAIRS, LANES, S_STATE), lambda b, s: (b, 0, 0, 0, 0))
    if not zero_init:
        in_specs.append(state_spec)
        args.append(state)
    out_specs = [_sequence_out_spec(seq, MIX_S), state_spec]
    out_shape = [jax.ShapeDtypeStruct((nb, seq, MIX_S), BF16),
                 jax.ShapeDtypeStruct((nb, N_DIR, S_PAIRS, LANES, S_STATE), F32)]
    scratch = [pltpu.VMEM((N_DIR, S_PAIRS, S_STATE, LANES), F32),
               pltpu.VMEM((seq, MIX_S), F32),
               pltpu.VMEM((N_DIR, CHUNK, MIX_S), F32)]
    return in_specs, args, out_specs, out_shape, scratch


def _mixer_kernel(*refs, nc, zero_init, counts):
    (m_in, s_in), (m_out, s_out), (m_scr, s_scr) = counts
    pos = 0
    parts = []
    for n in (m_in, s_in, m_out, s_out, m_scr, s_scr):
        parts.append(refs[pos:pos + n])
        pos += n
    gens = (_mlstm_phases(parts[0] + parts[2] + parts[4], nc, zero_init),
            _ssd_phases(parts[1] + parts[3] + parts[5], nc, zero_init))
    for _ in range(3):
        live = list(gens)
        while live:
            live = [g for g in live if next(g) is not _DONE]


def _mixers(p, xbc, scans, g_mlstm, dskip, g_ssd, mlstm_state, ssd_state, nb, nc):
    m = _mlstm_specs(p, scans, g_mlstm, mlstm_state, nb, nc)
    s = _ssd_specs(p, xbc, scans, dskip, g_ssd, ssd_state, nb, nc)
    counts = tuple((len(a), len(b)) for a, b in ((m[0], s[0]), (m[2], s[2]), (m[4], s[4])))
    return pl.pallas_call(
        functools.partial(_mixer_kernel, nc=nc, zero_init=mlstm_state is None, counts=counts),
        grid=(nb, nc),
        in_specs=m[0] + s[0],
        out_specs=m[2] + s[2],
        out_shape=m[3] + s[3],
        scratch_shapes=m[4] + s[4],
        compiler_params=_params(("parallel", "arbitrary"), 60 * 1024 * 1024),
        name="mixers",
    )(*(m[1] + s[1]))


OUT_TM = 512


OUT_SUB = 256


def _outproj_kernel(hm_ref, ys_ref, w_ref, x_ref, g1_ref, gpost_ref, gpre_ref, sc_ref, sh_ref, x1_ref, u2_ref):
    for r in range(OUT_TM // OUT_SUB):
        rows = slice(r * OUT_SUB, (r + 1) * OUT_SUB)
        mix = _dot(hm_ref[rows, :], w_ref[0:MIX_M, :]) + _dot(ys_ref[rows, :], w_ref[MIX_M:, :])
        x1 = x_ref[rows, :] + g1_ref[...] * (_rms(mix) * gpost_ref[...])
        x1_ref[rows, :] = x1
        u2_ref[rows, :] = (_rms(x1) * gpre_ref[...] * (1.0 + sc_ref[...]) + sh_ref[...]).astype(BF16)


def _out_proj(hm, ys, w_out16, x2d, g1, gpost, gpre, sc2, sh2, rows_per_mod):
    t = x2d.shape[0]
    tiles_per_mod = rows_per_mod // OUT_TM
    row = lambda i: (i, 0)
    const = lambda i: (0, 0)
    mod = pl.BlockSpec((None, 1, D_MODEL), lambda i: (i // tiles_per_mod, 0, 0))
    return pl.pallas_call(
        _outproj_kernel,
        grid=(t // OUT_TM,),
        in_specs=[pl.BlockSpec((OUT_TM, MIX_M), row), pl.BlockSpec((OUT_TM, MIX_S), row),
                  pl.BlockSpec((D_MODEL, D_MODEL), const), pl.BlockSpec((OUT_TM, D_MODEL), row),
                  mod, pl.BlockSpec((1, D_MODEL), const), pl.BlockSpec((1, D_MODEL), const), mod, mod],
        out_specs=[pl.BlockSpec((OUT_TM, D_MODEL), row), pl.BlockSpec((OUT_TM, D_MODEL), row)],
        out_shape=[jax.ShapeDtypeStruct((t, D_MODEL), F32), jax.ShapeDtypeStruct((t, D_MODEL), BF16)],
        compiler_params=_params(("parallel",)),
        name="out_proj",
    )(hm, ys, w_out16, x2d, g1, gpost, gpre, sc2, sh2)


MLP_TM = 1024
MLP_TH = 512
MLP_SUB = 512


def _mlp_kernel(u_ref, w1_ref, w2_ref, x1_hbm, g2_ref, gpost_ref, o_ref, x1_buf, x1_sem):
    i = pl.program_id(0)
    j = pl.program_id(1)

    def x1_copy():
        rows = pl.ds(pl.multiple_of(i * MLP_TM, MLP_TM), MLP_TM)
        return pltpu.make_async_copy(x1_hbm.at[rows, :], x1_buf, x1_sem)

    @pl.when(j == 0)
    def _():
        x1_copy().start()
        o_ref[...] = jnp.zeros_like(o_ref)

    w1 = w1_ref[...].astype(BF16)
    w2 = w2_ref[...].astype(BF16)
    for r in range(MLP_TM // MLP_SUB):
        rows = slice(r * MLP_SUB, (r + 1) * MLP_SUB)
        hid = jnp.square(jnp.maximum(_dot(u_ref[rows, :], w1), 0.0)).astype(BF16)
        o_ref[rows, :] += _dot(hid, w2)

    @pl.when(j == pl.num_programs(1) - 1)
    def _():
        x1_copy().wait()

        def fin(r, carry):
            rows = pl.ds(pl.multiple_of(r * MLP_SUB, MLP_SUB), MLP_SUB)
            o_ref[rows, :] = x1_buf[rows, :] + g2_ref[...] * (_rms(o_ref[rows, :]) * gpost_ref[...])
            return carry
        lax.fori_loop(0, MLP_TM // MLP_SUB, fin, 0)


def _mlp(u2, w1, w2, x1, g2, gpost, rows_per_mod):
    t = u2.shape[0]
    tiles_per_mod = rows_per_mod // MLP_TM
    return pl.pallas_call(
        _mlp_kernel,
        grid=(t // MLP_TM, D_FF // MLP_TH),
        in_specs=[pl.BlockSpec((MLP_TM, D_MODEL), lambda i, j: (i, 0)),
                  pl.BlockSpec((D_MODEL, MLP_TH), lambda i, j: (0, j)),
                  pl.BlockSpec((MLP_TH, D_MODEL), lambda i, j: (j, 0)),
                  pl.BlockSpec(memory_space=pl.ANY),
                  pl.BlockSpec((None, 1, D_MODEL), lambda i, j: (i // tiles_per_mod, 0, 0)),
                  pl.BlockSpec((1, D_MODEL), lambda i, j: (0, 0))],
        out_specs=pl.BlockSpec((MLP_TM, D_MODEL), lambda i, j: (i, 0)),
        out_shape=jax.ShapeDtypeStruct((t, D_MODEL), F32),
        scratch_shapes=[pltpu.VMEM((MLP_TM, D_MODEL), F32), pltpu.SemaphoreType.DMA(())],
        compiler_params=_params(("parallel", "arbitrary"), 60 * 1024 * 1024),
        name="mlp",
    )(u2, w1, w2, x1, g2, gpost)


def _gate_row(i_vals, f_vals, dt_vals):
    v = jnp.concatenate([i_vals.reshape(-1), f_vals.reshape(-1), dt_vals.reshape(-1)]).astype(F32)
    return jnp.pad(v, (0, GATE_W - N_GATES)).reshape(1, GATE_W)


def _block(x, mods, state, weights, width):
    nb, seq, _ = x.shape
    nc = seq // CHUNK
    t = nb * seq
    x2d = x.reshape(t, D_MODEL)
    sh1, sc1, g1, sh2, sc2, g2 = mods
    rows_per_mod = t // sh1.shape[0]
    w = weights

    p, gc = _in_proj(x2d, sc1, sh1, w["g_pre_mix"], w["w_t"], w["wg"], rows_per_mod)
    scans = _gate_scans(gc, w["gate_bias"], w["gate_alog"])
    xbc = _conv(p, w["conv_w9"], w["conv_b"], nb, seq, width, CONV_BLOCK_ELEMS // seq)

    if state is None:
        m_state = s_state = None
    else:
        c0, n0, m0, s0 = state
        m_state = (c0, n0.reshape(nb, N_UNITS, M_DQK),
                   jnp.pad(m0.reshape(nb, 1, N_UNITS), ((0, 0), (0, 0), (0, LANES - N_UNITS))))
        s_state = s0.reshape(nb, N_DIR, S_PAIRS, LANES, S_STATE)
    hm, c_new, n_new, m_new, ys, s_new = _mixers(p, xbc, scans, w["g_mlstm_norm"], w["dskip"],
                                                 w["g_ssd_norm"], m_state, s_state, nb, nc)

    x1, u2 = _out_proj(hm.reshape(t, MIX_M), ys.reshape(t, MIX_S), w["w_out"], x2d, g1,
                       w["g_post_mix"], w["g_pre_mlp"], sc2, sh2, rows_per_mod)
    y = _mlp(u2, w["w_mlp_in"], w["w_mlp_out"], x1, g2, w["g_post_mlp"], rows_per_mod)
    new_state = (c_new.reshape(nb, 1, N_DIR, M_HEADS, M_DQK, M_DV),
                 n_new.reshape(nb, 1, N_DIR, M_HEADS, M_DQK),
                 m_new[:, 0, :N_UNITS].reshape(nb, 1, N_DIR, M_HEADS),
                 s_new.reshape(nb, 1, N_DIR, S_HEADS, S_HEADDIM, S_STATE))
    return y.reshape(nb, seq, D_MODEL), new_state


def kernel(x_prompt, x_sample, state_mlstm_c, state_mlstm_n, state_mlstm_m, state_ssd, c, c_ctx, w_mod, b_mod,
           g_pre_mix, g_post_mix, w_in, b_igate, b_fgate, conv_w, conv_b, dt_bias, a_log, d_skip, g_mlstm_norm,
           g_ssd_norm, w_out, g_pre_mlp, g_post_mlp, w_mlp_in, w_mlp_out):
    assert w_mod.shape[0] == 1, "one layer"
    nb_s = x_sample.shape[0]

    cond8 = jnp.zeros((8, D_MODEL), F32).at[0].set(c_ctx).at[1:1 + nb_s].set(c)
    mod = _modulation(cond8, w_mod[0], b_mod[0].reshape(1, -1))
    mods = [mod[:, k * D_MODEL:(k + 1) * D_MODEL] for k in range(6)]
    mods_p = [m[0:1].reshape(1, 1, D_MODEL) for m in mods]
    mods_s = [m[1:1 + nb_s].reshape(nb_s, 1, D_MODEL) for m in mods]

    w_t = w_in[0].T.astype(BF16)
    assert w_t.shape[0] == P_MAIN + N_GATES
    gate_rows = jnp.concatenate([w_t[P_Z:P_Z + G_DT], w_t[P_MAIN + G_DT:]], axis=0)
    gate_rows = jnp.pad(gate_rows, ((0, GATE_W - N_GATES), (0, 0)))
    zeros_u = jnp.zeros((N_UNITS,), F32)
    row = lambda v: v.reshape(1, -1)
    weights = dict(
        w_t=w_t, wg=gate_rows,
        g_pre_mix=row(g_pre_mix[0]), g_post_mix=row(g_post_mix[0]),
        g_pre_mlp=row(g_pre_mlp[0]), g_post_mlp=row(g_post_mlp[0]),
        conv_w9=conv_w[0].reshape(9, XBC), conv_b=row(conv_b[0]),
        gate_bias=_gate_row(b_igate[0], b_fgate[0], dt_bias[0]),
        gate_alog=_gate_row(zeros_u, zeros_u, a_log[0]),
        dskip=row(jnp.repeat(d_skip[0], S_HEADDIM)),
        g_mlstm_norm=row(g_mlstm_norm[0]), g_ssd_norm=row(g_ssd_norm[0]),
        w_out=w_out[0].astype(BF16), w_mlp_in=w_mlp_in[0], w_mlp_out=w_mlp_out[0])

    y_p, st = _block(x_prompt, mods_p, None, weights, x_prompt.shape[1])
    cache = (state_mlstm_c[:, 0], state_mlstm_n[:, 0], state_mlstm_m[:, 0], state_ssd[:, 0])
    y_s, _ = _block(x_sample, mods_s, cache, weights, GRID_W)
    return (y_p, y_s) + st
```

```python
import functools

import jax
import jax.numpy as jnp
from jax import lax
from jax.experimental import pallas as pl
from jax.experimental.pallas import tpu as pltpu

F32 = jnp.float32
BF16 = jnp.bfloat16

D_MODEL = 2048
CHUNK = 128
N_DIR = 2
M_HEADS = 4
M_DQK = 128
M_DV = 256
MIX_M = M_HEADS * M_DV
S_HEADS = 16
S_HEADDIM = 64
S_STATE = 128
S_GROUPS = 4
S_REP = S_HEADS // S_GROUPS
MIX_S = S_HEADS * S_HEADDIM
XBC = MIX_S + 2 * S_GROUPS * S_STATE
D_FF = 4 * D_MODEL
GRID_W = 64
EPS = 1e-6
LANES = 128

P_Q = 0
P_K = M_HEADS * M_DQK
P_V = 2 * M_HEADS * M_DQK
P_O = P_V + MIX_M
P_Z = P_O + MIX_M
P_XBC = P_Z + MIX_S
P_MAIN = P_XBC + XBC
GATE_W = LANES
N_UNITS = N_DIR * M_HEADS
G_I = 0
G_F = N_UNITS
G_DT = 2 * N_UNITS
N_GATES = G_DT + N_DIR * S_HEADS
G_WK = G_DT + N_DIR * S_HEADS
G_CU = G_WK + N_DIR * S_HEADS
G_B = G_CU + N_UNITS
assert G_B + N_UNITS <= GATE_W

PAIR = LANES // S_HEADDIM
S_PAIRS = S_HEADS // PAIR
PAIRS_PER_GROUP = S_REP // PAIR

VMEM_LIMIT = 48 * 1024 * 1024

_NT = (((1,), (1,)), ((), ()))


def _params(sem, limit=VMEM_LIMIT):
    return pltpu.CompilerParams(dimension_semantics=sem, vmem_limit_bytes=limit)


def _silu(x):
    return x / (1.0 + jnp.exp(-x))


def _sigmoid(x):
    return 1.0 / (1.0 + jnp.exp(-x))


def _rms(x):
    return x * lax.rsqrt(jnp.mean(x * x, axis=-1, keepdims=True) + EPS)


def _dot(a, b, dims=None, precision=None):
    if dims is None:
        dims = (((a.ndim - 1,), (0,)), ((), ()))
    return lax.dot_general(a, b, dims, precision=precision, preferred_element_type=F32)


def _lane_bcast(tile, lane):
    return jnp.broadcast_to(tile[:, lane:lane + 1], tile.shape)


def _mod_kernel(c_ref, w_ref, b_ref, o_ref):
    a = _silu(c_ref[...]).astype(BF16)
    o_ref[...] = _dot(a, w_ref[...].astype(BF16)) + b_ref[...]


def _modulation(cond8, w_mod, b_mod):
    n = w_mod.shape[1]
    tn = 1024
    return pl.pallas_call(
        _mod_kernel,
        grid=(n // tn,),
        in_specs=[pl.BlockSpec((8, D_MODEL), lambda j: (0, 0)),
                  pl.BlockSpec((D_MODEL, tn), lambda j: (0, j)),
                  pl.BlockSpec((1, tn), lambda j: (0, j))],
        out_specs=pl.BlockSpec((8, tn), lambda j: (0, j)),
        out_shape=jax.ShapeDtypeStruct((8, n), F32),
        compiler_params=_params(("parallel",)),
        name="modulation",
    )(cond8, w_mod, b_mod)


IN_TM = 1024
IN_TN = 1024
IN_SUB = 256


def _inproj_kernel(x_ref, sc_ref, sh_ref, g_ref, w_ref, wg_ref, o_ref, gc_ref, u_ref):
    j = pl.program_id(1)

    j_o, j_z = P_O // IN_TN, P_Z // IN_TN

    @pl.when(j == 0)
    def _():
        for r in range(IN_TM // IN_SUB):
            rows = slice(r * IN_SUB, (r + 1) * IN_SUB)
            y = _rms(x_ref[rows, :]) * g_ref[...]
            u = (y * (1.0 + sc_ref[...]) + sh_ref[...]).astype(BF16)
            u_ref[rows, :] = u
            o_ref[rows, :] = _dot(u, w_ref[...], _NT)

    @pl.when((j > 0) & (j != j_o) & (j != j_z))
    def _():
        o_ref[...] = _dot(u_ref[...], w_ref[...], _NT)

    @pl.when((j == j_o) | (j == j_z))
    def _():
        acc = _dot(u_ref[...], w_ref[...], _NT)
        sig = _sigmoid(acc)
        o_ref[...] = jnp.where(j == j_o, sig, acc * sig)

    @pl.when(j == pl.num_programs(1) - 1)
    def _():
        gc_ref[...] = _dot(u_ref[...], wg_ref[...], _NT)


def _in_proj(x2d, sc, sh, g, w_t, wg, rows_per_mod):
    t = x2d.shape[0]
    tiles_per_mod = rows_per_mod // IN_TM
    n_a = P_Z // IN_TN
    assert P_O % IN_TN == 0 and MIX_M == IN_TN and MIX_S == IN_TN

    def first_row(i, j):
        return (j * (IN_TN // G_DT) + (j >= n_a).astype(jnp.int32)) * G_DT, 0
    return pl.pallas_call(
        _inproj_kernel,
        grid=(t // IN_TM, P_MAIN // IN_TN),
        in_specs=[pl.BlockSpec((IN_TM, D_MODEL), lambda i, j: (i, 0)),
                  pl.BlockSpec((None, 1, D_MODEL), lambda i, j: (i // tiles_per_mod, 0, 0)),
                  pl.BlockSpec((None, 1, D_MODEL), lambda i, j: (i // tiles_per_mod, 0, 0)),
                  pl.BlockSpec((1, D_MODEL), lambda i, j: (0, 0)),
                  pl.BlockSpec((pl.Element(IN_TN), pl.Element(D_MODEL)), first_row),
                  pl.BlockSpec((GATE_W, D_MODEL), lambda i, j: (0, 0))],
        out_specs=[pl.BlockSpec((IN_TM, IN_TN), lambda i, j: (i, j)),
                   pl.BlockSpec((IN_TM, GATE_W), lambda i, j: (i, 0))],
        out_shape=[jax.ShapeDtypeStruct((t, P_MAIN), F32),
                   jax.ShapeDtypeStruct((t, GATE_W), F32)],
        scratch_shapes=[pltpu.VMEM((IN_TM, D_MODEL), BF16)],
        compiler_params=_params(("parallel", "arbitrary"), 60 * 1024 * 1024),
        name="in_proj",
    )(x2d, sc, sh, g, w_t, wg)


def _conv_kernel(x_ref, w_ref, b_ref, o_ref, *, seq, width):
    x = x_ref[...]
    t = lax.broadcasted_iota(jnp.int32, x.shape, 0)
    c = jnp.bitwise_and(t, width - 1)
    xl = jnp.where(c >= 1, pltpu.roll(x, 1, 0), 0.0)
    xr = jnp.where(c <= width - 2, pltpu.roll(x, seq - 1, 0), 0.0)

    def taps(di):
        return w_ref[3 * di:3 * di + 1, :] * xl + w_ref[3 * di + 1:3 * di + 2, :] * x \
            + w_ref[3 * di + 2:3 * di + 3, :] * xr

    out = taps(1) + b_ref[...]
    if seq > width:
        out = out + jnp.where(t >= width, pltpu.roll(taps(0), width, 0), 0.0)
        out = out + jnp.where(t < seq - width, pltpu.roll(taps(2), seq - width, 0), 0.0)
    o_ref[...] = _silu(out)


CONV_BLOCK_ELEMS = 512 * 1024


def _conv(p, conv_w9, conv_b, nb, seq, width, cn):
    cn = min(cn, XBC)
    col0 = P_XBC // cn
    return pl.pallas_call(
        functools.partial(_conv_kernel, seq=seq, width=width),
        grid=(nb, XBC // cn),
        in_specs=[pl.BlockSpec((seq, cn), lambda b, j: (b, col0 + j)),
                  pl.BlockSpec((9, cn), lambda b, j: (0, j)),
                  pl.BlockSpec((1, cn), lambda b, j: (0, j))],
        out_specs=pl.BlockSpec((seq, cn), lambda b, j: (b, j)),
        out_shape=jax.ShapeDtypeStruct((nb * seq, XBC), F32),
        compiler_params=_params(("parallel", "parallel")),
        name="grid_conv",
    )(p, conv_w9, conv_b)


def _tri_masks():
    r = lax.broadcasted_iota(jnp.int32, (CHUNK, CHUNK), 0)
    c = lax.broadcasted_iota(jnp.int32, (CHUNK, CHUNK), 1)
    return r >= c, r <= c


SCAN_CHUNKS = 8


def _scan_kernel(gc_ref, bias_ref, alog_ref, col_ref, bcol_ref, row_ref):
    lower, upper = _tri_masks()
    lo, up = lower.astype(F32), upper.astype(F32)
    hi = lax.Precision.HIGHEST
    lane = lax.broadcasted_iota(jnp.int32, (CHUNK, GATE_W), 1)
    time = lax.broadcasted_iota(jnp.int32, (CHUNK, GATE_W), 0)
    lane1 = lax.broadcasted_iota(jnp.int32, (1, GATE_W), 1)

    def backward(l):
        unit_bwd = (l < G_DT) & (jnp.bitwise_and(l, N_UNITS - 1) >= M_HEADS)
        return unit_bwd | ((l >= G_DT + S_HEADS) & (l < N_GATES))

    is_bwd, is_bwd1 = backward(lane), backward(lane1)
    is_i = lane < G_F
    is_f = (lane >= G_F) & (lane < G_DT)
    is_dt = (lane >= G_DT) & (lane < N_GATES)
    neg_a = -jnp.exp(alog_ref[...])
    for c in range(SCAN_CHUNKS):
        rows = slice(c * CHUNK, (c + 1) * CHUNK)
        g = gc_ref[rows, :] + bias_ref[...]
        soft = jnp.log1p(jnp.exp(-jnp.abs(g)))
        logf = jnp.minimum(g, 0.0) - soft
        dt = jnp.maximum(g, 0.0) + soft
        x = jnp.where(is_f, logf, jnp.where(is_dt, dt * neg_a, 0.0))
        cs = jnp.where(is_bwd, _dot(up, x, precision=hi), _dot(lo, x, precision=hi))
        b_units = pltpu.roll(cs, GATE_W - G_F, 1)
        u = g - b_units
        cu = u
        k = 1
        while k < CHUNK:
            prev = jnp.where(time >= k, pltpu.roll(cu, k, 0), -jnp.inf)
            nxt = jnp.where(time < CHUNK - k, pltpu.roll(cu, CHUNK - k, 0), -jnp.inf)
            cu = jnp.maximum(cu, jnp.where(is_bwd, nxt, prev))
            k *= 2
        total = jnp.where(is_bwd1, cs[0:1, :], cs[CHUNK - 1:CHUNK, :])
        wk = jnp.exp(total - cs) * dt
        q = cs - jnp.log(dt)
        col_ref[rows, :] = jnp.where(is_i, cu, cs)
        bcol_ref[rows, :] = b_units
        tail = jnp.where(lane < G_CU, pltpu.roll(wk, G_WK - G_DT, 1),
                         jnp.where(lane < G_B, pltpu.roll(cu, G_CU, 1), pltpu.roll(b_units, G_B, 1)))
        row_ref[rows, :] = jnp.where(is_i, u, jnp.where(is_dt, q, tail)).T


def _gate_scans(gc, bias, alog):
    t = gc.shape[0]
    tm = SCAN_CHUNKS * CHUNK
    row = lambda i: (i, 0)
    const = lambda i: (0, 0)
    return pl.pallas_call(
        _scan_kernel,
        grid=(t // tm,),
        in_specs=[pl.BlockSpec((tm, GATE_W), row), pl.BlockSpec((1, GATE_W), const),
                  pl.BlockSpec((1, GATE_W), const)],
        out_specs=[pl.BlockSpec((tm, GATE_W), row)] * 3,
        out_shape=[jax.ShapeDtypeStruct((t, GATE_W), F32)] * 3,
        compiler_params=_params(("parallel",)),
        name="gate_scans",
    )(gc, bias, alog)


N_ROWS = 16
_WORK, _DONE = "work", "done"


def _mlstm_phases(refs, nc, zero_init):
    (qf, kf, vf, of, colf, bcolf, rowf, qb, kb, vb, ob, colb, bcolb, rowb, gn) = refs[:15]
    pos = 15
    if not zero_init:
        c0, n0, m0 = refs[pos:pos + 3]
        pos += 3
    hm_out, c_out, n_out, m_out, c_s, n_s, m_s, hpart, hbuf = refs[pos:]
    s = pl.program_id(1)
    half = nc // 2

    @pl.when(s == 0)
    def _():
        hpart[...] = jnp.zeros_like(hpart)
        if zero_init:
            c_s[...] = jnp.zeros_like(c_s)
            n_s[...] = jnp.zeros_like(n_s)
            m_s[...] = jnp.zeros_like(m_s)
        else:
            c_s[...] = c0[...]
            m_s[...] = m0[...]
            for u in range(N_UNITS):
                n_s[u] = jnp.broadcast_to(n0[u:u + 1, :], (N_ROWS, M_DQK))

    yield _DONE
    lower, upper = _tri_masks()
    lane1 = lax.broadcasted_iota(jnp.int32, (1, LANES), 1)
    ones_rows = jnp.ones((N_ROWS, CHUNK), BF16)
    per_dir = ((qf, kf, vf, of, colf, bcolf, rowf, lower, s),
               (qb, kb, vb, ob, colb, bcolb, rowb, upper, nc - 1 - s))
    m_prev = m_s[...]
    m_next = m_prev
    c_new, n_new = [], []
    for d, (q_ref, k_ref, v_ref, o_ref, col_ref, bcol_ref, row_ref, mask, chunk) in enumerate(per_dir):
        last = CHUNK - 1 if d == 0 else 0
        stab = jnp.maximum(m_prev, col_ref[...])
        w_c = jnp.exp(m_prev - stab)
        stab_last = stab[last:last + 1, :]
        m_new = bcol_ref[last:last + 1, :] + stab_last
        decay = w_c[last:last + 1, :]
        c_old = [c_s[d, h] for h in range(M_HEADS)]
        n_old = [n_s[d * M_HEADS + h] for h in range(M_HEADS)]
        for h in range(M_HEADS):
            u = d * M_HEADS + h
            stab_b = _lane_bcast(stab, u)
            u_row = row_ref[u:u + 1, :]
            p = jnp.exp(jnp.where(mask, u_row - stab_b, -jnp.inf))
            qh = q_ref[:, h * M_DQK:(h + 1) * M_DQK] * (M_DQK ** -0.5)
            k_t = k_ref[:, h * M_DQK:(h + 1) * M_DQK].T
            v16 = v_ref[:, h * M_DV:(h + 1) * M_DV].astype(BF16)
            sc = _dot(qh.astype(BF16), k_t.astype(BF16)) * p
            lhs = jnp.concatenate([sc.astype(BF16), (qh * _lane_bcast(w_c, u)).astype(BF16)], axis=1)
            num = _dot(lhs, jnp.concatenate([v16, c_old[h].astype(BF16)], axis=0))
            den = _dot(jnp.concatenate([ones_rows, n_old[h].astype(BF16)], axis=1), lhs, _NT)[0:1, :]
            m_t = row_ref[G_B + u:G_B + u + 1, :] + jnp.maximum(m_prev[:, u:u + 1],
                                                                row_ref[G_CU + u:G_CU + u + 1, :])
            inv_row = 1.0 / jnp.maximum(jnp.abs(den), jnp.exp(-m_t))
            inv = jnp.broadcast_to(inv_row, (CHUNK, LANES)).T
            hbuf[d, :, h * M_DV:(h + 1) * M_DV] = num * jnp.concatenate([inv, inv], axis=1)

            kw_t = (k_t * jnp.exp(u_row - stab_last[:, u:u + 1])).astype(BF16)
            dec = decay[:, u:u + 1]
            c_new.append(dec * c_old[h] + _dot(kw_t, v16))
            n_new.append(dec * n_old[h] + _dot(ones_rows, kw_t, _NT))
            yield _WORK
        mine = (lane1 >= d * M_HEADS) & (lane1 < (d + 1) * M_HEADS)
        m_next = jnp.where(mine, m_new, m_next)
    for u in range(N_UNITS):
        c_s[u // M_HEADS, u % M_HEADS] = c_new[u]
        n_s[u] = n_new[u]
    m_s[...] = m_next

    yield _DONE
    for d, (_, _, _, o_ref, _, _, _, _, chunk) in enumerate(per_dir):
        rows = pl.ds(pl.multiple_of(chunk * CHUNK, CHUNK), CHUNK)
        for h in range(M_HEADS):
            cols = slice(h * M_DV, (h + 1) * M_DV)
            tot = hbuf[d, :, cols] + hpart[rows, cols]
            hm_out[rows, cols] = (_rms(tot) * gn[:, cols] * o_ref[:, cols]).astype(BF16)
        hpart[rows, :] = hbuf[d]

    @pl.when(s == nc - 1)
    def _():
        c_out[...] = c_s[...]
        m_out[...] = m_s[...]
        for u in range(N_UNITS):
            n_out[u:u + 1, :] = n_s[u][0:1, :]

    yield _DONE


def _sequence_out_spec(seq, width):
    mode = dict(pipeline_mode=pl.Buffered(1)) if seq * width * 2 >= 2 * 1024 * 1024 else {}
    return pl.BlockSpec((None, seq, width), lambda b, s: (b, 0, 0), **mode)


def _mlstm_specs(p, scans, g_norm, state, nb, nc):
    zero_init = state is None
    seq = nc * CHUNK
    col, bcol, row = scans

    def fwd(cb):
        return lambda b, s: (b * nc + s, cb)

    def bwd(cb):
        return lambda b, s: (b * nc + nc - 1 - s, cb)

    def stream(mk):
        return [pl.BlockSpec((CHUNK, M_HEADS * M_DQK), mk(P_Q // (M_HEADS * M_DQK))),
                pl.BlockSpec((CHUNK, M_HEADS * M_DQK), mk(P_K // (M_HEADS * M_DQK))),
                pl.BlockSpec((CHUNK, MIX_M), mk(P_V // MIX_M)),
                pl.BlockSpec((CHUNK, MIX_M), mk(P_O // MIX_M)),
                pl.BlockSpec((CHUNK, GATE_W), mk(0)),
                pl.BlockSpec((CHUNK, GATE_W), mk(0)),
                pl.BlockSpec((CHUNK, GATE_W), mk(0))]

    in_specs = stream(fwd) + stream(bwd) + [pl.BlockSpec((1, MIX_M), lambda b, s: (0, 0))]
    args = [p] * 4 + [col, bcol, row] + [p] * 4 + [col, bcol, row, g_norm]
    state_specs = [pl.BlockSpec((None, N_DIR, M_HEADS, M_DQK, M_DV), lambda b, s: (b, 0, 0, 0, 0)),
                   pl.BlockSpec((None, N_UNITS, M_DQK), lambda b, s: (b, 0, 0)),
                   pl.BlockSpec((None, 1, LANES), lambda b, s: (b, 0, 0))]
    if not zero_init:
        in_specs += state_specs
        args += list(state)
    out_specs = [_sequence_out_spec(seq, MIX_M)] + state_specs
    out_shape = [jax.ShapeDtypeStruct((nb, seq, MIX_M), BF16),
                 jax.ShapeDtypeStruct((nb, N_DIR, M_HEADS, M_DQK, M_DV), F32),
                 jax.ShapeDtypeStruct((nb, N_UNITS, M_DQK), F32),
                 jax.ShapeDtypeStruct((nb, 1, LANES), F32)]
    scratch = [pltpu.VMEM((N_DIR, M_HEADS, M_DQK, M_DV), F32),
               pltpu.VMEM((N_UNITS, N_ROWS, M_DQK), F32),
               pltpu.VMEM((1, LANES), F32),
               pltpu.VMEM((seq, MIX_M), F32),
               pltpu.VMEM((N_DIR, CHUNK, MIX_M), F32)]
    return in_specs, args, out_specs, out_shape, scratch


def _ssd_phases(refs, nc, zero_init):
    (xf, bf, cf, zf, colf, rowf, xb, bb, cb, zb, colb, rowb, dskip, gn) = refs[:14]
    pos = 14
    if not zero_init:
        s0 = refs[pos]
        pos += 1
    ys_out, s_out, st_s, ypart, ybuf = refs[pos:]
    s = pl.program_id(1)
    half = nc // 2

    @pl.when(s == 0)
    def _():
        ypart[...] = jnp.zeros_like(ypart)
        if zero_init:
            st_s[...] = jnp.zeros_like(st_s)
        else:
            for d in range(N_DIR):
                for pr in range(S_PAIRS):
                    st_s[d, pr] = s0[d, pr].T

    yield _DONE
    lower, upper = _tri_masks()
    low_half = lax.broadcasted_iota(jnp.int32, (CHUNK, LANES), 1) < S_HEADDIM
    low_half1 = lax.broadcasted_iota(jnp.int32, (1, LANES), 1) < S_HEADDIM
    per_dir = ((xf, bf, cf, zf, colf, rowf, lower, s), (xb, bb, cb, zb, colb, rowb, upper, nc - 1 - s))
    st_new = []
    for d, (x_ref, b_ref, c_ref, z_ref, col_ref, row_ref, mask, chunk) in enumerate(per_dir):
        last = CHUNK - 1 if d == 0 else 0
        cs_c = col_ref[...]
        total = cs_c[last:last + 1, :]
        st_old = [st_s[d, pr] for pr in range(S_PAIRS)]
        for g in range(S_GROUPS):
            gs = slice(g * S_STATE, (g + 1) * S_STATE)
            cg = c_ref[:, gs].astype(BF16)
            b_f32 = b_ref[:, gs]
            cbm = _dot(cg, b_f32.astype(BF16), _NT)
            b_t = b_f32.T
            for pg in range(PAIRS_PER_GROUP):
                pr = g * PAIRS_PER_GROUP + pg
                cols = slice(pr * LANES, (pr + 1) * LANES)
                x16 = x_ref[:, cols].astype(BF16)
                zero16 = jnp.zeros_like(x16)
                halves = (jnp.where(low_half, x16, zero16), jnp.where(low_half, zero16, x16))
                state = st_old[pr]
                y_acc = jnp.zeros((CHUNK, LANES), F32)
                s_acc = jnp.zeros((S_STATE, LANES), F32)
                cs_b = []
                for e in range(PAIR):
                    gi = G_DT + d * S_HEADS + pr * PAIR + e
                    cs_b.append(_lane_bcast(cs_c, gi))
                    q_row = row_ref[gi:gi + 1, :]
                    wk_row = row_ref[gi + G_WK - G_DT:gi + G_WK - G_DT + 1, :]
                    mix = cbm * jnp.exp(jnp.where(mask, cs_b[e] - q_row, -jnp.inf))
                    y_acc = y_acc + _dot(mix.astype(BF16), halves[e])
                    s_acc = s_acc + _dot((b_t * wk_row).astype(BF16), halves[e])
                carry = jnp.exp(jnp.where(low_half, cs_b[0], cs_b[1]))
                ybuf[d, :, cols] = y_acc + _dot(cg, state.astype(BF16)) * carry
                gi0 = G_DT + d * S_HEADS + pr * PAIR
                dec = jnp.exp(jnp.where(low_half1, jnp.broadcast_to(total[:, gi0:gi0 + 1], (1, LANES)),
                                        jnp.broadcast_to(total[:, gi0 + 1:gi0 + 2], (1, LANES))))
                st_new.append(state * dec + s_acc)
            yield _WORK
    for d in range(N_DIR):
        for pr in range(S_PAIRS):
            st_s[d, pr] = st_new[d * S_PAIRS + pr]

    yield _DONE
    for d, (x_ref, _, _, z_ref, _, _, _, chunk) in enumerate(per_dir):
        rows = pl.ds(pl.multiple_of(chunk * CHUNK, CHUNK), CHUNK)
        y = ybuf[d] + ypart[rows, :] + dskip[...] * x_ref[...]
        ys_out[rows, :] = (_rms(y * z_ref[...]) * gn[...]).astype(BF16)
        ypart[rows, :] = ybuf[d]

    @pl.when(s == nc - 1)
    def _():
        for d in range(N_DIR):
            for pr in range(S_PAIRS):
                s_out[d, pr] = st_s[d, pr].T

    yield _DONE


def _ssd_specs(p, xbc, scans, dskip, g_norm, state, nb, nc):
    zero_init = state is None
    seq = nc * CHUNK
    bc_w = S_GROUPS * S_STATE
    col, _, row = scans

    def fwd(cb):
        return lambda b, s: (b * nc + s, cb)

    def bwd(cb):
        return lambda b, s: (b * nc + nc - 1 - s, cb)

    def stream(mk):
        return [pl.BlockSpec((CHUNK, MIX_S), mk(0)),
                pl.BlockSpec((CHUNK, bc_w), mk(MIX_S // bc_w)),
                pl.BlockSpec((CHUNK, bc_w), mk(MIX_S // bc_w + 1)),
                pl.BlockSpec((CHUNK, MIX_S), mk(P_Z // MIX_S)),
                pl.BlockSpec((CHUNK, GATE_W), mk(0)),
                pl.BlockSpec((CHUNK, GATE_W), mk(0))]

    const = lambda b, s: (0, 0)
    in_specs = stream(fwd) + stream(bwd) + [pl.BlockSpec((1, MIX_S), const), pl.BlockSpec((1, MIX_S), const)]
    args = [xbc, xbc, xbc, p, col, row] * 2 + [dskip, g_norm]
    state_spec = pl.BlockSpec((None, N_DIR, S_PAIRS, LANES, S_STATE), lambda b, s: (b, 0, 0, 0, 0))
    if not zero_init:
        in_specs.append(state_spec)
        args.append(state)
    out_specs = [_sequence_out_spec(seq, MIX_S), state_spec]
    out_shape = [jax.ShapeDtypeStruct((nb, seq, MIX_S), BF16),
                 jax.ShapeDtypeStruct((nb, N_DIR, S_PAIRS, LANES, S_STATE), F32)]
    scratch = [pltpu.VMEM((N_DIR, S_PAIRS, S_STATE, LANES), F32),
               pltpu.VMEM((seq, MIX_S), F32),
               pltpu.VMEM((N_DIR, CHUNK, MIX_S), F32)]
    return in_specs, args, out_specs, out_shape, scratch


def _mixer_kernel(*refs, nc, zero_init, counts):
    (m_in, s_in), (m_out, s_out), (m_scr, s_scr) = counts
    pos = 0
    parts = []
    for n in (m_in, s_in, m_out, s_out, m_scr, s_scr):
        parts.append(refs[pos:pos + n])
        pos += n
    gens = (_mlstm_phases(parts[0] + parts[2] + parts[4], nc, zero_init),
            _ssd_phases(parts[1] + parts[3] + parts[5], nc, zero_init))
    for _ in range(3):
        live = list(gens)
        while live:
            live = [g for g in live if next(g) is not _DONE]


def _mixers(p, xbc, scans, g_mlstm, dskip, g_ssd, mlstm_state, ssd_state, nb, nc):
    m = _mlstm_specs(p, scans, g_mlstm, mlstm_state, nb, nc)
    s = _ssd_specs(p, xbc, scans, dskip, g_ssd, ssd_state, nb, nc)
    counts = tuple((len(a), len(b)) for a, b in ((m[0], s[0]), (m[2], s[2]), (m[4], s[4])))
    return pl.pallas_call(
        functools.partial(_mixer_kernel, nc=nc, zero_init=mlstm_state is None, counts=counts),
        grid=(nb, nc),
        in_specs=m[0] + s[0],
        out_specs=m[2] + s[2],
        out_shape=m[3] + s[3],
        scratch_shapes=m[4] + s[4],
        compiler_params=_params(("parallel", "arbitrary"), 60 * 1024 * 1024),
        name="mixers",
    )(*(m[1] + s[1]))


OUT_TM = 512


OUT_SUB = 256


def _outproj_kernel(hm_ref, ys_ref, w_ref, x_ref, g1_ref, gpost_ref, gpre_ref, sc_ref, sh_ref, x1_ref, u2_ref):
    for r in range(OUT_TM // OUT_SUB):
        rows = slice(r * OUT_SUB, (r + 1) * OUT_SUB)
        mix = _dot(jnp.concatenate([hm_ref[rows, :], ys_ref[rows, :]], axis=1), w_ref[...])
        x1 = x_ref[rows, :] + g1_ref[...] * (_rms(mix) * gpost_ref[...])
        x1_ref[rows, :] = x1
        u2_ref[rows, :] = (_rms(x1) * gpre_ref[...] * (1.0 + sc_ref[...]) + sh_ref[...]).astype(BF16)


def _out_proj(hm, ys, w_out16, x2d, g1, gpost, gpre, sc2, sh2, rows_per_mod):
    t = x2d.shape[0]
    tiles_per_mod = rows_per_mod // OUT_TM
    row = lambda i: (i, 0)
    const = lambda i: (0, 0)
    mod = pl.BlockSpec((None, 1, D_MODEL), lambda i: (i // tiles_per_mod, 0, 0))
    return pl.pallas_call(
        _outproj_kernel,
        grid=(t // OUT_TM,),
        in_specs=[pl.BlockSpec((OUT_TM, MIX_M), row), pl.BlockSpec((OUT_TM, MIX_S), row),
                  pl.BlockSpec((D_MODEL, D_MODEL), const), pl.BlockSpec((OUT_TM, D_MODEL), row),
                  mod, pl.BlockSpec((1, D_MODEL), const), pl.BlockSpec((1, D_MODEL), const), mod, mod],
        out_specs=[pl.BlockSpec((OUT_TM, D_MODEL), row), pl.BlockSpec((OUT_TM, D_MODEL), row)],
        out_shape=[jax.ShapeDtypeStruct((t, D_MODEL), F32), jax.ShapeDtypeStruct((t, D_MODEL), BF16)],
        compiler_params=_params(("parallel",)),
        name="out_proj",
    )(hm, ys, w_out16, x2d, g1, gpost, gpre, sc2, sh2)


MLP_TM = 1024
MLP_TH = 512
MLP_SUB = 512


def _mlp_kernel(u_ref, w1_ref, w2_ref, x1_hbm, g2_ref, gpost_ref, o_ref, x1_buf, x1_sem):
    i = pl.program_id(0)
    j = pl.program_id(1)

    def x1_copy():
        rows = pl.ds(pl.multiple_of(i * MLP_TM, MLP_TM), MLP_TM)
        return pltpu.make_async_copy(x1_hbm.at[rows, :], x1_buf, x1_sem)

    def partial_sum(rows, w1, w2):
        hid = jnp.square(jnp.maximum(_dot(u_ref[rows, :], w1), 0.0)).astype(BF16)
        return _dot(hid, w2)

    sub_blocks = [slice(r * MLP_SUB, (r + 1) * MLP_SUB) for r in range(MLP_TM // MLP_SUB)]
    last = pl.num_programs(1) - 1

    @pl.when(j == 0)
    def _():
        x1_copy().start()
        w1, w2 = w1_ref[...].astype(BF16), w2_ref[...].astype(BF16)
        for rows in sub_blocks:
            o_ref[rows, :] = partial_sum(rows, w1, w2)

    @pl.when((j > 0) & (j < last))
    def _():
        w1, w2 = w1_ref[...].astype(BF16), w2_ref[...].astype(BF16)
        for rows in sub_blocks:
            o_ref[rows, :] += partial_sum(rows, w1, w2)

    @pl.when(j == last)
    def _():
        x1_copy().wait()
        w1, w2 = w1_ref[...].astype(BF16), w2_ref[...].astype(BF16)
        for rows in sub_blocks:
            total = o_ref[rows, :] + partial_sum(rows, w1, w2)
            o_ref[rows, :] = x1_buf[rows, :] + g2_ref[...] * (_rms(total) * gpost_ref[...])


def _mlp(u2, w1, w2, x1, g2, gpost, rows_per_mod):
    t = u2.shape[0]
    tiles_per_mod = rows_per_mod // MLP_TM
    assert D_FF // MLP_TH >= 2
    return pl.pallas_call(
        _mlp_kernel,
        grid=(t // MLP_TM, D_FF // MLP_TH),
        in_specs=[pl.BlockSpec((MLP_TM, D_MODEL), lambda i, j: (i, 0)),
                  pl.BlockSpec((D_MODEL, MLP_TH), lambda i, j: (0, j)),
                  pl.BlockSpec((MLP_TH, D_MODEL), lambda i, j: (j, 0)),
                  pl.BlockSpec(memory_space=pl.ANY),
                  pl.BlockSpec((None, 1, D_MODEL), lambda i, j: (i // tiles_per_mod, 0, 0)),
                  pl.BlockSpec((1, D_MODEL), lambda i, j: (0, 0))],
        out_specs=pl.BlockSpec((MLP_TM, D_MODEL), lambda i, j: (i, 0)),
        out_shape=jax.ShapeDtypeStruct((t, D_MODEL), F32),
        scratch_shapes=[pltpu.VMEM((MLP_TM, D_MODEL), F32), pltpu.SemaphoreType.DMA(())],
        compiler_params=_params(("parallel", "arbitrary"), 60 * 1024 * 1024),
        name="mlp",
    )(u2, w1, w2, x1, g2, gpost)


def _gate_row(i_vals, f_vals, dt_vals):
    v = jnp.concatenate([i_vals.reshape(-1), f_vals.reshape(-1), dt_vals.reshape(-1)]).astype(F32)
    return jnp.pad(v, (0, GATE_W - N_GATES)).reshape(1, GATE_W)


def _block(x, mods, state, weights, width):
    nb, seq, _ = x.shape
    nc = seq // CHUNK
    t = nb * seq
    x2d = x.reshape(t, D_MODEL)
    sh1, sc1, g1, sh2, sc2, g2 = mods
    rows_per_mod = t // sh1.shape[0]
    w = weights

    p, gc = _in_proj(x2d, sc1, sh1, w["g_pre_mix"], w["w_t"], w["wg"], rows_per_mod)
    scans = _gate_scans(gc, w["gate_bias"], w["gate_alog"])
    xbc = _conv(p, w["conv_w9"], w["conv_b"], nb, seq, width, CONV_BLOCK_ELEMS // seq)

    if state is None:
        m_state = s_state = None
    else:
        c0, n0, m0, s0 = state
        m_state = (c0, n0.reshape(nb, N_UNITS, M_DQK),
                   jnp.pad(m0.reshape(nb, 1, N_UNITS), ((0, 0), (0, 0), (0, LANES - N_UNITS))))
        s_state = s0.reshape(nb, N_DIR, S_PAIRS, LANES, S_STATE)
    hm, c_new, n_new, m_new, ys, s_new = _mixers(p, xbc, scans, w["g_mlstm_norm"], w["dskip"],
                                                 w["g_ssd_norm"], m_state, s_state, nb, nc)

    x1, u2 = _out_proj(hm.reshape(t, MIX_M), ys.reshape(t, MIX_S), w["w_out"], x2d, g1,
                       w["g_post_mix"], w["g_pre_mlp"], sc2, sh2, rows_per_mod)
    y = _mlp(u2, w["w_mlp_in"], w["w_mlp_out"], x1, g2, w["g_post_mlp"], rows_per_mod)
    new_state = (c_new.reshape(nb, 1, N_DIR, M_HEADS, M_DQK, M_DV),
                 n_new.reshape(nb, 1, N_DIR, M_HEADS, M_DQK),
                 m_new[:, 0, :N_UNITS].reshape(nb, 1, N_DIR, M_HEADS),
                 s_new.reshape(nb, 1, N_DIR, S_HEADS, S_HEADDIM, S_STATE))
    return y.reshape(nb, seq, D_MODEL), new_state


def kernel(x_prompt, x_sample, state_mlstm_c, state_mlstm_n, state_mlstm_m, state_ssd, c, c_ctx, w_mod, b_mod,
           g_pre_mix, g_post_mix, w_in, b_igate, b_fgate, conv_w, conv_b, dt_bias, a_log, d_skip, g_mlstm_norm,
           g_ssd_norm, w_out, g_pre_mlp, g_post_mlp, w_mlp_in, w_mlp_out):
    assert w_mod.shape[0] == 1, "one layer"
    nb_s = x_sample.shape[0]

    cond8 = jnp.zeros((8, D_MODEL), F32).at[0].set(c_ctx).at[1:1 + nb_s].set(c)
    mod = _modulation(cond8, w_mod[0], b_mod[0].reshape(1, -1))
    mods = [mod[:, k * D_MODEL:(k + 1) * D_MODEL] for k in range(6)]
    mods_p = [m[0:1].reshape(1, 1, D_MODEL) for m in mods]
    mods_s = [m[1:1 + nb_s].reshape(nb_s, 1, D_MODEL) for m in mods]

    w_t = w_in[0].T.astype(BF16)
    assert w_t.shape[0] == P_MAIN + N_GATES
    gate_rows = jnp.concatenate([w_t[P_Z:P_Z + G_DT], w_t[P_MAIN + G_DT:]], axis=0)
    gate_rows = jnp.pad(gate_rows, ((0, GATE_W - N_GATES), (0, 0)))
    zeros_u = jnp.zeros((N_UNITS,), F32)
    row = lambda v: v.reshape(1, -1)
    weights = dict(
        w_t=w_t, wg=gate_rows,
        g_pre_mix=row(g_pre_mix[0]), g_post_mix=row(g_post_mix[0]),
        g_pre_mlp=row(g_pre_mlp[0]), g_post_mlp=row(g_post_mlp[0]),
        conv_w9=conv_w[0].reshape(9, XBC), conv_b=row(conv_b[0]),
        gate_bias=_gate_row(b_igate[0], b_fgate[0], dt_bias[0]),
        gate_alog=_gate_row(zeros_u, zeros_u, a_log[0]),
        dskip=row(jnp.repeat(d_skip[0], S_HEADDIM)),
        g_mlstm_norm=row(g_mlstm_norm[0]), g_ssd_norm=row(g_ssd_norm[0]),
        w_out=w_out[0].astype(BF16), w_mlp_in=w_mlp_in[0], w_mlp_out=w_mlp_out[0])

    y_p, st = _block(x_prompt, mods_p, None, weights, x_prompt.shape[1])
    cache = (state_mlstm_c[:, 0], state_mlstm_n[:, 0], state_mlstm_m[:, 0], state_ssd[:, 0])
    y_s, _ = _block(x_sample, mods_s, cache, weights, GRID_W)
    return (y_p, y_s) + st
```

```python
import functools

import jax
import jax.numpy as jnp
from jax import lax
from jax.experimental import pallas as pl
from jax.experimental.pallas import tpu as pltpu

F32 = jnp.float32
BF16 = jnp.bfloat16

D_MODEL = 2048
CHUNK = 128
N_DIR = 2
M_HEADS = 4
M_DQK = 128
M_DV = 256
MIX_M = M_HEADS * M_DV
S_HEADS = 16
S_HEADDIM = 64
S_STATE = 128
S_GROUPS = 4
S_REP = S_HEADS // S_GROUPS
MIX_S = S_HEADS * S_HEADDIM
XBC = MIX_S + 2 * S_GROUPS * S_STATE
D_FF = 4 * D_MODEL
GRID_W = 64
EPS = 1e-6
LANES = 128

P_Q = 0
P_K = M_HEADS * M_DQK
P_V = 2 * M_HEADS * M_DQK
P_O = P_V + MIX_M
P_Z = P_O + MIX_M
P_XBC = P_Z + MIX_S
P_MAIN = P_XBC + XBC
GATE_W = LANES
N_UNITS = N_DIR * M_HEADS
G_I = 0
G_F = N_UNITS
G_DT = 2 * N_UNITS
N_GATES = G_DT + N_DIR * S_HEADS
G_WK = G_DT + N_DIR * S_HEADS
G_CU = G_WK + N_DIR * S_HEADS
G_B = G_CU + N_UNITS
assert G_B + N_UNITS <= GATE_W

PAIR = LANES // S_HEADDIM
S_PAIRS = S_HEADS // PAIR
PAIRS_PER_GROUP = S_REP // PAIR

VMEM_LIMIT = 48 * 1024 * 1024

_NT = (((1,), (1,)), ((), ()))


def _params(sem, limit=VMEM_LIMIT):
    return pltpu.CompilerParams(dimension_semantics=sem, vmem_limit_bytes=limit)


def _silu(x):
    return x / (1.0 + jnp.exp(-x))


def _sigmoid(x):
    return 1.0 / (1.0 + jnp.exp(-x))


def _rms(x):
    return x * lax.rsqrt(jnp.mean(x * x, axis=-1, keepdims=True) + EPS)


def _dot(a, b, dims=None, precision=None):
    if dims is None:
        dims = (((a.ndim - 1,), (0,)), ((), ()))
    return lax.dot_general(a, b, dims, precision=precision, preferred_element_type=F32)


def _lane_bcast(tile, lane):
    return jnp.broadcast_to(tile[:, lane:lane + 1], tile.shape)


def _mod_kernel(c_ref, w_ref, b_ref, o_ref):
    a = _silu(c_ref[...]).astype(BF16)
    o_ref[...] = _dot(a, w_ref[...].astype(BF16)) + b_ref[...]


def _modulation(cond8, w_mod, b_mod):
    n = w_mod.shape[1]
    tn = 1024
    return pl.pallas_call(
        _mod_kernel,
        grid=(n // tn,),
        in_specs=[pl.BlockSpec((8, D_MODEL), lambda j: (0, 0)),
                  pl.BlockSpec((D_MODEL, tn), lambda j: (0, j)),
                  pl.BlockSpec((1, tn), lambda j: (0, j))],
        out_specs=pl.BlockSpec((8, tn), lambda j: (0, j)),
        out_shape=jax.ShapeDtypeStruct((8, n), F32),
        compiler_params=_params(("parallel",)),
        name="modulation",
    )(cond8, w_mod, b_mod)


IN_TM = 1024
IN_TN = 1024
IN_SUB = 256


def _grid_conv(x, w_ref, b_ref, seq, width):
    n = x.shape[0]
    assert n % seq == 0 and seq % width == 0 and seq & (seq - 1) == 0 and width & (width - 1) == 0
    t = lax.broadcasted_iota(jnp.int32, x.shape, 0)
    c = jnp.bitwise_and(t, width - 1)
    xl = jnp.where(c >= 1, pltpu.roll(x, 1, 0), 0.0)
    xr = jnp.where(c <= width - 2, pltpu.roll(x, n - 1, 0), 0.0)

    def taps(di):
        return w_ref[3 * di:3 * di + 1, :] * xl + w_ref[3 * di + 1:3 * di + 2, :] * x \
            + w_ref[3 * di + 2:3 * di + 3, :] * xr

    out = taps(1) + b_ref[...]
    if seq > width:
        r = jnp.bitwise_and(t, seq - 1)
        out = out + jnp.where(r >= width, pltpu.roll(taps(0), width, 0), 0.0)
        out = out + jnp.where(r < seq - width, pltpu.roll(taps(2), n - width, 0), 0.0)
    return _silu(out)


def _inproj_kernel(x_ref, sc_ref, sh_ref, g_ref, w_ref, wg_ref, *rest, conv_width):
    if conv_width is None:
        o_ref, gc_ref, u_ref = rest
    else:
        cw_ref, cb_ref, o_ref, gc_ref, u_ref = rest
    j = pl.program_id(1)

    j_o, j_z = P_O // IN_TN, P_Z // IN_TN
    j_x = P_XBC // IN_TN
    plain = (j > 0) & (j != j_o) & (j != j_z)
    if conv_width is not None:
        plain = plain & (j < j_x)

        @pl.when(j >= j_x)
        def _():
            o_ref[...] = _grid_conv(_dot(u_ref[...], w_ref[...], _NT), cw_ref, cb_ref, conv_width, conv_width)

    @pl.when(j == 0)
    def _():
        for r in range(IN_TM // IN_SUB):
            rows = slice(r * IN_SUB, (r + 1) * IN_SUB)
            y = _rms(x_ref[rows, :]) * g_ref[...]
            u = (y * (1.0 + sc_ref[...]) + sh_ref[...]).astype(BF16)
            u_ref[rows, :] = u
            o_ref[rows, :] = _dot(u, w_ref[...], _NT)

    @pl.when(plain)
    def _():
        o_ref[...] = _dot(u_ref[...], w_ref[...], _NT)

    @pl.when((j == j_o) | (j == j_z))
    def _():
        acc = _dot(u_ref[...], w_ref[...], _NT)
        sig = _sigmoid(acc)
        o_ref[...] = jnp.where(j == j_o, sig, acc * sig)

    @pl.when(j == pl.num_programs(1) - 1)
    def _():
        gc_ref[...] = _dot(u_ref[...], wg_ref[...], _NT)


def _in_proj(x2d, sc, sh, g, w_t, wg, rows_per_mod, conv=None):
    t = x2d.shape[0]
    tiles_per_mod = rows_per_mod // IN_TM
    n_a = P_Z // IN_TN
    j_x = P_XBC // IN_TN
    assert P_O % IN_TN == 0 and MIX_M == IN_TN and MIX_S == IN_TN

    def first_row(i, j):
        return (j * (IN_TN // G_DT) + (j >= n_a).astype(jnp.int32)) * G_DT, 0

    conv_specs, conv_args, conv_width = [], [], None
    if conv is not None:
        conv_w9, conv_b, conv_width = conv
        assert IN_TM % conv_width == 0 and P_XBC % IN_TN == 0
        conv_specs = [pl.BlockSpec((9, IN_TN), lambda i, j: (0, jnp.maximum(j - j_x, 0))),
                      pl.BlockSpec((1, IN_TN), lambda i, j: (0, jnp.maximum(j - j_x, 0)))]
        conv_args = [conv_w9, conv_b]
    return pl.pallas_call(
        functools.partial(_inproj_kernel, conv_width=conv_width),
        grid=(t // IN_TM, P_MAIN // IN_TN),
        in_specs=[pl.BlockSpec((IN_TM, D_MODEL), lambda i, j: (i, 0)),
                  pl.BlockSpec((None, 1, D_MODEL), lambda i, j: (i // tiles_per_mod, 0, 0)),
                  pl.BlockSpec((None, 1, D_MODEL), lambda i, j: (i // tiles_per_mod, 0, 0)),
                  pl.BlockSpec((1, D_MODEL), lambda i, j: (0, 0)),
                  pl.BlockSpec((pl.Element(IN_TN), pl.Element(D_MODEL)), first_row),
                  pl.BlockSpec((GATE_W, D_MODEL), lambda i, j: (0, 0))] + conv_specs,
        out_specs=[pl.BlockSpec((IN_TM, IN_TN), lambda i, j: (i, j)),
                   pl.BlockSpec((IN_TM, GATE_W), lambda i, j: (i, 0))],
        out_shape=[jax.ShapeDtypeStruct((t, P_MAIN), F32),
                   jax.ShapeDtypeStruct((t, GATE_W), F32)],
        scratch_shapes=[pltpu.VMEM((IN_TM, D_MODEL), BF16)],
        compiler_params=_params(("parallel", "arbitrary"), 60 * 1024 * 1024),
        name="in_proj",
    )(x2d, sc, sh, g, w_t, wg, *conv_args)


def _conv_kernel(x_ref, w_ref, b_ref, o_ref, *, seq, width):
    o_ref[...] = _grid_conv(x_ref[...], w_ref, b_ref, seq, width)


CONV_BLOCK_ELEMS = 512 * 1024


def _conv(p, conv_w9, conv_b, nb, seq, width, cn):
    cn = min(cn, XBC)
    col0 = P_XBC // cn
    return pl.pallas_call(
        functools.partial(_conv_kernel, seq=seq, width=width),
        grid=(nb, XBC // cn),
        in_specs=[pl.BlockSpec((seq, cn), lambda b, j: (b, col0 + j)),
                  pl.BlockSpec((9, cn), lambda b, j: (0, j)),
                  pl.BlockSpec((1, cn), lambda b, j: (0, j))],
        out_specs=pl.BlockSpec((seq, cn), lambda b, j: (b, j)),
        out_shape=jax.ShapeDtypeStruct((nb * seq, XBC), F32),
        compiler_params=_params(("parallel", "parallel")),
        name="grid_conv",
    )(p, conv_w9, conv_b)


def _tri_masks():
    r = lax.broadcasted_iota(jnp.int32, (CHUNK, CHUNK), 0)
    c = lax.broadcasted_iota(jnp.int32, (CHUNK, CHUNK), 1)
    return r >= c, r <= c


SCAN_CHUNKS = 8


def _scan_kernel(gc_ref, bias_ref, alog_ref, col_ref, bcol_ref, row_ref):
    lower, upper = _tri_masks()
    lo, up = lower.astype(F32), upper.astype(F32)
    hi = lax.Precision.HIGHEST
    lane = lax.broadcasted_iota(jnp.int32, (CHUNK, GATE_W), 1)
    time = lax.broadcasted_iota(jnp.int32, (CHUNK, GATE_W), 0)
    lane1 = lax.broadcasted_iota(jnp.int32, (1, GATE_W), 1)

    def backward(l):
        unit_bwd = (l < G_DT) & (jnp.bitwise_and(l, N_UNITS - 1) >= M_HEADS)
        return unit_bwd | ((l >= G_DT + S_HEADS) & (l < N_GATES))

    is_bwd, is_bwd1 = backward(lane), backward(lane1)
    is_i = lane < G_F
    is_f = (lane >= G_F) & (lane < G_DT)
    is_dt = (lane >= G_DT) & (lane < N_GATES)
    neg_a = -jnp.exp(alog_ref[...])
    for c in range(SCAN_CHUNKS):
        rows = slice(c * CHUNK, (c + 1) * CHUNK)
        g = gc_ref[rows, :] + bias_ref[...]
        soft = jnp.log1p(jnp.exp(-jnp.abs(g)))
        logf = jnp.minimum(g, 0.0) - soft
        dt = jnp.maximum(g, 0.0) + soft
        x = jnp.where(is_f, logf, jnp.where(is_dt, dt * neg_a, 0.0))
        cs = jnp.where(is_bwd, _dot(up, x, precision=hi), _dot(lo, x, precision=hi))
        b_units = pltpu.roll(cs, GATE_W - G_F, 1)
        u = g - b_units
        cu = u
        k = 1
        while k < CHUNK:
            prev = jnp.where(time >= k, pltpu.roll(cu, k, 0), -jnp.inf)
            nxt = jnp.where(time < CHUNK - k, pltpu.roll(cu, CHUNK - k, 0), -jnp.inf)
            cu = jnp.maximum(cu, jnp.where(is_bwd, nxt, prev))
            k *= 2
        total = jnp.where(is_bwd1, cs[0:1, :], cs[CHUNK - 1:CHUNK, :])
        wk = jnp.exp(total - cs) * dt
        q = cs - jnp.log(dt)
        col_ref[rows, :] = jnp.where(is_i, cu, cs)
        bcol_ref[rows, :] = b_units
        tail = jnp.where(lane < G_CU, pltpu.roll(wk, G_WK - G_DT, 1),
                         jnp.where(lane < G_B, pltpu.roll(cu, G_CU, 1), pltpu.roll(b_units, G_B, 1)))
        row_ref[rows, :] = jnp.where(is_i, u, jnp.where(is_dt, q, tail)).T


def _gate_scans(gc, bias, alog):
    t = gc.shape[0]
    tm = SCAN_CHUNKS * CHUNK
    row = lambda i: (i, 0)
    const = lambda i: (0, 0)
    return pl.pallas_call(
        _scan_kernel,
        grid=(t // tm,),
        in_specs=[pl.BlockSpec((tm, GATE_W), row), pl.BlockSpec((1, GATE_W), const),
                  pl.BlockSpec((1, GATE_W), const)],
        out_specs=[pl.BlockSpec((tm, GATE_W), row)] * 3,
        out_shape=[jax.ShapeDtypeStruct((t, GATE_W), F32)] * 3,
        compiler_params=_params(("parallel",)),
        name="gate_scans",
    )(gc, bias, alog)


N_ROWS = 16
_WORK, _DONE = "work", "done"


def _mlstm_phases(refs, nc, zero_init):
    (qf, kf, vf, of, colf, bcolf, rowf, qb, kb, vb, ob, colb, bcolb, rowb, gn) = refs[:15]
    pos = 15
    if not zero_init:
        c0, n0, m0 = refs[pos:pos + 3]
        pos += 3
    hm_out, c_out, n_out, m_out, c_s, n_s, m_s, hpart, hbuf = refs[pos:]
    s = pl.program_id(1)
    half = nc // 2

    @pl.when(s == 0)
    def _():
        hpart[...] = jnp.zeros_like(hpart)
        if zero_init:
            c_s[...] = jnp.zeros_like(c_s)
            n_s[...] = jnp.zeros_like(n_s)
            m_s[...] = jnp.zeros_like(m_s)
        else:
            c_s[...] = c0[...]
            m_s[...] = m0[...]
            for u in range(N_UNITS):
                n_s[u] = jnp.broadcast_to(n0[u:u + 1, :], (N_ROWS, M_DQK))

    yield _DONE
    lower, upper = _tri_masks()
    lane1 = lax.broadcasted_iota(jnp.int32, (1, LANES), 1)
    ones_rows = jnp.ones((N_ROWS, CHUNK), BF16)
    per_dir = ((qf, kf, vf, of, colf, bcolf, rowf, lower, s),
               (qb, kb, vb, ob, colb, bcolb, rowb, upper, nc - 1 - s))
    m_prev = m_s[...]
    m_next = m_prev
    c_new, n_new = [], []
    for d, (q_ref, k_ref, v_ref, o_ref, col_ref, bcol_ref, row_ref, mask, chunk) in enumerate(per_dir):
        last = CHUNK - 1 if d == 0 else 0
        stab = jnp.maximum(m_prev, col_ref[...])
        w_c = jnp.exp(m_prev - stab)
        stab_last = stab[last:last + 1, :]
        m_new = bcol_ref[last:last + 1, :] + stab_last
        decay = w_c[last:last + 1, :]
        c_old = [c_s[d, h] for h in range(M_HEADS)]
        n_old = [n_s[d * M_HEADS + h] for h in range(M_HEADS)]
        for h in range(M_HEADS):
            u = d * M_HEADS + h
            stab_b = _lane_bcast(stab, u)
            u_row = row_ref[u:u + 1, :]
            p = jnp.exp(jnp.where(mask, u_row - stab_b, -jnp.inf))
            qh = q_ref[:, h * M_DQK:(h + 1) * M_DQK] * (M_DQK ** -0.5)
            k_t = k_ref[:, h * M_DQK:(h + 1) * M_DQK].T
            v16 = v_ref[:, h * M_DV:(h + 1) * M_DV].astype(BF16)
            sc = _dot(qh.astype(BF16), k_t.astype(BF16)) * p
            lhs = jnp.concatenate([sc.astype(BF16), (qh * _lane_bcast(w_c, u)).astype(BF16)], axis=1)
            num = _dot(lhs, jnp.concatenate([v16, c_old[h].astype(BF16)], axis=0))
            den = _dot(jnp.concatenate([ones_rows, n_old[h].astype(BF16)], axis=1), lhs, _NT)[0:1, :]
            m_t = row_ref[G_B + u:G_B + u + 1, :] + jnp.maximum(m_prev[:, u:u + 1],
                                                                row_ref[G_CU + u:G_CU + u + 1, :])
            inv_row = 1.0 / jnp.maximum(jnp.abs(den), jnp.exp(-m_t))
            inv = jnp.broadcast_to(inv_row, (CHUNK, LANES)).T
            hbuf[d, :, h * M_DV:(h + 1) * M_DV] = num * jnp.concatenate([inv, inv], axis=1)

            kw_t = (k_t * jnp.exp(u_row - stab_last[:, u:u + 1])).astype(BF16)
            dec = decay[:, u:u + 1]
            c_new.append(dec * c_old[h] + _dot(kw_t, v16))
            n_new.append(dec * n_old[h] + _dot(ones_rows, kw_t, _NT))
            yield _WORK
        mine = (lane1 >= d * M_HEADS) & (lane1 < (d + 1) * M_HEADS)
        m_next = jnp.where(mine, m_new, m_next)
    for u in range(N_UNITS):
        c_s[u // M_HEADS, u % M_HEADS] = c_new[u]
        n_s[u] = n_new[u]
    m_s[...] = m_next

    yield _DONE
    for d, (_, _, _, o_ref, _, _, _, _, chunk) in enumerate(per_dir):
        rows = pl.ds(pl.multiple_of(chunk * CHUNK, CHUNK), CHUNK)
        for h in range(M_HEADS):
            cols = slice(h * M_DV, (h + 1) * M_DV)
            tot = hbuf[d, :, cols] + hpart[rows, cols]
            hm_out[rows, cols] = (_rms(tot) * gn[:, cols] * o_ref[:, cols]).astype(BF16)
        hpart[rows, :] = hbuf[d]

    @pl.when(s == nc - 1)
    def _():
        c_out[...] = c_s[...]
        m_out[...] = m_s[...]
        for u in range(N_UNITS):
            n_out[u:u + 1, :] = n_s[u][0:1, :]

    yield _DONE


def _sequence_out_spec(seq, width):
    mode = dict(pipeline_mode=pl.Buffered(1)) if seq * width * 2 >= 2 * 1024 * 1024 else {}
    return pl.BlockSpec((None, seq, width), lambda b, s: (b, 0, 0), **mode)


def _mlstm_specs(p, scans, g_norm, state, nb, nc):
    zero_init = state is None
    seq = nc * CHUNK
    col, bcol, row = scans

    def fwd(cb):
        return lambda b, s: (b * nc + s, cb)

    def bwd(cb):
        return lambda b, s: (b * nc + nc - 1 - s, cb)

    def stream(mk):
        return [pl.BlockSpec((CHUNK, M_HEADS * M_DQK), mk(P_Q // (M_HEADS * M_DQK))),
                pl.BlockSpec((CHUNK, M_HEADS * M_DQK), mk(P_K // (M_HEADS * M_DQK))),
                pl.BlockSpec((CHUNK, MIX_M), mk(P_V // MIX_M)),
                pl.BlockSpec((CHUNK, MIX_M), mk(P_O // MIX_M)),
                pl.BlockSpec((CHUNK, GATE_W), mk(0)),
                pl.BlockSpec((CHUNK, GATE_W), mk(0)),
                pl.BlockSpec((CHUNK, GATE_W), mk(0))]

    in_specs = stream(fwd) + stream(bwd) + [pl.BlockSpec((1, MIX_M), lambda b, s: (0, 0))]
    args = [p] * 4 + [col, bcol, row] + [p] * 4 + [col, bcol, row, g_norm]
    state_specs = [pl.BlockSpec((None, N_DIR, M_HEADS, M_DQK, M_DV), lambda b, s: (b, 0, 0, 0, 0)),
                   pl.BlockSpec((None, N_UNITS, M_DQK), lambda b, s: (b, 0, 0)),
                   pl.BlockSpec((None, 1, LANES), lambda b, s: (b, 0, 0))]
    if not zero_init:
        in_specs += state_specs
        args += list(state)
    out_specs = [_sequence_out_spec(seq, MIX_M)] + state_specs
    out_shape = [jax.ShapeDtypeStruct((nb, seq, MIX_M), BF16),
                 jax.ShapeDtypeStruct((nb, N_DIR, M_HEADS, M_DQK, M_DV), F32),
                 jax.ShapeDtypeStruct((nb, N_UNITS, M_DQK), F32),
                 jax.ShapeDtypeStruct((nb, 1, LANES), F32)]
    scratch = [pltpu.VMEM((N_DIR, M_HEADS, M_DQK, M_DV), F32),
               pltpu.VMEM((N_UNITS, N_ROWS, M_DQK), F32),
               pltpu.VMEM((1, LANES), F32),
               pltpu.VMEM((seq, MIX_M), F32),
               pltpu.VMEM((N_DIR, CHUNK, MIX_M), F32)]
    return in_specs, args, out_specs, out_shape, scratch


def _ssd_phases(refs, nc, zero_init):
    (xf, bf, cf, zf, colf, rowf, xb, bb, cb, zb, colb, rowb, dskip, gn) = refs[:14]
    pos = 14
    if not zero_init:
        s0 = refs[pos]
        pos += 1
    ys_out, s_out, st_s, ypart, ybuf = refs[pos:]
    s = pl.program_id(1)
    half = nc // 2

    @pl.when(s == 0)
    def _():
        ypart[...] = jnp.zeros_like(ypart)
        if zero_init:
            st_s[...] = jnp.zeros_like(st_s)
        else:
            for d in range(N_DIR):
                for pr in range(S_PAIRS):
                    st_s[d, pr] = s0[d, pr].T

    yield _DONE
    lower, upper = _tri_masks()
    low_half = lax.broadcasted_iota(jnp.int32, (CHUNK, LANES), 1) < S_HEADDIM
    low_half1 = lax.broadcasted_iota(jnp.int32, (1, LANES), 1) < S_HEADDIM
    per_dir = ((xf, bf, cf, zf, colf, rowf, lower, s), (xb, bb, cb, zb, colb, rowb, upper, nc - 1 - s))
    st_new = []
    for d, (x_ref, b_ref, c_ref, z_ref, col_ref, row_ref, mask, chunk) in enumerate(per_dir):
        last = CHUNK - 1 if d == 0 else 0
        cs_c = col_ref[...]
        total = cs_c[last:last + 1, :]
        st_old = [st_s[d, pr] for pr in range(S_PAIRS)]
        for g in range(S_GROUPS):
            gs = slice(g * S_STATE, (g + 1) * S_STATE)
            cg = c_ref[:, gs].astype(BF16)
            b_f32 = b_ref[:, gs]
            cbm = _dot(cg, b_f32.astype(BF16), _NT)
            b_t = b_f32.T
            for pg in range(PAIRS_PER_GROUP):
                pr = g * PAIRS_PER_GROUP + pg
                cols = slice(pr * LANES, (pr + 1) * LANES)
                x16 = x_ref[:, cols].astype(BF16)
                zero16 = jnp.zeros_like(x16)
                halves = (jnp.where(low_half, x16, zero16), jnp.where(low_half, zero16, x16))
                state = st_old[pr]
                y_acc = jnp.zeros((CHUNK, LANES), F32)
                s_acc = jnp.zeros((S_STATE, LANES), F32)
                cs_b = []
                for e in range(PAIR):
                    gi = G_DT + d * S_HEADS + pr * PAIR + e
                    cs_b.append(_lane_bcast(cs_c, gi))
                    q_row = row_ref[gi:gi + 1, :]
                    wk_row = row_ref[gi + G_WK - G_DT:gi + G_WK - G_DT + 1, :]
                    mix = cbm * jnp.exp(jnp.where(mask, cs_b[e] - q_row, -jnp.inf))
                    y_acc = y_acc + _dot(mix.astype(BF16), halves[e])
                    s_acc = s_acc + _dot((b_t * wk_row).astype(BF16), halves[e])
                carry = jnp.exp(jnp.where(low_half, cs_b[0], cs_b[1]))
                ybuf[d, :, cols] = y_acc + _dot(cg, state.astype(BF16)) * carry
                gi0 = G_DT + d * S_HEADS + pr * PAIR
                dec = jnp.exp(jnp.where(low_half1, jnp.broadcast_to(total[:, gi0:gi0 + 1], (1, LANES)),
                                        jnp.broadcast_to(total[:, gi0 + 1:gi0 + 2], (1, LANES))))
                st_new.append(state * dec + s_acc)
            yield _WORK
    for d in range(N_DIR):
        for pr in range(S_PAIRS):
            st_s[d, pr] = st_new[d * S_PAIRS + pr]

    yield _DONE
    for d, (x_ref, _, _, z_ref, _, _, _, chunk) in enumerate(per_dir):
        rows = pl.ds(pl.multiple_of(chunk * CHUNK, CHUNK), CHUNK)
        y = ybuf[d] + ypart[rows, :] + dskip[...] * x_ref[...]
        ys_out[rows, :] = (_rms(y * z_ref[...]) * gn[...]).astype(BF16)
        ypart[rows, :] = ybuf[d]

    @pl.when(s == nc - 1)
    def _():
        for d in range(N_DIR):
            for pr in range(S_PAIRS):
                s_out[d, pr] = st_s[d, pr].T

    yield _DONE


def _ssd_specs(p, xbc, xbc_col, scans, dskip, g_norm, state, nb, nc):
    zero_init = state is None
    seq = nc * CHUNK
    bc_w = S_GROUPS * S_STATE
    assert xbc_col % MIX_S == 0
    col, _, row = scans

    def fwd(cb):
        return lambda b, s: (b * nc + s, cb)

    def bwd(cb):
        return lambda b, s: (b * nc + nc - 1 - s, cb)

    def stream(mk):
        return [pl.BlockSpec((CHUNK, MIX_S), mk(xbc_col // MIX_S)),
                pl.BlockSpec((CHUNK, bc_w), mk((xbc_col + MIX_S) // bc_w)),
                pl.BlockSpec((CHUNK, bc_w), mk((xbc_col + MIX_S) // bc_w + 1)),
                pl.BlockSpec((CHUNK, MIX_S), mk(P_Z // MIX_S)),
                pl.BlockSpec((CHUNK, GATE_W), mk(0)),
                pl.BlockSpec((CHUNK, GATE_W), mk(0))]

    const = lambda b, s: (0, 0)
    in_specs = stream(fwd) + stream(bwd) + [pl.BlockSpec((1, MIX_S), const), pl.BlockSpec((1, MIX_S), const)]
    args = [xbc, xbc, xbc, p, col, row] * 2 + [dskip, g_norm]
    state_spec = pl.BlockSpec((None, N_DIR, S_PAIRS, LANES, S_STATE), lambda b, s: (b, 0, 0, 0, 0))
    if not zero_init:
        in_specs.append(state_spec)
        args.append(state)
    out_specs = [_sequence_out_spec(seq, MIX_S), state_spec]
    out_shape = [jax.ShapeDtypeStruct((nb, seq, MIX_S), BF16),
                 jax.ShapeDtypeStruct((nb, N_DIR, S_PAIRS, LANES, S_STATE), F32)]
    scratch = [pltpu.VMEM((N_DIR, S_PAIRS, S_STATE, LANES), F32),
               pltpu.VMEM((seq, MIX_S), F32),
               pltpu.VMEM((N_DIR, CHUNK, MIX_S), F32)]
    return in_specs, args, out_specs, out_shape, scratch


def _mixer_kernel(*refs, nc, zero_init, counts):
    (m_in, s_in), (m_out, s_out), (m_scr, s_scr) = counts
    pos = 0
    parts = []
    for n in (m_in, s_in, m_out, s_out, m_scr, s_scr):
        parts.append(refs[pos:pos + n])
        pos += n
    gens = (_mlstm_phases(parts[0] + parts[2] + parts[4], nc, zero_init),
            _ssd_phases(parts[1] + parts[3] + parts[5], nc, zero_init))
    for _ in range(3):
        live = list(gens)
        while live:
            live = [g for g in live if next(g) is not _DONE]


def _mixers(p, xbc, xbc_col, scans, g_mlstm, dskip, g_ssd, mlstm_state, ssd_state, nb, nc):
    m = _mlstm_specs(p, scans, g_mlstm, mlstm_state, nb, nc)
    s = _ssd_specs(p, xbc, xbc_col, scans, dskip, g_ssd, ssd_state, nb, nc)
    counts = tuple((len(a), len(b)) for a, b in ((m[0], s[0]), (m[2], s[2]), (m[4], s[4])))
    return pl.pallas_call(
        functools.partial(_mixer_kernel, nc=nc, zero_init=mlstm_state is None, counts=counts),
        grid=(nb, nc),
        in_specs=m[0] + s[0],
        out_specs=m[2] + s[2],
        out_shape=m[3] + s[3],
        scratch_shapes=m[4] + s[4],
        compiler_params=_params(("parallel", "arbitrary"), 60 * 1024 * 1024),
        name="mixers",
    )(*(m[1] + s[1]))


OUT_TM = 512


OUT_SUB = 256


def _outproj_kernel(hm_ref, ys_ref, w_ref, x_ref, g1_ref, gpost_ref, gpre_ref, sc_ref, sh_ref, x1_ref, u2_ref):
    for r in range(OUT_TM // OUT_SUB):
        rows = slice(r * OUT_SUB, (r + 1) * OUT_SUB)
        mix = _dot(jnp.concatenate([hm_ref[rows, :], ys_ref[rows, :]], axis=1), w_ref[...])
        x1 = x_ref[rows, :] + g1_ref[...] * (_rms(mix) * gpost_ref[...])
        x1_ref[rows, :] = x1
        u2_ref[rows, :] = (_rms(x1) * gpre_ref[...] * (1.0 + sc_ref[...]) + sh_ref[...]).astype(BF16)


def _out_proj(hm, ys, w_out16, x2d, g1, gpost, gpre, sc2, sh2, rows_per_mod):
    t = x2d.shape[0]
    tiles_per_mod = rows_per_mod // OUT_TM
    row = lambda i: (i, 0)
    const = lambda i: (0, 0)
    mod = pl.BlockSpec((None, 1, D_MODEL), lambda i: (i // tiles_per_mod, 0, 0))
    return pl.pallas_call(
        _outproj_kernel,
        grid=(t // OUT_TM,),
        in_specs=[pl.BlockSpec((OUT_TM, MIX_M), row), pl.BlockSpec((OUT_TM, MIX_S), row),
                  pl.BlockSpec((D_MODEL, D_MODEL), const), pl.BlockSpec((OUT_TM, D_MODEL), row),
                  mod, pl.BlockSpec((1, D_MODEL), const), pl.BlockSpec((1, D_MODEL), const), mod, mod],
        out_specs=[pl.BlockSpec((OUT_TM, D_MODEL), row), pl.BlockSpec((OUT_TM, D_MODEL), row)],
        out_shape=[jax.ShapeDtypeStruct((t, D_MODEL), F32), jax.ShapeDtypeStruct((t, D_MODEL), BF16)],
        compiler_params=_params(("parallel",)),
        name="out_proj",
    )(hm, ys, w_out16, x2d, g1, gpost, gpre, sc2, sh2)


MLP_TM = 1024
MLP_TH = 512
MLP_SUB = 512


def _mlp_kernel(u_ref, w1_ref, w2_ref, x1_hbm, g2_ref, gpost_ref, o_ref, x1_buf, x1_sem):
    i = pl.program_id(0)
    j = pl.program_id(1)

    def x1_copy():
        rows = pl.ds(pl.multiple_of(i * MLP_TM, MLP_TM), MLP_TM)
        return pltpu.make_async_copy(x1_hbm.at[rows, :], x1_buf, x1_sem)

    def partial_sum(rows, w1, w2):
        hid = jnp.square(jnp.maximum(_dot(u_ref[rows, :], w1), 0.0)).astype(BF16)
        return _dot(hid, w2)

    sub_blocks = [slice(r * MLP_SUB, (r + 1) * MLP_SUB) for r in range(MLP_TM // MLP_SUB)]
    last = pl.num_programs(1) - 1

    @pl.when(j == 0)
    def _():
        x1_copy().start()
        w1, w2 = w1_ref[...].astype(BF16), w2_ref[...].astype(BF16)
        for rows in sub_blocks:
            o_ref[rows, :] = partial_sum(rows, w1, w2)

    @pl.when((j > 0) & (j < last))
    def _():
        w1, w2 = w1_ref[...].astype(BF16), w2_ref[...].astype(BF16)
        for rows in sub_blocks:
            o_ref[rows, :] += partial_sum(rows, w1, w2)

    @pl.when(j == last)
    def _():
        x1_copy().wait()
        w1, w2 = w1_ref[...].astype(BF16), w2_ref[...].astype(BF16)
        for rows in sub_blocks:
            total = o_ref[rows, :] + partial_sum(rows, w1, w2)
            o_ref[rows, :] = x1_buf[rows, :] + g2_ref[...] * (_rms(total) * gpost_ref[...])


def _mlp(u2, w1, w2, x1, g2, gpost, rows_per_mod):
    t = u2.shape[0]
    tiles_per_mod = rows_per_mod // MLP_TM
    assert D_FF // MLP_TH >= 2
    return pl.pallas_call(
        _mlp_kernel,
        grid=(t // MLP_TM, D_FF // MLP_TH),
        in_specs=[pl.BlockSpec((MLP_TM, D_MODEL), lambda i, j: (i, 0)),
                  pl.BlockSpec((D_MODEL, MLP_TH), lambda i, j: (0, j)),
                  pl.BlockSpec((MLP_TH, D_MODEL), lambda i, j: (j, 0)),
                  pl.BlockSpec(memory_space=pl.ANY),
                  pl.BlockSpec((None, 1, D_MODEL), lambda i, j: (i // tiles_per_mod, 0, 0)),
                  pl.BlockSpec((1, D_MODEL), lambda i, j: (0, 0))],
        out_specs=pl.BlockSpec((MLP_TM, D_MODEL), lambda i, j: (i, 0)),
        out_shape=jax.ShapeDtypeStruct((t, D_MODEL), F32),
        scratch_shapes=[pltpu.VMEM((MLP_TM, D_MODEL), F32), pltpu.SemaphoreType.DMA(())],
        compiler_params=_params(("parallel", "arbitrary"), 60 * 1024 * 1024),
        name="mlp",
    )(u2, w1, w2, x1, g2, gpost)


def _gate_row(i_vals, f_vals, dt_vals):
    v = jnp.concatenate([i_vals.reshape(-1), f_vals.reshape(-1), dt_vals.reshape(-1)]).astype(F32)
    return jnp.pad(v, (0, GATE_W - N_GATES)).reshape(1, GATE_W)


def _block(x, mods, state, weights, width):
    nb, seq, _ = x.shape
    nc = seq // CHUNK
    t = nb * seq
    x2d = x.reshape(t, D_MODEL)
    sh1, sc1, g1, sh2, sc2, g2 = mods
    rows_per_mod = t // sh1.shape[0]
    w = weights

    fused_conv = width == seq and IN_TM % seq == 0
    conv = (w["conv_w9"], w["conv_b"], width) if fused_conv else None
    p, gc = _in_proj(x2d, sc1, sh1, w["g_pre_mix"], w["w_t"], w["wg"], rows_per_mod, conv)
    scans = _gate_scans(gc, w["gate_bias"], w["gate_alog"])
    if fused_conv:
        xbc, xbc_col = p, P_XBC
    else:
        xbc, xbc_col = _conv(p, w["conv_w9"], w["conv_b"], nb, seq, width, CONV_BLOCK_ELEMS // seq), 0

    if state is None:
        m_state = s_state = None
    else:
        c0, n0, m0, s0 = state
        m_state = (c0, n0.reshape(nb, N_UNITS, M_DQK),
                   jnp.pad(m0.reshape(nb, 1, N_UNITS), ((0, 0), (0, 0), (0, LANES - N_UNITS))))
        s_state = s0.reshape(nb, N_DIR, S_PAIRS, LANES, S_STATE)
    hm, c_new, n_new, m_new, ys, s_new = _mixers(p, xbc, xbc_col, scans, w["g_mlstm_norm"], w["dskip"],
                                                 w["g_ssd_norm"], m_state, s_state, nb, nc)

    x1, u2 = _out_proj(hm.reshape(t, MIX_M), ys.reshape(t, MIX_S), w["w_out"], x2d, g1,
                       w["g_post_mix"], w["g_pre_mlp"], sc2, sh2, rows_per_mod)
    y = _mlp(u2, w["w_mlp_in"], w["w_mlp_out"], x1, g2, w["g_post_mlp"], rows_per_mod)
    new_state = (c_new.reshape(nb, 1, N_DIR, M_HEADS, M_DQK, M_DV),
                 n_new.reshape(nb, 1, N_DIR, M_HEADS, M_DQK),
                 m_new[:, 0, :N_UNITS].reshape(nb, 1, N_DIR, M_HEADS),
                 s_new.reshape(nb, 1, N_DIR, S_HEADS, S_HEADDIM, S_STATE))
    return y.reshape(nb, seq, D_MODEL), new_state


def kernel(x_prompt, x_sample, state_mlstm_c, state_mlstm_n, state_mlstm_m, state_ssd, c, c_ctx, w_mod, b_mod,
           g_pre_mix, g_post_mix, w_in, b_igate, b_fgate, conv_w, conv_b, dt_bias, a_log, d_skip, g_mlstm_norm,
           g_ssd_norm, w_out, g_pre_mlp, g_post_mlp, w_mlp_in, w_mlp_out):
    assert w_mod.shape[0] == 1, "one layer"
    nb_s = x_sample.shape[0]

    cond8 = jnp.zeros((8, D_MODEL), F32).at[0].set(c_ctx).at[1:1 + nb_s].set(c)
    mod = _modulation(cond8, w_mod[0], b_mod[0].reshape(1, -1))
    mods = [mod[:, k * D_MODEL:(k + 1) * D_MODEL] for k in range(6)]
    mods_p = [m[0:1].reshape(1, 1, D_MODEL) for m in mods]
    mods_s = [m[1:1 + nb_s].reshape(nb_s, 1, D_MODEL) for m in mods]

    w_t = w_in[0].T.astype(BF16)
    assert w_t.shape[0] == P_MAIN + N_GATES
    gate_rows = jnp.concatenate([w_t[P_Z:P_Z + G_DT], w_t[P_MAIN + G_DT:]], axis=0)
    gate_rows = jnp.pad(gate_rows, ((0, GATE_W - N_GATES), (0, 0)))
    zeros_u = jnp.zeros((N_UNITS,), F32)
    row = lambda v: v.reshape(1, -1)
    weights = dict(
        w_t=w_t, wg=gate_rows,
        g_pre_mix=row(g_pre_mix[0]), g_post_mix=row(g_post_mix[0]),
        g_pre_mlp=row(g_pre_mlp[0]), g_post_mlp=row(g_post_mlp[0]),
        conv_w9=conv_w[0].reshape(9, XBC), conv_b=row(conv_b[0]),
        gate_bias=_gate_row(b_igate[0], b_fgate[0], dt_bias[0]),
        gate_alog=_gate_row(zeros_u, zeros_u, a_log[0]),
        dskip=row(jnp.repeat(d_skip[0], S_HEADDIM)),
        g_mlstm_norm=row(g_mlstm_norm[0]), g_ssd_norm=row(g_ssd_norm[0]),
        w_out=w_out[0].astype(BF16), w_mlp_in=w_mlp_in[0], w_mlp_out=w_mlp_out[0])

    y_p, st = _block(x_prompt, mods_p, None, weights, x_prompt.shape[1])
    cache = (state_mlstm_c[:, 0], state_mlstm_n[:, 0], state_mlstm_m[:, 0], state_ssd[:, 0])
    y_s, _ = _block(x_sample, mods_s, cache, weights, GRID_W)
    return (y_p, y_s) + st
```

```python
import functools

import jax
import jax.numpy as jnp
from jax import lax
from jax.experimental import pallas as pl
from jax.experimental.pallas import tpu as pltpu

F32 = jnp.float32
BF16 = jnp.bfloat16

D_MODEL = 2048
CHUNK = 128
N_DIR = 2
M_HEADS = 4
M_DQK = 128
M_DV = 256
MIX_M = M_HEADS * M_DV
S_HEADS = 16
S_HEADDIM = 64
S_STATE = 128
S_GROUPS = 4
S_REP = S_HEADS // S_GROUPS
MIX_S = S_HEADS * S_HEADDIM
XBC = MIX_S + 2 * S_GROUPS * S_STATE
D_FF = 4 * D_MODEL
GRID_W = 64
EPS = 1e-6
LANES = 128

P_Q = 0
P_K = M_HEADS * M_DQK
P_V = 2 * M_HEADS * M_DQK
P_O = P_V + MIX_M
P_Z = P_O + MIX_M
P_XBC = P_Z + MIX_S
P_MAIN = P_XBC + XBC
GATE_W = LANES
N_UNITS = N_DIR * M_HEADS
G_I = 0
G_F = N_UNITS
G_DT = 2 * N_UNITS
N_GATES = G_DT + N_DIR * S_HEADS
G_WK = G_DT + N_DIR * S_HEADS
G_CU = G_WK + N_DIR * S_HEADS
G_B = G_CU + N_UNITS
assert G_B + N_UNITS <= GATE_W

PAIR = LANES // S_HEADDIM
S_PAIRS = S_HEADS // PAIR
PAIRS_PER_GROUP = S_REP // PAIR

MIB = 1024 * 1024
VMEM_LIMIT = 48 * MIB
VMEM_LIMIT_LARGE = 60 * MIB

_NT = (((1,), (1,)), ((), ()))


def _params(sem, limit=VMEM_LIMIT):
    return pltpu.CompilerParams(dimension_semantics=sem, vmem_limit_bytes=limit)


def _silu(x):
    return x / (1.0 + jnp.exp(-x))


def _sigmoid(x):
    return 1.0 / (1.0 + jnp.exp(-x))


def _rms(x):
    return x * lax.rsqrt(jnp.mean(x * x, axis=-1, keepdims=True) + EPS)


def _dot(a, b, dims=None, precision=None):
    if dims is None:
        dims = (((a.ndim - 1,), (0,)), ((), ()))
    return lax.dot_general(a, b, dims, precision=precision, preferred_element_type=F32)


def _lane_bcast(tile, lane):
    return jnp.broadcast_to(tile[:, lane:lane + 1], tile.shape)


def _mod_kernel(c_ref, w_ref, b_ref, o_ref):
    a = _silu(c_ref[...]).astype(BF16)
    o_ref[...] = _dot(a, w_ref[...].astype(BF16)) + b_ref[...]


def _modulation(cond8, w_mod, b_mod):
    n = w_mod.shape[1]
    tn = 1024
    return pl.pallas_call(
        _mod_kernel,
        grid=(n // tn,),
        in_specs=[pl.BlockSpec((8, D_MODEL), lambda j: (0, 0)),
                  pl.BlockSpec((D_MODEL, tn), lambda j: (0, j)),
                  pl.BlockSpec((1, tn), lambda j: (0, j))],
        out_specs=pl.BlockSpec((8, tn), lambda j: (0, j)),
        out_shape=jax.ShapeDtypeStruct((8, n), F32),
        compiler_params=_params(("parallel",)),
        name="modulation",
    )(cond8, w_mod, b_mod)


IN_TM = 1024
IN_TN = 1024
IN_SUB = 256


def _grid_conv(x, w_ref, b_ref, seq, width):
    n = x.shape[0]
    assert n % seq == 0 and seq % width == 0 and seq & (seq - 1) == 0 and width & (width - 1) == 0
    t = lax.broadcasted_iota(jnp.int32, x.shape, 0)
    c = jnp.bitwise_and(t, width - 1)
    xl = jnp.where(c >= 1, pltpu.roll(x, 1, 0), 0.0)
    xr = jnp.where(c <= width - 2, pltpu.roll(x, n - 1, 0), 0.0)

    def taps(di):
        return w_ref[3 * di:3 * di + 1, :] * xl + w_ref[3 * di + 1:3 * di + 2, :] * x \
            + w_ref[3 * di + 2:3 * di + 3, :] * xr

    out = taps(1) + b_ref[...]
    if seq > width:
        r = jnp.bitwise_and(t, seq - 1)
        out = out + jnp.where(r >= width, pltpu.roll(taps(0), width, 0), 0.0)
        out = out + jnp.where(r < seq - width, pltpu.roll(taps(2), n - width, 0), 0.0)
    return _silu(out)


def _inproj_kernel(x_ref, sc_ref, sh_ref, g_ref, w_ref, wg_ref, *rest, conv_width):
    if conv_width is None:
        o_ref, gc_ref, u_ref = rest
    else:
        cw_ref, cb_ref, o_ref, gc_ref, u_ref = rest
    j = pl.program_id(1)

    j_o, j_z = P_O // IN_TN, P_Z // IN_TN
    j_x = P_XBC // IN_TN
    plain = (j > 0) & (j != j_o) & (j != j_z)
    if conv_width is not None:
        plain = plain & (j < j_x)

        @pl.when(j >= j_x)
        def _():
            o_ref[...] = _grid_conv(_dot(u_ref[...], w_ref[...], _NT), cw_ref, cb_ref, conv_width, conv_width)

    @pl.when(j == 0)
    def _():
        for r in range(IN_TM // IN_SUB):
            rows = slice(r * IN_SUB, (r + 1) * IN_SUB)
            y = _rms(x_ref[rows, :]) * g_ref[...]
            u = (y * (1.0 + sc_ref[...]) + sh_ref[...]).astype(BF16)
            u_ref[rows, :] = u
            o_ref[rows, :] = _dot(u, w_ref[...], _NT)

    @pl.when(plain)
    def _():
        o_ref[...] = _dot(u_ref[...], w_ref[...], _NT)

    @pl.when((j == j_o) | (j == j_z))
    def _():
        acc = _dot(u_ref[...], w_ref[...], _NT)
        sig = _sigmoid(acc)
        o_ref[...] = jnp.where(j == j_o, sig, acc * sig)

    @pl.when(j == pl.num_programs(1) - 1)
    def _():
        gc_ref[...] = _dot(u_ref[...], wg_ref[...], _NT)


def _in_proj(x2d, sc, sh, g, w_t, wg, rows_per_mod, conv=None):
    t = x2d.shape[0]
    tiles_per_mod = rows_per_mod // IN_TM
    n_a = P_Z // IN_TN
    j_x = P_XBC // IN_TN
    assert P_O % IN_TN == 0 and MIX_M == IN_TN and MIX_S == IN_TN

    def first_row(i, j):
        return (j * (IN_TN // G_DT) + (j >= n_a).astype(jnp.int32)) * G_DT, 0

    conv_specs, conv_args, conv_width = [], [], None
    if conv is not None:
        conv_w9, conv_b, conv_width = conv
        assert IN_TM % conv_width == 0 and P_XBC % IN_TN == 0
        conv_specs = [pl.BlockSpec((9, IN_TN), lambda i, j: (0, jnp.maximum(j - j_x, 0))),
                      pl.BlockSpec((1, IN_TN), lambda i, j: (0, jnp.maximum(j - j_x, 0)))]
        conv_args = [conv_w9, conv_b]
    return pl.pallas_call(
        functools.partial(_inproj_kernel, conv_width=conv_width),
        grid=(t // IN_TM, P_MAIN // IN_TN),
        in_specs=[pl.BlockSpec((IN_TM, D_MODEL), lambda i, j: (i, 0)),
                  pl.BlockSpec((None, 1, D_MODEL), lambda i, j: (i // tiles_per_mod, 0, 0)),
                  pl.BlockSpec((None, 1, D_MODEL), lambda i, j: (i // tiles_per_mod, 0, 0)),
                  pl.BlockSpec((1, D_MODEL), lambda i, j: (0, 0)),
                  pl.BlockSpec((pl.Element(IN_TN), pl.Element(D_MODEL)), first_row),
                  pl.BlockSpec((GATE_W, D_MODEL), lambda i, j: (0, 0))] + conv_specs,
        out_specs=[pl.BlockSpec((IN_TM, IN_TN), lambda i, j: (i, j)),
                   pl.BlockSpec((IN_TM, GATE_W), lambda i, j: (i, 0))],
        out_shape=[jax.ShapeDtypeStruct((t, P_MAIN), F32),
                   jax.ShapeDtypeStruct((t, GATE_W), F32)],
        scratch_shapes=[pltpu.VMEM((IN_TM, D_MODEL), BF16)],
        compiler_params=_params(("parallel", "arbitrary"), VMEM_LIMIT_LARGE),
        name="in_proj",
    )(x2d, sc, sh, g, w_t, wg, *conv_args)


def _conv_kernel(x_ref, w_ref, b_ref, o_ref, *, seq, width):
    o_ref[...] = _grid_conv(x_ref[...], w_ref, b_ref, seq, width)


CONV_BLOCK_ELEMS = 2 * MIB // 4


def _conv(p, conv_w9, conv_b, nb, seq, width, cn):
    cn = min(cn, XBC)
    col0 = P_XBC // cn
    return pl.pallas_call(
        functools.partial(_conv_kernel, seq=seq, width=width),
        grid=(nb, XBC // cn),
        in_specs=[pl.BlockSpec((seq, cn), lambda b, j: (b, col0 + j)),
                  pl.BlockSpec((9, cn), lambda b, j: (0, j)),
                  pl.BlockSpec((1, cn), lambda b, j: (0, j))],
        out_specs=pl.BlockSpec((seq, cn), lambda b, j: (b, j)),
        out_shape=jax.ShapeDtypeStruct((nb * seq, XBC), F32),
        compiler_params=_params(("parallel", "parallel")),
        name="grid_conv",
    )(p, conv_w9, conv_b)


def _tri_masks():
    r = lax.broadcasted_iota(jnp.int32, (CHUNK, CHUNK), 0)
    c = lax.broadcasted_iota(jnp.int32, (CHUNK, CHUNK), 1)
    return r >= c, r <= c


SCAN_CHUNKS = 8


def _scan_kernel(gc_ref, bias_ref, alog_ref, col_ref, bcol_ref, row_ref):
    lower, upper = _tri_masks()
    lo, up = lower.astype(F32), upper.astype(F32)
    hi = lax.Precision.HIGHEST
    lane = lax.broadcasted_iota(jnp.int32, (CHUNK, GATE_W), 1)
    time = lax.broadcasted_iota(jnp.int32, (CHUNK, GATE_W), 0)
    lane1 = lax.broadcasted_iota(jnp.int32, (1, GATE_W), 1)

    def backward(l):
        unit_bwd = (l < G_DT) & (jnp.bitwise_and(l, N_UNITS - 1) >= M_HEADS)
        return unit_bwd | ((l >= G_DT + S_HEADS) & (l < N_GATES))

    is_bwd, is_bwd1 = backward(lane), backward(lane1)
    is_i = lane < G_F
    is_f = (lane >= G_F) & (lane < G_DT)
    is_dt = (lane >= G_DT) & (lane < N_GATES)
    neg_a = -jnp.exp(alog_ref[...])
    for c in range(SCAN_CHUNKS):
        rows = slice(c * CHUNK, (c + 1) * CHUNK)
        g = gc_ref[rows, :] + bias_ref[...]
        soft = jnp.log1p(jnp.exp(-jnp.abs(g)))
        logf = jnp.minimum(g, 0.0) - soft
        dt = jnp.maximum(g, 0.0) + soft
        x = jnp.where(is_f, logf, jnp.where(is_dt, dt * neg_a, 0.0))
        cs = jnp.where(is_bwd, _dot(up, x, precision=hi), _dot(lo, x, precision=hi))
        b_units = pltpu.roll(cs, GATE_W - G_F, 1)
        u = g - b_units
        cu = u
        k = 1
        while k < CHUNK:
            prev = jnp.where(time >= k, pltpu.roll(cu, k, 0), -jnp.inf)
            nxt = jnp.where(time < CHUNK - k, pltpu.roll(cu, CHUNK - k, 0), -jnp.inf)
            cu = jnp.maximum(cu, jnp.where(is_bwd, nxt, prev))
            k *= 2
        total = jnp.where(is_bwd1, cs[0:1, :], cs[CHUNK - 1:CHUNK, :])
        wk = jnp.exp(total - cs) * dt
        q = cs - jnp.log(dt)
        col_ref[rows, :] = jnp.where(is_i, cu, cs)
        bcol_ref[rows, :] = b_units
        tail = jnp.where(lane < G_CU, pltpu.roll(wk, G_WK - G_DT, 1),
                         jnp.where(lane < G_B, pltpu.roll(cu, G_CU, 1), pltpu.roll(b_units, G_B, 1)))
        row_ref[rows, :] = jnp.where(is_i, u, jnp.where(is_dt, q, tail)).T


def _gate_scans(gc, bias, alog):
    t = gc.shape[0]
    tm = SCAN_CHUNKS * CHUNK
    row = lambda i: (i, 0)
    const = lambda i: (0, 0)
    return pl.pallas_call(
        _scan_kernel,
        grid=(t // tm,),
        in_specs=[pl.BlockSpec((tm, GATE_W), row), pl.BlockSpec((1, GATE_W), const),
                  pl.BlockSpec((1, GATE_W), const)],
        out_specs=[pl.BlockSpec((tm, GATE_W), row)] * 3,
        out_shape=[jax.ShapeDtypeStruct((t, GATE_W), F32)] * 3,
        compiler_params=_params(("parallel",)),
        name="gate_scans",
    )(gc, bias, alog)


N_ROWS = 16
_WORK, _DONE = "work", "done"


def _mlstm_phases(refs, nc, zero_init):
    (qf, kf, vf, of, colf, bcolf, rowf, qb, kb, vb, ob, colb, bcolb, rowb, gn) = refs[:15]
    pos = 15
    if not zero_init:
        c0, n0, m0 = refs[pos:pos + 3]
        pos += 3
    hm_out, c_out, n_out, m_out, c_s, n_s, m_s, hpart, hbuf = refs[pos:]
    s = pl.program_id(1)
    half = nc // 2

    @pl.when(s == 0)
    def _():
        hpart[...] = jnp.zeros_like(hpart)
        if zero_init:
            c_s[...] = jnp.zeros_like(c_s)
            n_s[...] = jnp.zeros_like(n_s)
            m_s[...] = jnp.zeros_like(m_s)
        else:
            c_s[...] = c0[...]
            m_s[...] = m0[...]
            for u in range(N_UNITS):
                n_s[u] = jnp.broadcast_to(n0[u:u + 1, :], (N_ROWS, M_DQK))

    yield _DONE
    lower, upper = _tri_masks()
    lane1 = lax.broadcasted_iota(jnp.int32, (1, LANES), 1)
    ones_rows = jnp.ones((N_ROWS, CHUNK), BF16)
    per_dir = ((qf, kf, vf, of, colf, bcolf, rowf, lower, s),
               (qb, kb, vb, ob, colb, bcolb, rowb, upper, nc - 1 - s))
    m_prev = m_s[...]
    m_next = m_prev
    c_new, n_new = [], []
    for d, (q_ref, k_ref, v_ref, o_ref, col_ref, bcol_ref, row_ref, mask, chunk) in enumerate(per_dir):
        last = CHUNK - 1 if d == 0 else 0
        stab = jnp.maximum(m_prev, col_ref[...])
        w_c = jnp.exp(m_prev - stab)
        stab_last = stab[last:last + 1, :]
        m_new = bcol_ref[last:last + 1, :] + stab_last
        decay = w_c[last:last + 1, :]
        c_old = [c_s[d, h] for h in range(M_HEADS)]
        n_old = [n_s[d * M_HEADS + h] for h in range(M_HEADS)]
        for h in range(M_HEADS):
            u = d * M_HEADS + h
            stab_b = _lane_bcast(stab, u)
            u_row = row_ref[u:u + 1, :]
            p = jnp.exp(jnp.where(mask, u_row - stab_b, -jnp.inf))
            qh = q_ref[:, h * M_DQK:(h + 1) * M_DQK] * (M_DQK ** -0.5)
            k_t = k_ref[:, h * M_DQK:(h + 1) * M_DQK].T
            v16 = v_ref[:, h * M_DV:(h + 1) * M_DV].astype(BF16)
            sc = _dot(qh.astype(BF16), k_t.astype(BF16)) * p
            lhs = jnp.concatenate([sc.astype(BF16), (qh * _lane_bcast(w_c, u)).astype(BF16)], axis=1)
            num = _dot(lhs, jnp.concatenate([v16, c_old[h].astype(BF16)], axis=0))
            den = _dot(jnp.concatenate([ones_rows, n_old[h].astype(BF16)], axis=1), lhs, _NT)[0:1, :]
            m_t = row_ref[G_B + u:G_B + u + 1, :] + jnp.maximum(m_prev[:, u:u + 1],
                                                                row_ref[G_CU + u:G_CU + u + 1, :])
            inv_row = 1.0 / jnp.maximum(jnp.abs(den), jnp.exp(-m_t))
            inv = jnp.broadcast_to(inv_row, (CHUNK, LANES)).T
            hbuf[d, :, h * M_DV:(h + 1) * M_DV] = num * jnp.concatenate([inv, inv], axis=1)

            kw_t = (k_t * jnp.exp(u_row - stab_last[:, u:u + 1])).astype(BF16)
            dec = decay[:, u:u + 1]
            c_new.append(dec * c_old[h] + _dot(kw_t, v16))
            n_new.append(dec * n_old[h] + _dot(ones_rows, kw_t, _NT))
            yield _WORK
        mine = (lane1 >= d * M_HEADS) & (lane1 < (d + 1) * M_HEADS)
        m_next = jnp.where(mine, m_new, m_next)
    for u in range(N_UNITS):
        c_s[u // M_HEADS, u % M_HEADS] = c_new[u]
        n_s[u] = n_new[u]
    m_s[...] = m_next

    yield _DONE
    for d, (_, _, _, o_ref, _, _, _, _, chunk) in enumerate(per_dir):
        rows = pl.ds(pl.multiple_of(chunk * CHUNK, CHUNK), CHUNK)
        for h in range(M_HEADS):
            cols = slice(h * M_DV, (h + 1) * M_DV)
            tot = hbuf[d, :, cols] + hpart[rows, cols]
            hm_out[rows, cols] = (_rms(tot) * gn[:, cols] * o_ref[:, cols]).astype(BF16)
        hpart[rows, :] = hbuf[d]

    @pl.when(s == nc - 1)
    def _():
        c_out[...] = c_s[...]
        m_out[...] = m_s[...]
        for u in range(N_UNITS):
            n_out[u:u + 1, :] = n_s[u][0:1, :]

    yield _DONE


def _sequence_out_spec(seq, width):
    mode = dict(pipeline_mode=pl.Buffered(1)) if seq * width * 2 >= 2 * MIB else {}
    return pl.BlockSpec((None, seq, width), lambda b, s: (b, 0, 0), **mode)


def _mlstm_specs(p, scans, g_norm, state, nb, nc):
    zero_init = state is None
    seq = nc * CHUNK
    col, bcol, row = scans

    def fwd(cb):
        return lambda b, s: (b * nc + s, cb)

    def bwd(cb):
        return lambda b, s: (b * nc + nc - 1 - s, cb)

    def stream(mk):
        return [pl.BlockSpec((CHUNK, M_HEADS * M_DQK), mk(P_Q // (M_HEADS * M_DQK))),
                pl.BlockSpec((CHUNK, M_HEADS * M_DQK), mk(P_K // (M_HEADS * M_DQK))),
                pl.BlockSpec((CHUNK, MIX_M), mk(P_V // MIX_M)),
                pl.BlockSpec((CHUNK, MIX_M), mk(P_O // MIX_M)),
                pl.BlockSpec((CHUNK, GATE_W), mk(0)),
                pl.BlockSpec((CHUNK, GATE_W), mk(0)),
                pl.BlockSpec((CHUNK, GATE_W), mk(0))]

    in_specs = stream(fwd) + stream(bwd) + [pl.BlockSpec((1, MIX_M), lambda b, s: (0, 0))]
    args = [p] * 4 + [col, bcol, row] + [p] * 4 + [col, bcol, row, g_norm]
    state_specs = [pl.BlockSpec((None, N_DIR, M_HEADS, M_DQK, M_DV), lambda b, s: (b, 0, 0, 0, 0)),
                   pl.BlockSpec((None, N_UNITS, M_DQK), lambda b, s: (b, 0, 0)),
                   pl.BlockSpec((None, 1, LANES), lambda b, s: (b, 0, 0))]
    if not zero_init:
        in_specs += state_specs
        args += list(state)
    out_specs = [_sequence_out_spec(seq, MIX_M)] + state_specs
    out_shape = [jax.ShapeDtypeStruct((nb, seq, MIX_M), BF16),
                 jax.ShapeDtypeStruct((nb, N_DIR, M_HEADS, M_DQK, M_DV), F32),
                 jax.ShapeDtypeStruct((nb, N_UNITS, M_DQK), F32),
                 jax.ShapeDtypeStruct((nb, 1, LANES), F32)]
    scratch = [pltpu.VMEM((N_DIR, M_HEADS, M_DQK, M_DV), F32),
               pltpu.VMEM((N_UNITS, N_ROWS, M_DQK), F32),
               pltpu.VMEM((1, LANES), F32),
               pltpu.VMEM((seq, MIX_M), F32),
               pltpu.VMEM((N_DIR, CHUNK, MIX_M), F32)]
    return in_specs, args, out_specs, out_shape, scratch


def _ssd_phases(refs, nc, zero_init):
    (xf, bf, cf, zf, colf, rowf, xb, bb, cb, zb, colb, rowb, dskip, gn) = refs[:14]
    pos = 14
    if not zero_init:
        s0 = refs[pos]
        pos += 1
    ys_out, s_out, st_s, ypart, ybuf = refs[pos:]
    s = pl.program_id(1)
    half = nc // 2

    @pl.when(s == 0)
    def _():
        ypart[...] = jnp.zeros_like(ypart)
        if zero_init:
            st_s[...] = jnp.zeros_like(st_s)
        else:
            for d in range(N_DIR):
                for pr in range(S_PAIRS):
                    st_s[d, pr] = s0[d, pr].T

    yield _DONE
    lower, upper = _tri_masks()
    low_half = lax.broadcasted_iota(jnp.int32, (CHUNK, LANES), 1) < S_HEADDIM
    low_half1 = lax.broadcasted_iota(jnp.int32, (1, LANES), 1) < S_HEADDIM
    per_dir = ((xf, bf, cf, zf, colf, rowf, lower, s), (xb, bb, cb, zb, colb, rowb, upper, nc - 1 - s))
    st_new = []
    for d, (x_ref, b_ref, c_ref, z_ref, col_ref, row_ref, mask, chunk) in enumerate(per_dir):
        last = CHUNK - 1 if d == 0 else 0
        cs_c = col_ref[...]
        total = cs_c[last:last + 1, :]
        st_old = [st_s[d, pr] for pr in range(S_PAIRS)]
        for g in range(S_GROUPS):
            gs = slice(g * S_STATE, (g + 1) * S_STATE)
            cg = c_ref[:, gs].astype(BF16)
            b_f32 = b_ref[:, gs]
            cbm = _dot(cg, b_f32.astype(BF16), _NT)
            b_t = b_f32.T
            for pg in range(PAIRS_PER_GROUP):
                pr = g * PAIRS_PER_GROUP + pg
                cols = slice(pr * LANES, (pr + 1) * LANES)
                x16 = x_ref[:, cols].astype(BF16)
                zero16 = jnp.zeros_like(x16)
                halves = (jnp.where(low_half, x16, zero16), jnp.where(low_half, zero16, x16))
                state = st_old[pr]
                y_acc = jnp.zeros((CHUNK, LANES), F32)
                s_acc = jnp.zeros((S_STATE, LANES), F32)
                cs_b = []
                for e in range(PAIR):
                    gi = G_DT + d * S_HEADS + pr * PAIR + e
                    cs_b.append(_lane_bcast(cs_c, gi))
                    q_row = row_ref[gi:gi + 1, :]
                    wk_row = row_ref[gi + G_WK - G_DT:gi + G_WK - G_DT + 1, :]
                    mix = cbm * jnp.exp(jnp.where(mask, cs_b[e] - q_row, -jnp.inf))
                    y_acc = y_acc + _dot(mix.astype(BF16), halves[e])
                    s_acc = s_acc + _dot((b_t * wk_row).astype(BF16), halves[e])
                carry = jnp.exp(jnp.where(low_half, cs_b[0], cs_b[1]))
                ybuf[d, :, cols] = y_acc + _dot(cg, state.astype(BF16)) * carry
                gi0 = G_DT + d * S_HEADS + pr * PAIR
                dec = jnp.exp(jnp.where(low_half1, jnp.broadcast_to(total[:, gi0:gi0 + 1], (1, LANES)),
                                        jnp.broadcast_to(total[:, gi0 + 1:gi0 + 2], (1, LANES))))
                st_new.append(state * dec + s_acc)
            yield _WORK
    for d in range(N_DIR):
        for pr in range(S_PAIRS):
            st_s[d, pr] = st_new[d * S_PAIRS + pr]

    yield _DONE
    for d, (x_ref, _, _, z_ref, _, _, _, chunk) in enumerate(per_dir):
        rows = pl.ds(pl.multiple_of(chunk * CHUNK, CHUNK), CHUNK)
        y = ybuf[d] + ypart[rows, :] + dskip[...] * x_ref[...]
        ys_out[rows, :] = (_rms(y * z_ref[...]) * gn[...]).astype(BF16)
        ypart[rows, :] = ybuf[d]

    @pl.when(s == nc - 1)
    def _():
        for d in range(N_DIR):
            for pr in range(S_PAIRS):
                s_out[d, pr] = st_s[d, pr].T

    yield _DONE


def _ssd_specs(p, xbc, xbc_col, scans, dskip, g_norm, state, nb, nc):
    zero_init = state is None
    seq = nc * CHUNK
    bc_w = S_GROUPS * S_STATE
    assert xbc_col % MIX_S == 0
    col, _, row = scans

    def fwd(cb):
        return lambda b, s: (b * nc + s, cb)

    def bwd(cb):
        return lambda b, s: (b * nc + nc - 1 - s, cb)

    def stream(mk):
        return [pl.BlockSpec((CHUNK, MIX_S), mk(xbc_col // MIX_S)),
                pl.BlockSpec((CHUNK, bc_w), mk((xbc_col + MIX_S) // bc_w)),
                pl.BlockSpec((CHUNK, bc_w), mk((xbc_col + MIX_S) // bc_w + 1)),
                pl.BlockSpec((CHUNK, MIX_S), mk(P_Z // MIX_S)),
                pl.BlockSpec((CHUNK, GATE_W), mk(0)),
                pl.BlockSpec((CHUNK, GATE_W), mk(0))]

    const = lambda b, s: (0, 0)
    in_specs = stream(fwd) + stream(bwd) + [pl.BlockSpec((1, MIX_S), const), pl.BlockSpec((1, MIX_S), const)]
    args = [xbc, xbc, xbc, p, col, row] * 2 + [dskip, g_norm]
    state_spec = pl.BlockSpec((None, N_DIR, S_PAIRS, LANES, S_STATE), lambda b, s: (b, 0, 0, 0, 0))
    if not zero_init:
        in_specs.append(state_spec)
        args.append(state)
    out_specs = [_sequence_out_spec(seq, MIX_S), state_spec]
    out_shape = [jax.ShapeDtypeStruct((nb, seq, MIX_S), BF16),
                 jax.ShapeDtypeStruct((nb, N_DIR, S_PAIRS, LANES, S_STATE), F32)]
    scratch = [pltpu.VMEM((N_DIR, S_PAIRS, S_STATE, LANES), F32),
               pltpu.VMEM((seq, MIX_S), F32),
               pltpu.VMEM((N_DIR, CHUNK, MIX_S), F32)]
    return in_specs, args, out_specs, out_shape, scratch


def _mixer_kernel(*refs, nc, zero_init, counts):
    (m_in, s_in), (m_out, s_out), (m_scr, s_scr) = counts
    pos = 0
    parts = []
    for n in (m_in, s_in, m_out, s_out, m_scr, s_scr):
        parts.append(refs[pos:pos + n])
        pos += n
    gens = (_mlstm_phases(parts[0] + parts[2] + parts[4], nc, zero_init),
            _ssd_phases(parts[1] + parts[3] + parts[5], nc, zero_init))
    for _ in range(3):
        live = list(gens)
        while live:
            live = [g for g in live if next(g) is not _DONE]


def _mixers(p, xbc, xbc_col, scans, g_mlstm, dskip, g_ssd, mlstm_state, ssd_state, nb, nc):
    m = _mlstm_specs(p, scans, g_mlstm, mlstm_state, nb, nc)
    s = _ssd_specs(p, xbc, xbc_col, scans, dskip, g_ssd, ssd_state, nb, nc)
    counts = tuple((len(a), len(b)) for a, b in ((m[0], s[0]), (m[2], s[2]), (m[4], s[4])))
    return pl.pallas_call(
        functools.partial(_mixer_kernel, nc=nc, zero_init=mlstm_state is None, counts=counts),
        grid=(nb, nc),
        in_specs=m[0] + s[0],
        out_specs=m[2] + s[2],
        out_shape=m[3] + s[3],
        scratch_shapes=m[4] + s[4],
        compiler_params=_params(("parallel", "arbitrary"), VMEM_LIMIT_LARGE),
        name="mixers",
    )(*(m[1] + s[1]))


OUT_TM = 512


OUT_SUB = 256


def _outproj_kernel(hm_ref, ys_ref, w_ref, x_ref, g1_ref, gpost_ref, gpre_ref, sc_ref, sh_ref, x1_ref, u2_ref):
    for r in range(OUT_TM // OUT_SUB):
        rows = slice(r * OUT_SUB, (r + 1) * OUT_SUB)
        mix = _dot(jnp.concatenate([hm_ref[rows, :], ys_ref[rows, :]], axis=1), w_ref[...])
        x1 = x_ref[rows, :] + g1_ref[...] * (_rms(mix) * gpost_ref[...])
        x1_ref[rows, :] = x1
        u2_ref[rows, :] = (_rms(x1) * gpre_ref[...] * (1.0 + sc_ref[...]) + sh_ref[...]).astype(BF16)


def _out_proj(hm, ys, w_out16, x2d, g1, gpost, gpre, sc2, sh2, rows_per_mod):
    t = x2d.shape[0]
    tiles_per_mod = rows_per_mod // OUT_TM
    row = lambda i: (i, 0)
    const = lambda i: (0, 0)
    mod = pl.BlockSpec((None, 1, D_MODEL), lambda i: (i // tiles_per_mod, 0, 0))
    return pl.pallas_call(
        _outproj_kernel,
        grid=(t // OUT_TM,),
        in_specs=[pl.BlockSpec((OUT_TM, MIX_M), row), pl.BlockSpec((OUT_TM, MIX_S), row),
                  pl.BlockSpec((D_MODEL, D_MODEL), const), pl.BlockSpec((OUT_TM, D_MODEL), row),
                  mod, pl.BlockSpec((1, D_MODEL), const), pl.BlockSpec((1, D_MODEL), const), mod, mod],
        out_specs=[pl.BlockSpec((OUT_TM, D_MODEL), row), pl.BlockSpec((OUT_TM, D_MODEL), row)],
        out_shape=[jax.ShapeDtypeStruct((t, D_MODEL), F32), jax.ShapeDtypeStruct((t, D_MODEL), BF16)],
        compiler_params=_params(("parallel",)),
        name="out_proj",
    )(hm, ys, w_out16, x2d, g1, gpost, gpre, sc2, sh2)


MLP_TM = 1024
MLP_TH = 512
MLP_SUB = 512


def _mlp_kernel(u_ref, w1_ref, w2_ref, x1_hbm, g2_ref, gpost_ref, o_ref, x1_buf, x1_sem):
    i = pl.program_id(0)
    j = pl.program_id(1)

    def x1_copy():
        rows = pl.ds(pl.multiple_of(i * MLP_TM, MLP_TM), MLP_TM)
        return pltpu.make_async_copy(x1_hbm.at[rows, :], x1_buf, x1_sem)

    def partial_sum(rows, w1, w2):
        hid = jnp.square(jnp.maximum(_dot(u_ref[rows, :], w1), 0.0)).astype(BF16)
        return _dot(hid, w2)

    sub_blocks = [slice(r * MLP_SUB, (r + 1) * MLP_SUB) for r in range(MLP_TM // MLP_SUB)]
    last = pl.num_programs(1) - 1

    @pl.when(j == 0)
    def _():
        x1_copy().start()
        w1, w2 = w1_ref[...].astype(BF16), w2_ref[...].astype(BF16)
        for rows in sub_blocks:
            o_ref[rows, :] = partial_sum(rows, w1, w2)

    @pl.when((j > 0) & (j < last))
    def _():
        w1, w2 = w1_ref[...].astype(BF16), w2_ref[...].astype(BF16)
        for rows in sub_blocks:
            o_ref[rows, :] += partial_sum(rows, w1, w2)

    @pl.when(j == last)
    def _():
        x1_copy().wait()
        w1, w2 = w1_ref[...].astype(BF16), w2_ref[...].astype(BF16)
        for rows in sub_blocks:
            total = o_ref[rows, :] + partial_sum(rows, w1, w2)
            o_ref[rows, :] = x1_buf[rows, :] + g2_ref[...] * (_rms(total) * gpost_ref[...])


def _mlp(u2, w1, w2, x1, g2, gpost, rows_per_mod):
    t = u2.shape[0]
    tiles_per_mod = rows_per_mod // MLP_TM
    assert D_FF // MLP_TH >= 2
    return pl.pallas_call(
        _mlp_kernel,
        grid=(t // MLP_TM, D_FF // MLP_TH),
        in_specs=[pl.BlockSpec((MLP_TM, D_MODEL), lambda i, j: (i, 0)),
                  pl.BlockSpec((D_MODEL, MLP_TH), lambda i, j: (0, j)),
                  pl.BlockSpec((MLP_TH, D_MODEL), lambda i, j: (j, 0)),
                  pl.BlockSpec(memory_space=pl.ANY),
                  pl.BlockSpec((None, 1, D_MODEL), lambda i, j: (i // tiles_per_mod, 0, 0)),
                  pl.BlockSpec((1, D_MODEL), lambda i, j: (0, 0))],
        out_specs=pl.BlockSpec((MLP_TM, D_MODEL), lambda i, j: (i, 0)),
        out_shape=jax.ShapeDtypeStruct((t, D_MODEL), F32),
        scratch_shapes=[pltpu.VMEM((MLP_TM, D_MODEL), F32), pltpu.SemaphoreType.DMA(())],
        compiler_params=_params(("parallel", "arbitrary"), VMEM_LIMIT_LARGE),
        name="mlp",
    )(u2, w1, w2, x1, g2, gpost)


def _gate_row(i_vals, f_vals, dt_vals):
    v = jnp.concatenate([i_vals.reshape(-1), f_vals.reshape(-1), dt_vals.reshape(-1)]).astype(F32)
    return jnp.pad(v, (0, GATE_W - N_GATES)).reshape(1, GATE_W)


def _block(x, mods, state, weights, width):
    nb, seq, _ = x.shape
    nc = seq // CHUNK
    t = nb * seq
    x2d = x.reshape(t, D_MODEL)
    sh1, sc1, g1, sh2, sc2, g2 = mods
    rows_per_mod = t // sh1.shape[0]
    w = weights

    fused_conv = width == seq and IN_TM % seq == 0
    conv = (w["conv_w9"], w["conv_b"], width) if fused_conv else None
    p, gc = _in_proj(x2d, sc1, sh1, w["g_pre_mix"], w["w_t"], w["wg"], rows_per_mod, conv)
    scans = _gate_scans(gc, w["gate_bias"], w["gate_alog"])
    if fused_conv:
        xbc, xbc_col = p, P_XBC
    else:
        xbc, xbc_col = _conv(p, w["conv_w9"], w["conv_b"], nb, seq, width, CONV_BLOCK_ELEMS // seq), 0

    if state is None:
        m_state = s_state = None
    else:
        c0, n0, m0, s0 = state
        m_state = (c0, n0.reshape(nb, N_UNITS, M_DQK),
                   jnp.pad(m0.reshape(nb, 1, N_UNITS), ((0, 0), (0, 0), (0, LANES - N_UNITS))))
        s_state = s0.reshape(nb, N_DIR, S_PAIRS, LANES, S_STATE)
    hm, c_new, n_new, m_new, ys, s_new = _mixers(p, xbc, xbc_col, scans, w["g_mlstm_norm"], w["dskip"],
                                                 w["g_ssd_norm"], m_state, s_state, nb, nc)

    x1, u2 = _out_proj(hm.reshape(t, MIX_M), ys.reshape(t, MIX_S), w["w_out"], x2d, g1,
                       w["g_post_mix"], w["g_pre_mlp"], sc2, sh2, rows_per_mod)
    y = _mlp(u2, w["w_mlp_in"], w["w_mlp_out"], x1, g2, w["g_post_mlp"], rows_per_mod)
    new_state = (c_new.reshape(nb, 1, N_DIR, M_HEADS, M_DQK, M_DV),
                 n_new.reshape(nb, 1, N_DIR, M_HEADS, M_DQK),
                 m_new[:, 0, :N_UNITS].reshape(nb, 1, N_DIR, M_HEADS),
                 s_new.reshape(nb, 1, N_DIR, S_HEADS, S_HEADDIM, S_STATE))
    return y.reshape(nb, seq, D_MODEL), new_state


def kernel(x_prompt, x_sample, state_mlstm_c, state_mlstm_n, state_mlstm_m, state_ssd, c, c_ctx, w_mod, b_mod,
           g_pre_mix, g_post_mix, w_in, b_igate, b_fgate, conv_w, conv_b, dt_bias, a_log, d_skip, g_mlstm_norm,
           g_ssd_norm, w_out, g_pre_mlp, g_post_mlp, w_mlp_in, w_mlp_out):
    assert w_mod.shape[0] == 1, "one layer"
    nb_s = x_sample.shape[0]

    cond8 = jnp.zeros((8, D_MODEL), F32).at[0].set(c_ctx).at[1:1 + nb_s].set(c)
    mod = _modulation(cond8, w_mod[0], b_mod[0].reshape(1, -1))
    mods = [mod[:, k * D_MODEL:(k + 1) * D_MODEL] for k in range(6)]
    mods_p = [m[0:1].reshape(1, 1, D_MODEL) for m in mods]
    mods_s = [m[1:1 + nb_s].reshape(nb_s, 1, D_MODEL) for m in mods]

    w_t = w_in[0].T.astype(BF16)
    assert w_t.shape[0] == P_MAIN + N_GATES
    gate_rows = jnp.concatenate([w_t[P_Z:P_Z + G_DT], w_t[P_MAIN + G_DT:]], axis=0)
    gate_rows = jnp.pad(gate_rows, ((0, GATE_W - N_GATES), (0, 0)))
    zeros_u = jnp.zeros((N_UNITS,), F32)
    row = lambda v: v.reshape(1, -1)
    weights = dict(
        w_t=w_t, wg=gate_rows,
        g_pre_mix=row(g_pre_mix[0]), g_post_mix=row(g_post_mix[0]),
        g_pre_mlp=row(g_pre_mlp[0]), g_post_mlp=row(g_post_mlp[0]),
        conv_w9=conv_w[0].reshape(9, XBC), conv_b=row(conv_b[0]),
        gate_bias=_gate_row(b_igate[0], b_fgate[0], dt_bias[0]),
        gate_alog=_gate_row(zeros_u, zeros_u, a_log[0]),
        dskip=row(jnp.repeat(d_skip[0], S_HEADDIM)),
        g_mlstm_norm=row(g_mlstm_norm[0]), g_ssd_norm=row(g_ssd_norm[0]),
        w_out=w_out[0].astype(BF16), w_mlp_in=w_mlp_in[0], w_mlp_out=w_mlp_out[0])

    y_p, st = _block(x_prompt, mods_p, None, weights, x_prompt.shape[1])
    cache = (state_mlstm_c[:, 0], state_mlstm_n[:, 0], state_mlstm_m[:, 0], state_ssd[:, 0])
    y_s, _ = _block(x_sample, mods_s, cache, weights, GRID_W)
    return (y_p, y_s) + st
```

```python
import functools

import jax
import jax.numpy as jnp
from jax import lax
from jax.experimental import pallas as pl
from jax.experimental.pallas import tpu as pltpu

F32 = jnp.float32
BF16 = jnp.bfloat16

D_MODEL = 2048
CHUNK = 128
N_DIR = 2
M_HEADS = 4
M_DQK = 128
M_DV = 256
MIX_M = M_HEADS * M_DV
S_HEADS = 16
S_HEADDIM = 64
S_STATE = 128
S_GROUPS = 4
S_REP = S_HEADS // S_GROUPS
MIX_S = S_HEADS * S_HEADDIM
XBC = MIX_S + 2 * S_GROUPS * S_STATE
D_FF = 4 * D_MODEL
GRID_W = 64
EPS = 1e-6
LANES = 128

P_Q = 0
P_K = M_HEADS * M_DQK
P_V = 2 * M_HEADS * M_DQK
P_O = P_V + MIX_M
P_Z = P_O + MIX_M
P_XBC = P_Z + MIX_S
P_MAIN = P_XBC + XBC
GATE_W = LANES
N_UNITS = N_DIR * M_HEADS
G_I = 0
G_F = N_UNITS
G_DT = 2 * N_UNITS
N_GATES = G_DT + N_DIR * S_HEADS
G_WK = G_DT + N_DIR * S_HEADS
G_CU = G_WK + N_DIR * S_HEADS
G_B = G_CU + N_UNITS
assert G_B + N_UNITS <= GATE_W

PAIR = LANES // S_HEADDIM
S_PAIRS = S_HEADS // PAIR
PAIRS_PER_GROUP = S_REP // PAIR

MIB = 1024 * 1024
VMEM_LIMIT = 48 * MIB
VMEM_LIMIT_LARGE = 60 * MIB

_NT = (((1,), (1,)), ((), ()))


def _params(sem, limit=VMEM_LIMIT):
    return pltpu.CompilerParams(dimension_semantics=sem, vmem_limit_bytes=limit)


def _silu(x):
    return x / (1.0 + jnp.exp(-x))


def _sigmoid(x):
    return 1.0 / (1.0 + jnp.exp(-x))


def _rms(x):
    return x * lax.rsqrt(jnp.mean(x * x, axis=-1, keepdims=True) + EPS)


def _dot(a, b, dims=None, precision=None):
    if dims is None:
        dims = (((a.ndim - 1,), (0,)), ((), ()))
    return lax.dot_general(a, b, dims, precision=precision, preferred_element_type=F32)


def _lane_bcast(tile, lane):
    return jnp.broadcast_to(tile[:, lane:lane + 1], tile.shape)


def _mod_kernel(c_ref, w_ref, b_ref, o_ref):
    a = _silu(c_ref[...]).astype(BF16)
    o_ref[...] = _dot(a, w_ref[...].astype(BF16)) + b_ref[...]


def _modulation(cond8, w_mod, b_mod):
    n = w_mod.shape[1]
    tn = 1024
    return pl.pallas_call(
        _mod_kernel,
        grid=(n // tn,),
        in_specs=[pl.BlockSpec((8, D_MODEL), lambda j: (0, 0)),
                  pl.BlockSpec((D_MODEL, tn), lambda j: (0, j)),
                  pl.BlockSpec((1, tn), lambda j: (0, j))],
        out_specs=pl.BlockSpec((8, tn), lambda j: (0, j)),
        out_shape=jax.ShapeDtypeStruct((8, n), F32),
        compiler_params=_params(("parallel",)),
        name="modulation",
    )(cond8, w_mod, b_mod)


IN_TM = 1024
IN_TN = 1024
IN_SUB = 256


def _grid_conv(x, w_ref, b_ref, seq, width):
    n = x.shape[0]
    assert n % seq == 0 and seq % width == 0 and seq & (seq - 1) == 0 and width & (width - 1) == 0
    t = lax.broadcasted_iota(jnp.int32, x.shape, 0)
    c = jnp.bitwise_and(t, width - 1)
    xl = jnp.where(c >= 1, pltpu.roll(x, 1, 0), 0.0)
    xr = jnp.where(c <= width - 2, pltpu.roll(x, n - 1, 0), 0.0)

    def taps(di):
        return w_ref[3 * di:3 * di + 1, :] * xl + w_ref[3 * di + 1:3 * di + 2, :] * x \
            + w_ref[3 * di + 2:3 * di + 3, :] * xr

    out = taps(1) + b_ref[...]
    if seq > width:
        r = jnp.bitwise_and(t, seq - 1)
        out = out + jnp.where(r >= width, pltpu.roll(taps(0), width, 0), 0.0)
        out = out + jnp.where(r < seq - width, pltpu.roll(taps(2), n - width, 0), 0.0)
    return _silu(out)


def _inproj_kernel(x_ref, sc_ref, sh_ref, g_ref, w_ref, wg_ref, *rest, conv_width):
    if conv_width is None:
        o_ref, gc_ref, u_ref = rest
    else:
        cw_ref, cb_ref, o_ref, gc_ref, u_ref = rest
    j = pl.program_id(1)

    j_o, j_z = P_O // IN_TN, P_Z // IN_TN
    j_x = P_XBC // IN_TN
    plain = (j > 0) & (j != j_o) & (j != j_z)
    if conv_width is not None:
        plain = plain & (j < j_x)

        @pl.when(j >= j_x)
        def _():
            o_ref[...] = _grid_conv(_dot(u_ref[...], w_ref[...], _NT), cw_ref, cb_ref, conv_width, conv_width)

    @pl.when(j == 0)
    def _():
        for r in range(IN_TM // IN_SUB):
            rows = slice(r * IN_SUB, (r + 1) * IN_SUB)
            y = _rms(x_ref[rows, :]) * g_ref[...]
            u = (y * (1.0 + sc_ref[...]) + sh_ref[...]).astype(BF16)
            u_ref[rows, :] = u
            o_ref[rows, :] = _dot(u, w_ref[...], _NT)

    @pl.when(plain)
    def _():
        o_ref[...] = _dot(u_ref[...], w_ref[...], _NT)

    @pl.when((j == j_o) | (j == j_z))
    def _():
        acc = _dot(u_ref[...], w_ref[...], _NT)
        sig = _sigmoid(acc)
        o_ref[...] = jnp.where(j == j_o, sig, acc * sig)

    @pl.when(j == pl.num_programs(1) - 1)
    def _():
        gc_ref[...] = _dot(u_ref[...], wg_ref[...], _NT)


def _in_proj(x2d, sc, sh, g, w_t, wg, rows_per_mod, conv=None):
    t = x2d.shape[0]
    tiles_per_mod = rows_per_mod // IN_TM
    n_a = P_Z // IN_TN
    j_x = P_XBC // IN_TN
    assert P_O % IN_TN == 0 and MIX_M == IN_TN and MIX_S == IN_TN

    def first_row(i, j):
        return (j * (IN_TN // G_DT) + (j >= n_a).astype(jnp.int32)) * G_DT, 0

    conv_specs, conv_args, conv_width = [], [], None
    if conv is not None:
        conv_w9, conv_b, conv_width = conv
        assert IN_TM % conv_width == 0 and P_XBC % IN_TN == 0
        conv_specs = [pl.BlockSpec((9, IN_TN), lambda i, j: (0, jnp.maximum(j - j_x, 0))),
                      pl.BlockSpec((1, IN_TN), lambda i, j: (0, jnp.maximum(j - j_x, 0)))]
        conv_args = [conv_w9, conv_b]
    return pl.pallas_call(
        functools.partial(_inproj_kernel, conv_width=conv_width),
        grid=(t // IN_TM, P_MAIN // IN_TN),
        in_specs=[pl.BlockSpec((IN_TM, D_MODEL), lambda i, j: (i, 0)),
                  pl.BlockSpec((None, 1, D_MODEL), lambda i, j: (i // tiles_per_mod, 0, 0)),
                  pl.BlockSpec((None, 1, D_MODEL), lambda i, j: (i // tiles_per_mod, 0, 0)),
                  pl.BlockSpec((1, D_MODEL), lambda i, j: (0, 0)),
                  pl.BlockSpec((pl.Element(IN_TN), pl.Element(D_MODEL)), first_row),
                  pl.BlockSpec((GATE_W, D_MODEL), lambda i, j: (0, 0))] + conv_specs,
        out_specs=[pl.BlockSpec((IN_TM, IN_TN), lambda i, j: (i, j)),
                   pl.BlockSpec((IN_TM, GATE_W), lambda i, j: (i, 0))],
        out_shape=[jax.ShapeDtypeStruct((t, P_MAIN), F32),
                   jax.ShapeDtypeStruct((t, GATE_W), F32)],
        scratch_shapes=[pltpu.VMEM((IN_TM, D_MODEL), BF16)],
        compiler_params=_params(("parallel", "arbitrary"), VMEM_LIMIT_LARGE),
        name="in_proj",
    )(x2d, sc, sh, g, w_t, wg, *conv_args)


def _conv_kernel(x_ref, w_ref, b_ref, o_ref, *, seq, width):
    o_ref[...] = _grid_conv(x_ref[...], w_ref, b_ref, seq, width)


CONV_BLOCK_ELEMS = 2 * MIB // 4


def _conv(p, conv_w9, conv_b, nb, seq, width, cn):
    cn = min(cn, XBC)
    col0 = P_XBC // cn
    return pl.pallas_call(
        functools.partial(_conv_kernel, seq=seq, width=width),
        grid=(nb, XBC // cn),
        in_specs=[pl.BlockSpec((seq, cn), lambda b, j: (b, col0 + j)),
                  pl.BlockSpec((9, cn), lambda b, j: (0, j)),
                  pl.BlockSpec((1, cn), lambda b, j: (0, j))],
        out_specs=pl.BlockSpec((seq, cn), lambda b, j: (b, j)),
        out_shape=jax.ShapeDtypeStruct((nb * seq, XBC), F32),
        compiler_params=_params(("parallel", "parallel")),
        name="grid_conv",
    )(p, conv_w9, conv_b)


def _tri_masks():
    r = lax.broadcasted_iota(jnp.int32, (CHUNK, CHUNK), 0)
    c = lax.broadcasted_iota(jnp.int32, (CHUNK, CHUNK), 1)
    return r >= c, r <= c


SCAN_CHUNKS = 8


def _scan_kernel(gc_ref, bias_ref, alog_ref, col_ref, bcol_ref, row_ref):
    lower, upper = _tri_masks()
    lo, up = lower.astype(F32), upper.astype(F32)
    hi = lax.Precision.HIGHEST
    lane = lax.broadcasted_iota(jnp.int32, (CHUNK, GATE_W), 1)
    time = lax.broadcasted_iota(jnp.int32, (CHUNK, GATE_W), 0)
    lane1 = lax.broadcasted_iota(jnp.int32, (1, GATE_W), 1)

    def backward(l):
        unit_bwd = (l < G_DT) & (jnp.bitwise_and(l, N_UNITS - 1) >= M_HEADS)
        return unit_bwd | ((l >= G_DT + S_HEADS) & (l < N_GATES))

    is_bwd, is_bwd1 = backward(lane), backward(lane1)
    is_i = lane < G_F
    is_f = (lane >= G_F) & (lane < G_DT)
    is_dt = (lane >= G_DT) & (lane < N_GATES)
    neg_a = -jnp.exp(alog_ref[...])
    for c in range(SCAN_CHUNKS):
        rows = slice(c * CHUNK, (c + 1) * CHUNK)
        g = gc_ref[rows, :] + bias_ref[...]
        soft = jnp.log1p(jnp.exp(-jnp.abs(g)))
        logf = jnp.minimum(g, 0.0) - soft
        dt = jnp.maximum(g, 0.0) + soft
        x = jnp.where(is_f, logf, jnp.where(is_dt, dt * neg_a, 0.0))
        cs = jnp.where(is_bwd, _dot(up, x, precision=hi), _dot(lo, x, precision=hi))
        b_units = pltpu.roll(cs, GATE_W - G_F, 1)
        u = g - b_units
        cu = u
        k = 1
        while k < CHUNK:
            prev = jnp.where(time >= k, pltpu.roll(cu, k, 0), -jnp.inf)
            nxt = jnp.where(time < CHUNK - k, pltpu.roll(cu, CHUNK - k, 0), -jnp.inf)
            cu = jnp.maximum(cu, jnp.where(is_bwd, nxt, prev))
            k *= 2
        total = jnp.where(is_bwd1, cs[0:1, :], cs[CHUNK - 1:CHUNK, :])
        wk = jnp.exp(total - cs) * dt
        q = cs - jnp.log(dt)
        col_ref[rows, :] = jnp.where(is_i, cu, cs)
        bcol_ref[rows, :] = b_units
        tail = jnp.where(lane < G_CU, pltpu.roll(wk, G_WK - G_DT, 1),
                         jnp.where(lane < G_B, pltpu.roll(cu, G_CU, 1), pltpu.roll(b_units, G_B, 1)))
        row_ref[rows, :] = jnp.where(is_i, u, jnp.where(is_dt, q, tail)).T


def _gate_scans(gc, bias, alog):
    t = gc.shape[0]
    tm = SCAN_CHUNKS * CHUNK
    row = lambda i: (i, 0)
    const = lambda i: (0, 0)
    return pl.pallas_call(
        _scan_kernel,
        grid=(t // tm,),
        in_specs=[pl.BlockSpec((tm, GATE_W), row), pl.BlockSpec((1, GATE_W), const),
                  pl.BlockSpec((1, GATE_W), const)],
        out_specs=[pl.BlockSpec((tm, GATE_W), row)] * 3,
        out_shape=[jax.ShapeDtypeStruct((t, GATE_W), F32)] * 3,
        compiler_params=_params(("parallel",)),
        name="gate_scans",
    )(gc, bias, alog)


def _chunk_rows(chunk):
    if isinstance(chunk, int):
        return slice(chunk * CHUNK, (chunk + 1) * CHUNK)
    return pl.ds(pl.multiple_of(chunk * CHUNK, CHUNK), CHUNK)


def _finalise(s, nc, complete, park):
    half = nc // 2
    if isinstance(s, int):
        for d in range(N_DIR):
            (park if s < half else complete)(d)
        return

    @pl.when(s < half)
    def _():
        for d in range(N_DIR):
            park(d)

    @pl.when(s >= half)
    def _():
        for d in range(N_DIR):
            complete(d)


def _when(cond):
    if isinstance(cond, bool):
        return (lambda f: f()) if cond else (lambda f: None)
    return pl.when(cond)


N_ROWS = 16
_WORK, _DONE = "work", "done"


def _mlstm_phases(refs, nc, zero_init, s):
    (qf, kf, vf, of, colf, bcolf, rowf, qb, kb, vb, ob, colb, bcolb, rowb, gn) = refs[:15]
    pos = 15
    if not zero_init:
        c0, n0, m0 = refs[pos:pos + 3]
        pos += 3
    hm_out, c_out, n_out, m_out, c_s, n_s, m_s, hpart, hbuf = refs[pos:]

    @_when(s == 0)
    def _():
        if zero_init:
            c_s[...] = jnp.zeros_like(c_s)
            n_s[...] = jnp.zeros_like(n_s)
            m_s[...] = jnp.zeros_like(m_s)
        else:
            c_s[...] = c0[...]
            m_s[...] = m0[...]
            for u in range(N_UNITS):
                n_s[u] = jnp.broadcast_to(n0[u:u + 1, :], (N_ROWS, M_DQK))

    yield _DONE
    lower, upper = _tri_masks()
    lane1 = lax.broadcasted_iota(jnp.int32, (1, LANES), 1)
    ones_rows = jnp.ones((N_ROWS, CHUNK), BF16)
    per_dir = ((qf, kf, vf, of, colf, bcolf, rowf, lower, s),
               (qb, kb, vb, ob, colb, bcolb, rowb, upper, nc - 1 - s))
    m_prev = m_s[...]
    m_next = m_prev
    c_new, n_new = [], []
    for d, (q_ref, k_ref, v_ref, o_ref, col_ref, bcol_ref, row_ref, mask, chunk) in enumerate(per_dir):
        last = CHUNK - 1 if d == 0 else 0
        stab = jnp.maximum(m_prev, col_ref[...])
        w_c = jnp.exp(m_prev - stab)
        stab_last = stab[last:last + 1, :]
        m_new = bcol_ref[last:last + 1, :] + stab_last
        decay = w_c[last:last + 1, :]
        c_old = [c_s[d, h] for h in range(M_HEADS)]
        n_old = [n_s[d * M_HEADS + h] for h in range(M_HEADS)]
        for h in range(M_HEADS):
            u = d * M_HEADS + h
            stab_b = _lane_bcast(stab, u)
            u_row = row_ref[u:u + 1, :]
            p = jnp.exp(jnp.where(mask, u_row - stab_b, -jnp.inf))
            qh = q_ref[:, h * M_DQK:(h + 1) * M_DQK] * (M_DQK ** -0.5)
            k_t = k_ref[:, h * M_DQK:(h + 1) * M_DQK].T
            v16 = v_ref[:, h * M_DV:(h + 1) * M_DV].astype(BF16)
            sc = _dot(qh.astype(BF16), k_t.astype(BF16)) * p
            lhs = jnp.concatenate([sc.astype(BF16), (qh * _lane_bcast(w_c, u)).astype(BF16)], axis=1)
            num = _dot(lhs, jnp.concatenate([v16, c_old[h].astype(BF16)], axis=0))
            den = _dot(jnp.concatenate([ones_rows, n_old[h].astype(BF16)], axis=1), lhs, _NT)[0:1, :]
            m_t = row_ref[G_B + u:G_B + u + 1, :] + jnp.maximum(m_prev[:, u:u + 1],
                                                                row_ref[G_CU + u:G_CU + u + 1, :])
            inv_row = 1.0 / jnp.maximum(jnp.abs(den), jnp.exp(-m_t))
            inv = jnp.broadcast_to(inv_row, (CHUNK, LANES)).T
            hbuf[d, :, h * M_DV:(h + 1) * M_DV] = num * jnp.concatenate([inv, inv], axis=1)

            kw_t = (k_t * jnp.exp(u_row - stab_last[:, u:u + 1])).astype(BF16)
            dec = decay[:, u:u + 1]
            c_new.append(dec * c_old[h] + _dot(kw_t, v16))
            n_new.append(dec * n_old[h] + _dot(ones_rows, kw_t, _NT))
            yield _WORK
        mine = (lane1 >= d * M_HEADS) & (lane1 < (d + 1) * M_HEADS)
        m_next = jnp.where(mine, m_new, m_next)
    for u in range(N_UNITS):
        c_s[u // M_HEADS, u % M_HEADS] = c_new[u]
        n_s[u] = n_new[u]
    m_s[...] = m_next

    yield _DONE
    def complete(d):
        _, _, _, o_ref, _, _, _, _, chunk = per_dir[d]
        rows = _chunk_rows(chunk)
        for h in range(M_HEADS):
            cols = slice(h * M_DV, (h + 1) * M_DV)
            tot = hbuf[d, :, cols] + hpart[rows, cols]
            hm_out[rows, cols] = (_rms(tot) * gn[:, cols] * o_ref[:, cols]).astype(BF16)

    def park(d):
        hpart[_chunk_rows(per_dir[d][-1]), :] = hbuf[d]

    _finalise(s, nc, complete, park)

    @_when(s == nc - 1)
    def _():
        c_out[...] = c_s[...]
        m_out[...] = m_s[...]
        for u in range(N_UNITS):
            n_out[u:u + 1, :] = n_s[u][0:1, :]

    yield _DONE


def _sequence_out_spec(seq, width):
    mode = dict(pipeline_mode=pl.Buffered(1)) if seq * width * 2 >= 2 * MIB else {}
    return pl.BlockSpec((None, seq, width), lambda b, *_: (b, 0, 0), **mode)


def _stream_maps(nc, whole):
    if whole:
        return nc * CHUNK, [lambda cb: (lambda b: (b, cb))]
    return CHUNK, [lambda cb: (lambda b, s: (b * nc + s, cb)),
                   lambda cb: (lambda b, s: (b * nc + nc - 1 - s, cb))]


def _mlstm_specs(p, scans, g_norm, state, nb, nc, whole):
    zero_init = state is None
    seq = nc * CHUNK
    col, bcol, row = scans
    rows, maps = _stream_maps(nc, whole)

    def stream(mk):
        return [pl.BlockSpec((rows, M_HEADS * M_DQK), mk(P_Q // (M_HEADS * M_DQK))),
                pl.BlockSpec((rows, M_HEADS * M_DQK), mk(P_K // (M_HEADS * M_DQK))),
                pl.BlockSpec((rows, MIX_M), mk(P_V // MIX_M)),
                pl.BlockSpec((rows, MIX_M), mk(P_O // MIX_M)),
                pl.BlockSpec((rows, GATE_W), mk(0)),
                pl.BlockSpec((rows, GATE_W), mk(0)),
                pl.BlockSpec((rows, GATE_W), mk(0))]

    in_specs = sum((stream(mk) for mk in maps), []) + [pl.BlockSpec((1, MIX_M), lambda b, *_: (0, 0))]
    args = ([p] * 4 + [col, bcol, row]) * len(maps) + [g_norm]
    state_specs = [pl.BlockSpec((None, N_DIR, M_HEADS, M_DQK, M_DV), lambda b, *_: (b, 0, 0, 0, 0)),
                   pl.BlockSpec((None, N_UNITS, M_DQK), lambda b, *_: (b, 0, 0)),
                   pl.BlockSpec((None, 1, LANES), lambda b, *_: (b, 0, 0))]
    if not zero_init:
        in_specs += state_specs
        args += list(state)
    out_specs = [_sequence_out_spec(seq, MIX_M)] + state_specs
    out_shape = [jax.ShapeDtypeStruct((nb, seq, MIX_M), BF16),
                 jax.ShapeDtypeStruct((nb, N_DIR, M_HEADS, M_DQK, M_DV), F32),
                 jax.ShapeDtypeStruct((nb, N_UNITS, M_DQK), F32),
                 jax.ShapeDtypeStruct((nb, 1, LANES), F32)]
    scratch = [pltpu.VMEM((N_DIR, M_HEADS, M_DQK, M_DV), F32),
               pltpu.VMEM((N_UNITS, N_ROWS, M_DQK), F32),
               pltpu.VMEM((1, LANES), F32),
               pltpu.VMEM((seq, MIX_M), F32),
               pltpu.VMEM((N_DIR, CHUNK, MIX_M), F32)]
    return in_specs, args, out_specs, out_shape, scratch


def _ssd_phases(refs, nc, zero_init, s):
    (xf, bf, cf, zf, colf, rowf, xb, bb, cb, zb, colb, rowb, dskip, gn) = refs[:14]
    pos = 14
    if not zero_init:
        s0 = refs[pos]
        pos += 1
    ys_out, s_out, st_s, ypart, ybuf = refs[pos:]

    @_when(s == 0)
    def _():
        if zero_init:
            st_s[...] = jnp.zeros_like(st_s)
        else:
            for d in range(N_DIR):
                for pr in range(S_PAIRS):
                    st_s[d, pr] = s0[d, pr].T

    yield _DONE
    lower, upper = _tri_masks()
    low_half = lax.broadcasted_iota(jnp.int32, (CHUNK, LANES), 1) < S_HEADDIM
    low_half1 = lax.broadcasted_iota(jnp.int32, (1, LANES), 1) < S_HEADDIM
    per_dir = ((xf, bf, cf, zf, colf, rowf, lower, s), (xb, bb, cb, zb, colb, rowb, upper, nc - 1 - s))
    st_new = []
    for d, (x_ref, b_ref, c_ref, z_ref, col_ref, row_ref, mask, chunk) in enumerate(per_dir):
        last = CHUNK - 1 if d == 0 else 0
        cs_c = col_ref[...]
        total = cs_c[last:last + 1, :]
        st_old = [st_s[d, pr] for pr in range(S_PAIRS)]
        for g in range(S_GROUPS):
            gs = slice(g * S_STATE, (g + 1) * S_STATE)
            cg = c_ref[:, gs].astype(BF16)
            b_f32 = b_ref[:, gs]
            cbm = _dot(cg, b_f32.astype(BF16), _NT)
            b_t = b_f32.T
            for pg in range(PAIRS_PER_GROUP):
                pr = g * PAIRS_PER_GROUP + pg
                cols = slice(pr * LANES, (pr + 1) * LANES)
                x16 = x_ref[:, cols].astype(BF16)
                zero16 = jnp.zeros_like(x16)
                halves = (jnp.where(low_half, x16, zero16), jnp.where(low_half, zero16, x16))
                state = st_old[pr]
                y_acc = jnp.zeros((CHUNK, LANES), F32)
                s_acc = jnp.zeros((S_STATE, LANES), F32)
                cs_b = []
                for e in range(PAIR):
                    gi = G_DT + d * S_HEADS + pr * PAIR + e
                    cs_b.append(_lane_bcast(cs_c, gi))
                    q_row = row_ref[gi:gi + 1, :]
                    wk_row = row_ref[gi + G_WK - G_DT:gi + G_WK - G_DT + 1, :]
                    mix = cbm * jnp.exp(jnp.where(mask, cs_b[e] - q_row, -jnp.inf))
                    y_acc = y_acc + _dot(mix.astype(BF16), halves[e])
                    s_acc = s_acc + _dot((b_t * wk_row).astype(BF16), halves[e])
                carry = jnp.exp(jnp.where(low_half, cs_b[0], cs_b[1]))
                ybuf[d, :, cols] = y_acc + _dot(cg, state.astype(BF16)) * carry
                gi0 = G_DT + d * S_HEADS + pr * PAIR
                dec = jnp.exp(jnp.where(low_half1, jnp.broadcast_to(total[:, gi0:gi0 + 1], (1, LANES)),
                                        jnp.broadcast_to(total[:, gi0 + 1:gi0 + 2], (1, LANES))))
                st_new.append(state * dec + s_acc)
            yield _WORK
    for d in range(N_DIR):
        for pr in range(S_PAIRS):
            st_s[d, pr] = st_new[d * S_PAIRS + pr]

    yield _DONE
    def complete(d):
        x_ref, _, _, z_ref, _, _, _, chunk = per_dir[d]
        rows = _chunk_rows(chunk)
        y = ybuf[d] + ypart[rows, :] + dskip[...] * x_ref[...]
        ys_out[rows, :] = (_rms(y * z_ref[...]) * gn[...]).astype(BF16)

    def park(d):
        ypart[_chunk_rows(per_dir[d][-1]), :] = ybuf[d]

    _finalise(s, nc, complete, park)

    @_when(s == nc - 1)
    def _():
        for d in range(N_DIR):
            for pr in range(S_PAIRS):
                s_out[d, pr] = st_s[d, pr].T

    yield _DONE


def _ssd_specs(p, xbc, xbc_col, scans, dskip, g_norm, state, nb, nc, whole):
    zero_init = state is None
    seq = nc * CHUNK
    bc_w = S_GROUPS * S_STATE
    assert xbc_col % MIX_S == 0
    col, _, row = scans
    rows, maps = _stream_maps(nc, whole)

    def stream(mk):
        return [pl.BlockSpec((rows, MIX_S), mk(xbc_col // MIX_S)),
                pl.BlockSpec((rows, bc_w), mk((xbc_col + MIX_S) // bc_w)),
                pl.BlockSpec((rows, bc_w), mk((xbc_col + MIX_S) // bc_w + 1)),
                pl.BlockSpec((rows, MIX_S), mk(P_Z // MIX_S)),
                pl.BlockSpec((rows, GATE_W), mk(0)),
                pl.BlockSpec((rows, GATE_W), mk(0))]

    const = lambda b, *_: (0, 0)
    in_specs = sum((stream(mk) for mk in maps), []) + [pl.BlockSpec((1, MIX_S), const),
                                                        pl.BlockSpec((1, MIX_S), const)]
    args = [xbc, xbc, xbc, p, col, row] * len(maps) + [dskip, g_norm]
    state_spec = pl.BlockSpec((None, N_DIR, S_PAIRS, LANES, S_STATE), lambda b, *_: (b, 0, 0, 0, 0))
    if not zero_init:
        in_specs.append(state_spec)
        args.append(state)
    out_specs = [_sequence_out_spec(seq, MIX_S), state_spec]
    out_shape = [jax.ShapeDtypeStruct((nb, seq, MIX_S), BF16),
                 jax.ShapeDtypeStruct((nb, N_DIR, S_PAIRS, LANES, S_STATE), F32)]
    scratch = [pltpu.VMEM((N_DIR, S_PAIRS, S_STATE, LANES), F32),
               pltpu.VMEM((seq, MIX_S), F32),
               pltpu.VMEM((N_DIR, CHUNK, MIX_S), F32)]
    return in_specs, args, out_specs, out_shape, scratch


M_STREAMS, S_STREAMS = 7, 6


def _mixer_kernel(*refs, nc, zero_init, counts, whole):
    (m_in, s_in), (m_out, s_out), (m_scr, s_scr) = counts
    pos = 0
    parts = []
    for n in (m_in, s_in, m_out, s_out, m_scr, s_scr):
        parts.append(refs[pos:pos + n])
        pos += n

    def chunk_views(inputs, n_streams, s):
        streams, rest = inputs[:n_streams], inputs[n_streams:]
        fwd = tuple(r.at[_chunk_rows(s)] for r in streams)
        bwd = tuple(r.at[_chunk_rows(nc - 1 - s)] for r in streams)
        return fwd + bwd + rest

    for s in (range(nc) if whole else [pl.program_id(1)]):
        m_inputs = chunk_views(parts[0], M_STREAMS, s) if whole else parts[0]
        s_inputs = chunk_views(parts[1], S_STREAMS, s) if whole else parts[1]
        gens = (_mlstm_phases(m_inputs + parts[2] + parts[4], nc, zero_init, s),
                _ssd_phases(s_inputs + parts[3] + parts[5], nc, zero_init, s))
        for _ in range(3):
            live = list(gens)
            while live:
                live = [g for g in live if next(g) is not _DONE]


WHOLE_SEQUENCE_MAX_CHUNKS = 2


def _mixers(p, xbc, xbc_col, scans, g_mlstm, dskip, g_ssd, mlstm_state, ssd_state, nb, nc):
    whole = nc <= WHOLE_SEQUENCE_MAX_CHUNKS
    m = _mlstm_specs(p, scans, g_mlstm, mlstm_state, nb, nc, whole)
    s = _ssd_specs(p, xbc, xbc_col, scans, dskip, g_ssd, ssd_state, nb, nc, whole)
    counts = tuple((len(a), len(b)) for a, b in ((m[0], s[0]), (m[2], s[2]), (m[4], s[4])))
    return pl.pallas_call(
        functools.partial(_mixer_kernel, nc=nc, zero_init=mlstm_state is None, counts=counts, whole=whole),
        grid=(nb,) if whole else (nb, nc),
        in_specs=m[0] + s[0],
        out_specs=m[2] + s[2],
        out_shape=m[3] + s[3],
        scratch_shapes=m[4] + s[4],
        compiler_params=_params(("parallel",) if whole else ("parallel", "arbitrary"), VMEM_LIMIT_LARGE),
        name="mixers",
    )(*(m[1] + s[1]))


OUT_TM = 512


OUT_SUB = 256


def _outproj_kernel(hm_ref, ys_ref, w_ref, x_ref, g1_ref, gpost_ref, gpre_ref, sc_ref, sh_ref, x1_ref, u2_ref):
    for r in range(OUT_TM // OUT_SUB):
        rows = slice(r * OUT_SUB, (r + 1) * OUT_SUB)
        mix = _dot(jnp.concatenate([hm_ref[rows, :], ys_ref[rows, :]], axis=1), w_ref[...])
        x1 = x_ref[rows, :] + g1_ref[...] * (_rms(mix) * gpost_ref[...])
        x1_ref[rows, :] = x1
        u2_ref[rows, :] = (_rms(x1) * gpre_ref[...] * (1.0 + sc_ref[...]) + sh_ref[...]).astype(BF16)


def _out_proj(hm, ys, w_out16, x2d, g1, gpost, gpre, sc2, sh2, rows_per_mod):
    t = x2d.shape[0]
    tiles_per_mod = rows_per_mod // OUT_TM
    row = lambda i: (i, 0)
    const = lambda i: (0, 0)
    mod = pl.BlockSpec((None, 1, D_MODEL), lambda i: (i // tiles_per_mod, 0, 0))
    return pl.pallas_call(
        _outproj_kernel,
        grid=(t // OUT_TM,),
        in_specs=[pl.BlockSpec((OUT_TM, MIX_M), row), pl.BlockSpec((OUT_TM, MIX_S), row),
                  pl.BlockSpec((D_MODEL, D_MODEL), const), pl.BlockSpec((OUT_TM, D_MODEL), row),
                  mod, pl.BlockSpec((1, D_MODEL), const), pl.BlockSpec((1, D_MODEL), const), mod, mod],
        out_specs=[pl.BlockSpec((OUT_TM, D_MODEL), row), pl.BlockSpec((OUT_TM, D_MODEL), row)],
        out_shape=[jax.ShapeDtypeStruct((t, D_MODEL), F32), jax.ShapeDtypeStruct((t, D_MODEL), BF16)],
        compiler_params=_params(("parallel",)),
        name="out_proj",
    )(hm, ys, w_out16, x2d, g1, gpost, gpre, sc2, sh2)


MLP_TM = 1024
MLP_TH = 512
MLP_SUB = 512


def _mlp_kernel(u_ref, w1_ref, w2_ref, x1_hbm, g2_ref, gpost_ref, o_ref, x1_buf, x1_sem):
    i = pl.program_id(0)
    j = pl.program_id(1)

    def x1_copy():
        rows = pl.ds(pl.multiple_of(i * MLP_TM, MLP_TM), MLP_TM)
        return pltpu.make_async_copy(x1_hbm.at[rows, :], x1_buf, x1_sem)

    def partial_sum(rows, w1, w2):
        hid = jnp.square(jnp.maximum(_dot(u_ref[rows, :], w1), 0.0)).astype(BF16)
        return _dot(hid, w2)

    sub_blocks = [slice(r * MLP_SUB, (r + 1) * MLP_SUB) for r in range(MLP_TM // MLP_SUB)]
    last = pl.num_programs(1) - 1

    @pl.when(j == 0)
    def _():
        x1_copy().start()
        w1, w2 = w1_ref[...].astype(BF16), w2_ref[...].astype(BF16)
        for rows in sub_blocks:
            o_ref[rows, :] = partial_sum(rows, w1, w2)

    @pl.when((j > 0) & (j < last))
    def _():
        w1, w2 = w1_ref[...].astype(BF16), w2_ref[...].astype(BF16)
        for rows in sub_blocks:
            o_ref[rows, :] += partial_sum(rows, w1, w2)

    @pl.when(j == last)
    def _():
        x1_copy().wait()
        w1, w2 = w1_ref[...].astype(BF16), w2_ref[...].astype(BF16)
        for rows in sub_blocks:
            total = o_ref[rows, :] + partial_sum(rows, w1, w2)
            o_ref[rows, :] = x1_buf[rows, :] + g2_ref[...] * (_rms(total) * gpost_ref[...])


def _mlp(u2, w1, w2, x1, g2, gpost, rows_per_mod):
    t = u2.shape[0]
    tiles_per_mod = rows_per_mod // MLP_TM
    assert D_FF // MLP_TH >= 2
    return pl.pallas_call(
        _mlp_kernel,
        grid=(t // MLP_TM, D_FF // MLP_TH),
        in_specs=[pl.BlockSpec((MLP_TM, D_MODEL), lambda i, j: (i, 0)),
                  pl.BlockSpec((D_MODEL, MLP_TH), lambda i, j: (0, j)),
                  pl.BlockSpec((MLP_TH, D_MODEL), lambda i, j: (j, 0)),
                  pl.BlockSpec(memory_space=pl.ANY),
                  pl.BlockSpec((None, 1, D_MODEL), lambda i, j: (i // tiles_per_mod, 0, 0)),
                  pl.BlockSpec((1, D_MODEL), lambda i, j: (0, 0))],
        out_specs=pl.BlockSpec((MLP_TM, D_MODEL), lambda i, j: (i, 0)),
        out_shape=jax.ShapeDtypeStruct((t, D_MODEL), F32),
        scratch_shapes=[pltpu.VMEM((MLP_TM, D_MODEL), F32), pltpu.SemaphoreType.DMA(())],
        compiler_params=_params(("parallel", "arbitrary"), VMEM_LIMIT_LARGE),
        name="mlp",
    )(u2, w1, w2, x1, g2, gpost)


def _gate_row(i_vals, f_vals, dt_vals):
    v = jnp.concatenate([i_vals.reshape(-1), f_vals.reshape(-1), dt_vals.reshape(-1)]).astype(F32)
    return jnp.pad(v, (0, GATE_W - N_GATES)).reshape(1, GATE_W)


def _block(x, mods, state, weights, width):
    nb, seq, _ = x.shape
    nc = seq // CHUNK
    t = nb * seq
    x2d = x.reshape(t, D_MODEL)
    sh1, sc1, g1, sh2, sc2, g2 = mods
    rows_per_mod = t // sh1.shape[0]
    w = weights

    fused_conv = width == seq and IN_TM % seq == 0
    conv = (w["conv_w9"], w["conv_b"], width) if fused_conv else None
    p, gc = _in_proj(x2d, sc1, sh1, w["g_pre_mix"], w["w_t"], w["wg"], rows_per_mod, conv)
    scans = _gate_scans(gc, w["gate_bias"], w["gate_alog"])
    if fused_conv:
        xbc, xbc_col = p, P_XBC
    else:
        xbc, xbc_col = _conv(p, w["conv_w9"], w["conv_b"], nb, seq, width, CONV_BLOCK_ELEMS // seq), 0

    if state is None:
        m_state = s_state = None
    else:
        c0, n0, m0, s0 = state
        m_state = (c0, n0.reshape(nb, N_UNITS, M_DQK),
                   jnp.pad(m0.reshape(nb, 1, N_UNITS), ((0, 0), (0, 0), (0, LANES - N_UNITS))))
        s_state = s0.reshape(nb, N_DIR, S_PAIRS, LANES, S_STATE)
    hm, c_new, n_new, m_new, ys, s_new = _mixers(p, xbc, xbc_col, scans, w["g_mlstm_norm"], w["dskip"],
                                                 w["g_ssd_norm"], m_state, s_state, nb, nc)

    x1, u2 = _out_proj(hm.reshape(t, MIX_M), ys.reshape(t, MIX_S), w["w_out"], x2d, g1,
                       w["g_post_mix"], w["g_pre_mlp"], sc2, sh2, rows_per_mod)
    y = _mlp(u2, w["w_mlp_in"], w["w_mlp_out"], x1, g2, w["g_post_mlp"], rows_per_mod)
    new_state = (c_new.reshape(nb, 1, N_DIR, M_HEADS, M_DQK, M_DV),
                 n_new.reshape(nb, 1, N_DIR, M_HEADS, M_DQK),
                 m_new[:, 0, :N_UNITS].reshape(nb, 1, N_DIR, M_HEADS),
                 s_new.reshape(nb, 1, N_DIR, S_HEADS, S_HEADDIM, S_STATE))
    return y.reshape(nb, seq, D_MODEL), new_state


def kernel(x_prompt, x_sample, state_mlstm_c, state_mlstm_n, state_mlstm_m, state_ssd, c, c_ctx, w_mod, b_mod,
           g_pre_mix, g_post_mix, w_in, b_igate, b_fgate, conv_w, conv_b, dt_bias, a_log, d_skip, g_mlstm_norm,
           g_ssd_norm, w_out, g_pre_mlp, g_post_mlp, w_mlp_in, w_mlp_out):
    assert w_mod.shape[0] == 1, "one layer"
    nb_s = x_sample.shape[0]

    cond8 = jnp.zeros((8, D_MODEL), F32).at[0].set(c_ctx).at[1:1 + nb_s].set(c)
    mod = _modulation(cond8, w_mod[0], b_mod[0].reshape(1, -1))
    mods = [mod[:, k * D_MODEL:(k + 1) * D_MODEL] for k in range(6)]
    mods_p = [m[0:1].reshape(1, 1, D_MODEL) for m in mods]
    mods_s = [m[1:1 + nb_s].reshape(nb_s, 1, D_MODEL) for m in mods]

    w_t = w_in[0].T.astype(BF16)
    assert w_t.shape[0] == P_MAIN + N_GATES
    gate_rows = jnp.concatenate([w_t[P_Z:P_Z + G_DT], w_t[P_MAIN + G_DT:]], axis=0)
    gate_rows = jnp.pad(gate_rows, ((0, GATE_W - N_GATES), (0, 0)))
    zeros_u = jnp.zeros((N_UNITS,), F32)
    row = lambda v: v.reshape(1, -1)
    weights = dict(
        w_t=w_t, wg=gate_rows,
        g_pre_mix=row(g_pre_mix[0]), g_post_mix=row(g_post_mix[0]),
        g_pre_mlp=row(g_pre_mlp[0]), g_post_mlp=row(g_post_mlp[0]),
        conv_w9=conv_w[0].reshape(9, XBC), conv_b=row(conv_b[0]),
        gate_bias=_gate_row(b_igate[0], b_fgate[0], dt_bias[0]),
        gate_alog=_gate_row(zeros_u, zeros_u, a_log[0]),
        dskip=row(jnp.repeat(d_skip[0], S_HEADDIM)),
        g_mlstm_norm=row(g_mlstm_norm[0]), g_ssd_norm=row(g_ssd_norm[0]),
        w_out=w_out[0].astype(BF16), w_mlp_in=w_mlp_in[0], w_mlp_out=w_mlp_out[0])

    y_p, st = _block(x_prompt, mods_p, None, weights, x_prompt.shape[1])
    cache = (state_mlstm_c[:, 0], state_mlstm_n[:, 0], state_mlstm_m[:, 0], state_ssd[:, 0])
    y_s, _ = _block(x_sample, mods_s, cache, weights, GRID_W)
    return (y_p, y_s) + st
```

```python
import functools

import jax
import jax.numpy as jnp
from jax import lax
from jax.experimental import pallas as pl
from jax.experimental.pallas import tpu as pltpu

F32 = jnp.float32
BF16 = jnp.bfloat16

D_MODEL = 2048
CHUNK = 128
N_DIR = 2
M_HEADS = 4
M_DQK = 128
M_DV = 256
MIX_M = M_HEADS * M_DV
S_HEADS = 16
S_HEADDIM = 64
S_STATE = 128
S_GROUPS = 4
S_REP = S_HEADS // S_GROUPS
MIX_S = S_HEADS * S_HEADDIM
XBC = MIX_S + 2 * S_GROUPS * S_STATE
D_FF = 4 * D_MODEL
GRID_W = 64
EPS = 1e-6
LANES = 128

P_Q = 0
P_K = M_HEADS * M_DQK
P_V = 2 * M_HEADS * M_DQK
P_O = P_V + MIX_M
P_Z = P_O + MIX_M
P_XBC = P_Z + MIX_S
P_MAIN = P_XBC + XBC
GATE_W = LANES
N_UNITS = N_DIR * M_HEADS
G_I = 0
G_F = N_UNITS
G_DT = 2 * N_UNITS
N_GATES = G_DT + N_DIR * S_HEADS
G_WK = G_DT + N_DIR * S_HEADS
G_CU = G_WK + N_DIR * S_HEADS
G_B = G_CU + N_UNITS
assert G_B + N_UNITS <= GATE_W

PAIR = LANES // S_HEADDIM
S_PAIRS = S_HEADS // PAIR
PAIRS_PER_GROUP = S_REP // PAIR

MIB = 1024 * 1024
VMEM_LIMIT = 48 * MIB
VMEM_LIMIT_LARGE = 60 * MIB

_NT = (((1,), (1,)), ((), ()))


def _params(sem, limit=VMEM_LIMIT):
    return pltpu.CompilerParams(dimension_semantics=sem, vmem_limit_bytes=limit)


def _silu(x):
    return x / (1.0 + jnp.exp(-x))


def _sigmoid(x):
    return 1.0 / (1.0 + jnp.exp(-x))


def _rms(x):
    return x * lax.rsqrt(jnp.mean(x * x, axis=-1, keepdims=True) + EPS)


def _dot(a, b, dims=None, precision=None):
    if dims is None:
        dims = (((a.ndim - 1,), (0,)), ((), ()))
    return lax.dot_general(a, b, dims, precision=precision, preferred_element_type=F32)


def _lane_bcast(tile, lane):
    return jnp.broadcast_to(tile[:, lane:lane + 1], tile.shape)


def _mod_kernel(c_ref, w_ref, b_ref, o_ref):
    a = _silu(c_ref[...]).astype(BF16)
    o_ref[...] = _dot(a, w_ref[...].astype(BF16)) + b_ref[...]


def _modulation(cond8, w_mod, b_mod):
    n = w_mod.shape[1]
    tn = 1024
    return pl.pallas_call(
        _mod_kernel,
        grid=(n // tn,),
        in_specs=[pl.BlockSpec((8, D_MODEL), lambda j: (0, 0)),
                  pl.BlockSpec((D_MODEL, tn), lambda j: (0, j)),
                  pl.BlockSpec((1, tn), lambda j: (0, j))],
        out_specs=pl.BlockSpec((8, tn), lambda j: (0, j)),
        out_shape=jax.ShapeDtypeStruct((8, n), F32),
        compiler_params=_params(("parallel",)),
        name="modulation",
    )(cond8, w_mod, b_mod)


IN_TM = 1024
IN_TN = 1024
IN_SUB = 256


def _grid_conv(x, w_ref, b_ref, seq, width):
    n = x.shape[0]
    assert n % seq == 0 and seq % width == 0 and seq & (seq - 1) == 0 and width & (width - 1) == 0
    t = lax.broadcasted_iota(jnp.int32, x.shape, 0)
    c = jnp.bitwise_and(t, width - 1)
    xl = jnp.where(c >= 1, pltpu.roll(x, 1, 0), 0.0)
    xr = jnp.where(c <= width - 2, pltpu.roll(x, n - 1, 0), 0.0)

    def taps(di):
        return w_ref[3 * di:3 * di + 1, :] * xl + w_ref[3 * di + 1:3 * di + 2, :] * x \
            + w_ref[3 * di + 2:3 * di + 3, :] * xr

    out = taps(1) + b_ref[...]
    if seq > width:
        r = jnp.bitwise_and(t, seq - 1)
        out = out + jnp.where(r >= width, pltpu.roll(taps(0), width, 0), 0.0)
        out = out + jnp.where(r < seq - width, pltpu.roll(taps(2), n - width, 0), 0.0)
    return _silu(out)


SIDE_STEPS = 4


def _inproj_kernel(x_ref, sc_ref, sh_ref, g_ref, w_ref, wg_ref, *rest, conv_width, side_cast):
    rest = list(rest)
    if conv_width is not None:
        cw_ref, cb_ref = rest[:2]
        rest = rest[2:]
    if side_cast:
        side_ref, o_ref, gc_ref, side16_ref, u_ref = rest
    else:
        o_ref, gc_ref, u_ref = rest
    j = pl.program_id(1)

    if side_cast:
        @pl.when(j < SIDE_STEPS)
        def _():
            side16_ref[...] = side_ref[...].astype(BF16)

    j_o, j_z = P_O // IN_TN, P_Z // IN_TN
    j_x = P_XBC // IN_TN
    plain = (j > 0) & (j != j_o) & (j != j_z)
    if conv_width is not None:
        plain = plain & (j < j_x)

        @pl.when(j >= j_x)
        def _():
            o_ref[...] = _grid_conv(_dot(u_ref[...], w_ref[...], _NT), cw_ref, cb_ref, conv_width, conv_width)

    @pl.when(j == 0)
    def _():
        for r in range(IN_TM // IN_SUB):
            rows = slice(r * IN_SUB, (r + 1) * IN_SUB)
            y = _rms(x_ref[rows, :]) * g_ref[...]
            u = (y * (1.0 + sc_ref[...]) + sh_ref[...]).astype(BF16)
            u_ref[rows, :] = u
            o_ref[rows, :] = _dot(u, w_ref[...], _NT)

    @pl.when(plain)
    def _():
        o_ref[...] = _dot(u_ref[...], w_ref[...], _NT)

    @pl.when((j == j_o) | (j == j_z))
    def _():
        acc = _dot(u_ref[...], w_ref[...], _NT)
        sig = _sigmoid(acc)
        o_ref[...] = jnp.where(j == j_o, sig, acc * sig)

    @pl.when(j == pl.num_programs(1) - 1)
    def _():
        gc_ref[...] = _dot(u_ref[...], wg_ref[...], _NT)


def _in_proj(x2d, sc, sh, g, w_t, wg, rows_per_mod, conv=None, side=None):
    t = x2d.shape[0]
    tiles_per_mod = rows_per_mod // IN_TM
    n_a = P_Z // IN_TN
    j_x = P_XBC // IN_TN
    assert P_O % IN_TN == 0 and MIX_M == IN_TN and MIX_S == IN_TN

    def first_row(i, j):
        return (j * (IN_TN // G_DT) + (j >= n_a).astype(jnp.int32)) * G_DT, 0

    conv_specs, conv_args, conv_width = [], [], None
    if conv is not None:
        conv_w9, conv_b, conv_width = conv
        assert IN_TM % conv_width == 0 and P_XBC % IN_TN == 0
        conv_specs = [pl.BlockSpec((9, IN_TN), lambda i, j: (0, jnp.maximum(j - j_x, 0))),
                      pl.BlockSpec((1, IN_TN), lambda i, j: (0, jnp.maximum(j - j_x, 0)))]
        conv_args = [conv_w9, conv_b]
    side_specs, side_args, side_out_specs, side_out_shape = [], [], [], []
    if side is not None:
        side_blocks = (t // IN_TM) * SIDE_STEPS
        assert P_MAIN // IN_TN >= SIDE_STEPS and side.shape[0] % (side_blocks * N_ROWS) == 0
        side_spec = pl.BlockSpec((side.shape[0] // side_blocks, side.shape[1]),
                                 lambda i, j: (i * SIDE_STEPS + jnp.minimum(j, SIDE_STEPS - 1), 0))
        side_specs, side_args, side_out_specs = [side_spec], [side], [side_spec]
        side_out_shape = [jax.ShapeDtypeStruct(side.shape, BF16)]
    return pl.pallas_call(
        functools.partial(_inproj_kernel, conv_width=conv_width, side_cast=side is not None),
        grid=(t // IN_TM, P_MAIN // IN_TN),
        in_specs=[pl.BlockSpec((IN_TM, D_MODEL), lambda i, j: (i, 0)),
                  pl.BlockSpec((None, 1, D_MODEL), lambda i, j: (i // tiles_per_mod, 0, 0)),
                  pl.BlockSpec((None, 1, D_MODEL), lambda i, j: (i // tiles_per_mod, 0, 0)),
                  pl.BlockSpec((1, D_MODEL), lambda i, j: (0, 0)),
                  pl.BlockSpec((pl.Element(IN_TN), pl.Element(D_MODEL)), first_row),
                  pl.BlockSpec((GATE_W, D_MODEL), lambda i, j: (0, 0))] + conv_specs + side_specs,
        out_specs=[pl.BlockSpec((IN_TM, IN_TN), lambda i, j: (i, j)),
                   pl.BlockSpec((IN_TM, GATE_W), lambda i, j: (i, 0))] + side_out_specs,
        out_shape=[jax.ShapeDtypeStruct((t, P_MAIN), F32),
                   jax.ShapeDtypeStruct((t, GATE_W), F32)] + side_out_shape,
        scratch_shapes=[pltpu.VMEM((IN_TM, D_MODEL), BF16)],
        compiler_params=_params(("parallel", "arbitrary"), VMEM_LIMIT_LARGE),
        name="in_proj",
    )(x2d, sc, sh, g, w_t, wg, *conv_args, *side_args)


def _conv_kernel(x_ref, w_ref, b_ref, o_ref, *, seq, width):
    o_ref[...] = _grid_conv(x_ref[...], w_ref, b_ref, seq, width)


CONV_BLOCK_ELEMS = 2 * MIB // 4


def _conv(p, conv_w9, conv_b, nb, seq, width, cn):
    cn = min(cn, XBC)
    col0 = P_XBC // cn
    return pl.pallas_call(
        functools.partial(_conv_kernel, seq=seq, width=width),
        grid=(nb, XBC // cn),
        in_specs=[pl.BlockSpec((seq, cn), lambda b, j: (b, col0 + j)),
                  pl.BlockSpec((9, cn), lambda b, j: (0, j)),
                  pl.BlockSpec((1, cn), lambda b, j: (0, j))],
        out_specs=pl.BlockSpec((seq, cn), lambda b, j: (b, j)),
        out_shape=jax.ShapeDtypeStruct((nb * seq, XBC), F32),
        compiler_params=_params(("parallel", "parallel")),
        name="grid_conv",
    )(p, conv_w9, conv_b)


def _tri_masks():
    r = lax.broadcasted_iota(jnp.int32, (CHUNK, CHUNK), 0)
    c = lax.broadcasted_iota(jnp.int32, (CHUNK, CHUNK), 1)
    return r >= c, r <= c


SCAN_CHUNKS = 8


def _scan_kernel(gc_ref, bias_ref, alog_ref, col_ref, bcol_ref, row_ref):
    lower, upper = _tri_masks()
    lo, up = lower.astype(F32), upper.astype(F32)
    hi = lax.Precision.HIGHEST
    lane = lax.broadcasted_iota(jnp.int32, (CHUNK, GATE_W), 1)
    time = lax.broadcasted_iota(jnp.int32, (CHUNK, GATE_W), 0)
    lane1 = lax.broadcasted_iota(jnp.int32, (1, GATE_W), 1)

    def backward(l):
        unit_bwd = (l < G_DT) & (jnp.bitwise_and(l, N_UNITS - 1) >= M_HEADS)
        return unit_bwd | ((l >= G_DT + S_HEADS) & (l < N_GATES))

    is_bwd, is_bwd1 = backward(lane), backward(lane1)
    is_i = lane < G_F
    is_f = (lane >= G_F) & (lane < G_DT)
    is_dt = (lane >= G_DT) & (lane < N_GATES)
    neg_a = -jnp.exp(alog_ref[...])
    for c in range(SCAN_CHUNKS):
        rows = slice(c * CHUNK, (c + 1) * CHUNK)
        g = gc_ref[rows, :] + bias_ref[...]
        soft = jnp.log1p(jnp.exp(-jnp.abs(g)))
        logf = jnp.minimum(g, 0.0) - soft
        dt = jnp.maximum(g, 0.0) + soft
        x = jnp.where(is_f, logf, jnp.where(is_dt, dt * neg_a, 0.0))
        cs = jnp.where(is_bwd, _dot(up, x, precision=hi), _dot(lo, x, precision=hi))
        b_units = pltpu.roll(cs, GATE_W - G_F, 1)
        u = g - b_units
        cu = u
        k = 1
        while k < CHUNK:
            prev = jnp.where(time >= k, pltpu.roll(cu, k, 0), -jnp.inf)
            nxt = jnp.where(time < CHUNK - k, pltpu.roll(cu, CHUNK - k, 0), -jnp.inf)
            cu = jnp.maximum(cu, jnp.where(is_bwd, nxt, prev))
            k *= 2
        total = jnp.where(is_bwd1, cs[0:1, :], cs[CHUNK - 1:CHUNK, :])
        wk = jnp.exp(total - cs) * dt
        q = cs - jnp.log(dt)
        col_ref[rows, :] = jnp.where(is_i, cu, cs)
        bcol_ref[rows, :] = b_units
        tail = jnp.where(lane < G_CU, pltpu.roll(wk, G_WK - G_DT, 1),
                         jnp.where(lane < G_B, pltpu.roll(cu, G_CU, 1), pltpu.roll(b_units, G_B, 1)))
        row_ref[rows, :] = jnp.where(is_i, u, jnp.where(is_dt, q, tail)).T


def _gate_scans(gc, bias, alog):
    t = gc.shape[0]
    tm = SCAN_CHUNKS * CHUNK
    row = lambda i: (i, 0)
    const = lambda i: (0, 0)
    return pl.pallas_call(
        _scan_kernel,
        grid=(t // tm,),
        in_specs=[pl.BlockSpec((tm, GATE_W), row), pl.BlockSpec((1, GATE_W), const),
                  pl.BlockSpec((1, GATE_W), const)],
        out_specs=[pl.BlockSpec((tm, GATE_W), row)] * 3,
        out_shape=[jax.ShapeDtypeStruct((t, GATE_W), F32)] * 3,
        compiler_params=_params(("parallel",)),
        name="gate_scans",
    )(gc, bias, alog)


N_ROWS = 16
_WORK, _DONE = "work", "done"


def _mlstm_phases(refs, nc, zero_init):
    (qf, kf, vf, of, colf, bcolf, rowf, qb, kb, vb, ob, colb, bcolb, rowb, gn) = refs[:15]
    pos = 15
    if not zero_init:
        c0, n0, m0 = refs[pos:pos + 3]
        pos += 3
    hm_out, c_out, n_out, m_out, c_s, n_s, m_s, hpart, hbuf = refs[pos:]
    s = pl.program_id(1)
    half = nc // 2

    @pl.when(s == 0)
    def _():
        hpart[...] = jnp.zeros_like(hpart)
        if zero_init:
            c_s[...] = jnp.zeros_like(c_s)
            n_s[...] = jnp.zeros_like(n_s)
            m_s[...] = jnp.zeros_like(m_s)
        else:
            c_s[...] = c0[...]
            m_s[...] = m0[...]
            for u in range(N_UNITS):
                n_s[u] = jnp.broadcast_to(n0[u:u + 1, :], (N_ROWS, M_DQK))

    yield _DONE
    lower, upper = _tri_masks()
    lane1 = lax.broadcasted_iota(jnp.int32, (1, LANES), 1)
    ones_rows = jnp.ones((N_ROWS, CHUNK), BF16)
    per_dir = ((qf, kf, vf, of, colf, bcolf, rowf, lower, s),
               (qb, kb, vb, ob, colb, bcolb, rowb, upper, nc - 1 - s))
    m_prev = m_s[...]
    m_next = m_prev
    c_new, n_new = [], []
    for d, (q_ref, k_ref, v_ref, o_ref, col_ref, bcol_ref, row_ref, mask, chunk) in enumerate(per_dir):
        last = CHUNK - 1 if d == 0 else 0
        stab = jnp.maximum(m_prev, col_ref[...])
        w_c = jnp.exp(m_prev - stab)
        stab_last = stab[last:last + 1, :]
        m_new = bcol_ref[last:last + 1, :] + stab_last
        decay = w_c[last:last + 1, :]
        c_old = [c_s[d, h] for h in range(M_HEADS)]
        n_old = [n_s[d * M_HEADS + h] for h in range(M_HEADS)]
        for h in range(M_HEADS):
            u = d * M_HEADS + h
            stab_b = _lane_bcast(stab, u)
            u_row = row_ref[u:u + 1, :]
            p = jnp.exp(jnp.where(mask, u_row - stab_b, -jnp.inf))
            qh = q_ref[:, h * M_DQK:(h + 1) * M_DQK] * (M_DQK ** -0.5)
            k_t = k_ref[:, h * M_DQK:(h + 1) * M_DQK].T
            v16 = v_ref[:, h * M_DV:(h + 1) * M_DV].astype(BF16)
            sc = _dot(qh.astype(BF16), k_t.astype(BF16)) * p
            lhs = jnp.concatenate([sc.astype(BF16), (qh * _lane_bcast(w_c, u)).astype(BF16)], axis=1)
            num = _dot(lhs, jnp.concatenate([v16, c_old[h].astype(BF16)], axis=0))
            den = _dot(jnp.concatenate([ones_rows, n_old[h].astype(BF16)], axis=1), lhs, _NT)[0:1, :]
            m_t = row_ref[G_B + u:G_B + u + 1, :] + jnp.maximum(m_prev[:, u:u + 1],
                                                                row_ref[G_CU + u:G_CU + u + 1, :])
            inv_row = 1.0 / jnp.maximum(jnp.abs(den), jnp.exp(-m_t))
            inv = jnp.broadcast_to(inv_row, (CHUNK, LANES)).T
            hbuf[d, :, h * M_DV:(h + 1) * M_DV] = num * jnp.concatenate([inv, inv], axis=1)

            kw_t = (k_t * jnp.exp(u_row - stab_last[:, u:u + 1])).astype(BF16)
            dec = decay[:, u:u + 1]
            c_new.append(dec * c_old[h] + _dot(kw_t, v16))
            n_new.append(dec * n_old[h] + _dot(ones_rows, kw_t, _NT))
            yield _WORK
        mine = (lane1 >= d * M_HEADS) & (lane1 < (d + 1) * M_HEADS)
        m_next = jnp.where(mine, m_new, m_next)
    for u in range(N_UNITS):
        c_s[u // M_HEADS, u % M_HEADS] = c_new[u]
        n_s[u] = n_new[u]
    m_s[...] = m_next

    yield _DONE
    for d, (_, _, _, o_ref, _, _, _, _, chunk) in enumerate(per_dir):
        rows = pl.ds(pl.multiple_of(chunk * CHUNK, CHUNK), CHUNK)
        for h in range(M_HEADS):
            cols = slice(h * M_DV, (h + 1) * M_DV)
            tot = hbuf[d, :, cols] + hpart[rows, cols]
            hm_out[rows, cols] = (_rms(tot) * gn[:, cols] * o_ref[:, cols]).astype(BF16)
        hpart[rows, :] = hbuf[d]

    @pl.when(s == nc - 1)
    def _():
        c_out[...] = c_s[...]
        m_out[...] = m_s[...]
        for u in range(N_UNITS):
            n_out[u:u + 1, :] = n_s[u][0:1, :]

    yield _DONE


def _sequence_out_spec(seq, width):
    mode = dict(pipeline_mode=pl.Buffered(1)) if seq * width * 2 >= 2 * MIB else {}
    return pl.BlockSpec((None, seq, width), lambda b, s: (b, 0, 0), **mode)


def _mlstm_specs(p, scans, g_norm, state, nb, nc):
    zero_init = state is None
    seq = nc * CHUNK
    col, bcol, row = scans

    def fwd(cb):
        return lambda b, s: (b * nc + s, cb)

    def bwd(cb):
        return lambda b, s: (b * nc + nc - 1 - s, cb)

    def stream(mk):
        return [pl.BlockSpec((CHUNK, M_HEADS * M_DQK), mk(P_Q // (M_HEADS * M_DQK))),
                pl.BlockSpec((CHUNK, M_HEADS * M_DQK), mk(P_K // (M_HEADS * M_DQK))),
                pl.BlockSpec((CHUNK, MIX_M), mk(P_V // MIX_M)),
                pl.BlockSpec((CHUNK, MIX_M), mk(P_O // MIX_M)),
                pl.BlockSpec((CHUNK, GATE_W), mk(0)),
                pl.BlockSpec((CHUNK, GATE_W), mk(0)),
                pl.BlockSpec((CHUNK, GATE_W), mk(0))]

    in_specs = stream(fwd) + stream(bwd) + [pl.BlockSpec((1, MIX_M), lambda b, s: (0, 0))]
    args = [p] * 4 + [col, bcol, row] + [p] * 4 + [col, bcol, row, g_norm]
    state_specs = [pl.BlockSpec((None, N_DIR, M_HEADS, M_DQK, M_DV), lambda b, s: (b, 0, 0, 0, 0)),
                   pl.BlockSpec((None, N_UNITS, M_DQK), lambda b, s: (b, 0, 0)),
                   pl.BlockSpec((None, 1, LANES), lambda b, s: (b, 0, 0))]
    if not zero_init:
        in_specs += state_specs
        args += list(state)
    out_specs = [_sequence_out_spec(seq, MIX_M)] + state_specs
    out_shape = [jax.ShapeDtypeStruct((nb, seq, MIX_M), BF16),
                 jax.ShapeDtypeStruct((nb, N_DIR, M_HEADS, M_DQK, M_DV), F32),
                 jax.ShapeDtypeStruct((nb, N_UNITS, M_DQK), F32),
                 jax.ShapeDtypeStruct((nb, 1, LANES), F32)]
    scratch = [pltpu.VMEM((N_DIR, M_HEADS, M_DQK, M_DV), F32),
               pltpu.VMEM((N_UNITS, N_ROWS, M_DQK), F32),
               pltpu.VMEM((1, LANES), F32),
               pltpu.VMEM((seq, MIX_M), F32),
               pltpu.VMEM((N_DIR, CHUNK, MIX_M), F32)]
    return in_specs, args, out_specs, out_shape, scratch


def _ssd_phases(refs, nc, zero_init):
    (xf, bf, cf, zf, colf, rowf, xb, bb, cb, zb, colb, rowb, dskip, gn) = refs[:14]
    pos = 14
    if not zero_init:
        s0 = refs[pos]
        pos += 1
    ys_out, s_out, st_s, ypart, ybuf = refs[pos:]
    s = pl.program_id(1)
    half = nc // 2

    @pl.when(s == 0)
    def _():
        ypart[...] = jnp.zeros_like(ypart)
        if zero_init:
            st_s[...] = jnp.zeros_like(st_s)
        else:
            for d in range(N_DIR):
                for pr in range(S_PAIRS):
                    st_s[d, pr] = s0[d, pr].T

    yield _DONE
    lower, upper = _tri_masks()
    low_half = lax.broadcasted_iota(jnp.int32, (CHUNK, LANES), 1) < S_HEADDIM
    low_half1 = lax.broadcasted_iota(jnp.int32, (1, LANES), 1) < S_HEADDIM
    per_dir = ((xf, bf, cf, zf, colf, rowf, lower, s), (xb, bb, cb, zb, colb, rowb, upper, nc - 1 - s))
    st_new = []
    for d, (x_ref, b_ref, c_ref, z_ref, col_ref, row_ref, mask, chunk) in enumerate(per_dir):
        last = CHUNK - 1 if d == 0 else 0
        cs_c = col_ref[...]
        total = cs_c[last:last + 1, :]
        st_old = [st_s[d, pr] for pr in range(S_PAIRS)]
        for g in range(S_GROUPS):
            gs = slice(g * S_STATE, (g + 1) * S_STATE)
            cg = c_ref[:, gs].astype(BF16)
            b_f32 = b_ref[:, gs]
            cbm = _dot(cg, b_f32.astype(BF16), _NT)
            b_t = b_f32.T
            for pg in range(PAIRS_PER_GROUP):
                pr = g * PAIRS_PER_GROUP + pg
                cols = slice(pr * LANES, (pr + 1) * LANES)
                x16 = x_ref[:, cols].astype(BF16)
                zero16 = jnp.zeros_like(x16)
                halves = (jnp.where(low_half, x16, zero16), jnp.where(low_half, zero16, x16))
                state = st_old[pr]
                y_acc = jnp.zeros((CHUNK, LANES), F32)
                s_acc = jnp.zeros((S_STATE, LANES), F32)
                cs_b = []
                for e in range(PAIR):
                    gi = G_DT + d * S_HEADS + pr * PAIR + e
                    cs_b.append(_lane_bcast(cs_c, gi))
                    q_row = row_ref[gi:gi + 1, :]
                    wk_row = row_ref[gi + G_WK - G_DT:gi + G_WK - G_DT + 1, :]
                    mix = cbm * jnp.exp(jnp.where(mask, cs_b[e] - q_row, -jnp.inf))
                    y_acc = y_acc + _dot(mix.astype(BF16), halves[e])
                    s_acc = s_acc + _dot((b_t * wk_row).astype(BF16), halves[e])
                carry = jnp.exp(jnp.where(low_half, cs_b[0], cs_b[1]))
                ybuf[d, :, cols] = y_acc + _dot(cg, state.astype(BF16)) * carry
                gi0 = G_DT + d * S_HEADS + pr * PAIR
                dec = jnp.exp(jnp.where(low_half1, jnp.broadcast_to(total[:, gi0:gi0 + 1], (1, LANES)),
                                        jnp.broadcast_to(total[:, gi0 + 1:gi0 + 2], (1, LANES))))
                st_new.append(state * dec + s_acc)
            yield _WORK
    for d in range(N_DIR):
        for pr in range(S_PAIRS):
            st_s[d, pr] = st_new[d * S_PAIRS + pr]

    yield _DONE
    for d, (x_ref, _, _, z_ref, _, _, _, chunk) in enumerate(per_dir):
        rows = pl.ds(pl.multiple_of(chunk * CHUNK, CHUNK), CHUNK)
        y = ybuf[d] + ypart[rows, :] + dskip[...] * x_ref[...]
        ys_out[rows, :] = (_rms(y * z_ref[...]) * gn[...]).astype(BF16)
        ypart[rows, :] = ybuf[d]

    @pl.when(s == nc - 1)
    def _():
        for d in range(N_DIR):
            for pr in range(S_PAIRS):
                s_out[d, pr] = st_s[d, pr].T

    yield _DONE


def _ssd_specs(p, xbc, xbc_col, scans, dskip, g_norm, state, nb, nc):
    zero_init = state is None
    seq = nc * CHUNK
    bc_w = S_GROUPS * S_STATE
    assert xbc_col % MIX_S == 0
    col, _, row = scans

    def fwd(cb):
        return lambda b, s: (b * nc + s, cb)

    def bwd(cb):
        return lambda b, s: (b * nc + nc - 1 - s, cb)

    def stream(mk):
        return [pl.BlockSpec((CHUNK, MIX_S), mk(xbc_col // MIX_S)),
                pl.BlockSpec((CHUNK, bc_w), mk((xbc_col + MIX_S) // bc_w)),
                pl.BlockSpec((CHUNK, bc_w), mk((xbc_col + MIX_S) // bc_w + 1)),
                pl.BlockSpec((CHUNK, MIX_S), mk(P_Z // MIX_S)),
                pl.BlockSpec((CHUNK, GATE_W), mk(0)),
                pl.BlockSpec((CHUNK, GATE_W), mk(0))]

    const = lambda b, s: (0, 0)
    in_specs = stream(fwd) + stream(bwd) + [pl.BlockSpec((1, MIX_S), const), pl.BlockSpec((1, MIX_S), const)]
    args = [xbc, xbc, xbc, p, col, row] * 2 + [dskip, g_norm]
    state_spec = pl.BlockSpec((None, N_DIR, S_PAIRS, LANES, S_STATE), lambda b, s: (b, 0, 0, 0, 0))
    if not zero_init:
        in_specs.append(state_spec)
        args.append(state)
    out_specs = [_sequence_out_spec(seq, MIX_S), state_spec]
    out_shape = [jax.ShapeDtypeStruct((nb, seq, MIX_S), BF16),
                 jax.ShapeDtypeStruct((nb, N_DIR, S_PAIRS, LANES, S_STATE), F32)]
    scratch = [pltpu.VMEM((N_DIR, S_PAIRS, S_STATE, LANES), F32),
               pltpu.VMEM((seq, MIX_S), F32),
               pltpu.VMEM((N_DIR, CHUNK, MIX_S), F32)]
    return in_specs, args, out_specs, out_shape, scratch


def _mixer_kernel(*refs, nc, zero_init, counts):
    (m_in, s_in), (m_out, s_out), (m_scr, s_scr) = counts
    pos = 0
    parts = []
    for n in (m_in, s_in, m_out, s_out, m_scr, s_scr):
        parts.append(refs[pos:pos + n])
        pos += n
    gens = (_mlstm_phases(parts[0] + parts[2] + parts[4], nc, zero_init),
            _ssd_phases(parts[1] + parts[3] + parts[5], nc, zero_init))
    for _ in range(3):
        live = list(gens)
        while live:
            live = [g for g in live if next(g) is not _DONE]


def _mixers(p, xbc, xbc_col, scans, g_mlstm, dskip, g_ssd, mlstm_state, ssd_state, nb, nc):
    m = _mlstm_specs(p, scans, g_mlstm, mlstm_state, nb, nc)
    s = _ssd_specs(p, xbc, xbc_col, scans, dskip, g_ssd, ssd_state, nb, nc)
    counts = tuple((len(a), len(b)) for a, b in ((m[0], s[0]), (m[2], s[2]), (m[4], s[4])))
    return pl.pallas_call(
        functools.partial(_mixer_kernel, nc=nc, zero_init=mlstm_state is None, counts=counts),
        grid=(nb, nc),
        in_specs=m[0] + s[0],
        out_specs=m[2] + s[2],
        out_shape=m[3] + s[3],
        scratch_shapes=m[4] + s[4],
        compiler_params=_params(("parallel", "arbitrary"), VMEM_LIMIT_LARGE),
        name="mixers",
    )(*(m[1] + s[1]))


OUT_TM = 512


OUT_SUB = 256


def _outproj_kernel(hm_ref, ys_ref, w_ref, x_ref, g1_ref, gpost_ref, gpre_ref, sc_ref, sh_ref, x1_ref, u2_ref):
    for r in range(OUT_TM // OUT_SUB):
        rows = slice(r * OUT_SUB, (r + 1) * OUT_SUB)
        mix = _dot(jnp.concatenate([hm_ref[rows, :], ys_ref[rows, :]], axis=1), w_ref[...])
        x1 = x_ref[rows, :] + g1_ref[...] * (_rms(mix) * gpost_ref[...])
        x1_ref[rows, :] = x1
        u2_ref[rows, :] = (_rms(x1) * gpre_ref[...] * (1.0 + sc_ref[...]) + sh_ref[...]).astype(BF16)


def _out_proj(hm, ys, w_out16, x2d, g1, gpost, gpre, sc2, sh2, rows_per_mod):
    t = x2d.shape[0]
    tiles_per_mod = rows_per_mod // OUT_TM
    row = lambda i: (i, 0)
    const = lambda i: (0, 0)
    mod = pl.BlockSpec((None, 1, D_MODEL), lambda i: (i // tiles_per_mod, 0, 0))
    return pl.pallas_call(
        _outproj_kernel,
        grid=(t // OUT_TM,),
        in_specs=[pl.BlockSpec((OUT_TM, MIX_M), row), pl.BlockSpec((OUT_TM, MIX_S), row),
                  pl.BlockSpec((D_MODEL, D_MODEL), const), pl.BlockSpec((OUT_TM, D_MODEL), row),
                  mod, pl.BlockSpec((1, D_MODEL), const), pl.BlockSpec((1, D_MODEL), const), mod, mod],
        out_specs=[pl.BlockSpec((OUT_TM, D_MODEL), row), pl.BlockSpec((OUT_TM, D_MODEL), row)],
        out_shape=[jax.ShapeDtypeStruct((t, D_MODEL), F32), jax.ShapeDtypeStruct((t, D_MODEL), BF16)],
        compiler_params=_params(("parallel",)),
        name="out_proj",
    )(hm, ys, w_out16, x2d, g1, gpost, gpre, sc2, sh2)


MLP_TM = 1024
MLP_TH = 512
MLP_SUB = 512


def _mlp_kernel(u_ref, w1_ref, w2_ref, x1_hbm, g2_ref, gpost_ref, o_ref, x1_buf, x1_sem):
    i = pl.program_id(0)
    j = pl.program_id(1)

    def x1_copy():
        rows = pl.ds(pl.multiple_of(i * MLP_TM, MLP_TM), MLP_TM)
        return pltpu.make_async_copy(x1_hbm.at[rows, :], x1_buf, x1_sem)

    def partial_sum(rows, w1, w2):
        hid = jnp.square(jnp.maximum(_dot(u_ref[rows, :], w1), 0.0)).astype(BF16)
        return _dot(hid, w2)

    sub_blocks = [slice(r * MLP_SUB, (r + 1) * MLP_SUB) for r in range(MLP_TM // MLP_SUB)]
    last = pl.num_programs(1) - 1

    @pl.when(j == 0)
    def _():
        x1_copy().start()
        w1, w2 = w1_ref[...].astype(BF16), w2_ref[...].astype(BF16)
        for rows in sub_blocks:
            o_ref[rows, :] = partial_sum(rows, w1, w2)

    @pl.when((j > 0) & (j < last))
    def _():
        w1, w2 = w1_ref[...].astype(BF16), w2_ref[...].astype(BF16)
        for rows in sub_blocks:
            o_ref[rows, :] += partial_sum(rows, w1, w2)

    @pl.when(j == last)
    def _():
        x1_copy().wait()
        w1, w2 = w1_ref[...].astype(BF16), w2_ref[...].astype(BF16)
        for rows in sub_blocks:
            total = o_ref[rows, :] + partial_sum(rows, w1, w2)
            o_ref[rows, :] = x1_buf[rows, :] + g2_ref[...] * (_rms(total) * gpost_ref[...])


def _mlp(u2, w1, w2, x1, g2, gpost, rows_per_mod):
    t = u2.shape[0]
    tiles_per_mod = rows_per_mod // MLP_TM
    assert D_FF // MLP_TH >= 2
    return pl.pallas_call(
        _mlp_kernel,
        grid=(t // MLP_TM, D_FF // MLP_TH),
        in_specs=[pl.BlockSpec((MLP_TM, D_MODEL), lambda i, j: (i, 0)),
                  pl.BlockSpec((D_MODEL, MLP_TH), lambda i, j: (0, j)),
                  pl.BlockSpec((MLP_TH, D_MODEL), lambda i, j: (j, 0)),
                  pl.BlockSpec(memory_space=pl.ANY),
                  pl.BlockSpec((None, 1, D_MODEL), lambda i, j: (i // tiles_per_mod, 0, 0)),
                  pl.BlockSpec((1, D_MODEL), lambda i, j: (0, 0))],
        out_specs=pl.BlockSpec((MLP_TM, D_MODEL), lambda i, j: (i, 0)),
        out_shape=jax.ShapeDtypeStruct((t, D_MODEL), F32),
        scratch_shapes=[pltpu.VMEM((MLP_TM, D_MODEL), F32), pltpu.SemaphoreType.DMA(())],
        compiler_params=_params(("parallel", "arbitrary"), VMEM_LIMIT_LARGE),
        name="mlp",
    )(u2, w1, w2, x1, g2, gpost)


def _gate_row(i_vals, f_vals, dt_vals):
    v = jnp.concatenate([i_vals.reshape(-1), f_vals.reshape(-1), dt_vals.reshape(-1)]).astype(F32)
    return jnp.pad(v, (0, GATE_W - N_GATES)).reshape(1, GATE_W)


def _block(x, mods, state, weights, width):
    nb, seq, _ = x.shape
    nc = seq // CHUNK
    t = nb * seq
    x2d = x.reshape(t, D_MODEL)
    sh1, sc1, g1, sh2, sc2, g2 = mods
    rows_per_mod = t // sh1.shape[0]
    w = weights

    fused_conv = width == seq and IN_TM % seq == 0
    conv = (w["conv_w9"], w["conv_b"], width) if fused_conv else None
    side = w["w_out"] if w["w_out"].dtype == F32 else None
    p, gc, *cast = _in_proj(x2d, sc1, sh1, w["g_pre_mix"], w["w_t"], w["wg"], rows_per_mod, conv, side)
    w_out16 = cast[0] if cast else w["w_out"]
    scans = _gate_scans(gc, w["gate_bias"], w["gate_alog"])
    if fused_conv:
        xbc, xbc_col = p, P_XBC
    else:
        xbc, xbc_col = _conv(p, w["conv_w9"], w["conv_b"], nb, seq, width, CONV_BLOCK_ELEMS // seq), 0

    if state is None:
        m_state = s_state = None
    else:
        c0, n0, m0, s0 = state
        m_state = (c0, n0.reshape(nb, N_UNITS, M_DQK),
                   jnp.pad(m0.reshape(nb, 1, N_UNITS), ((0, 0), (0, 0), (0, LANES - N_UNITS))))
        s_state = s0.reshape(nb, N_DIR, S_PAIRS, LANES, S_STATE)
    hm, c_new, n_new, m_new, ys, s_new = _mixers(p, xbc, xbc_col, scans, w["g_mlstm_norm"], w["dskip"],
                                                 w["g_ssd_norm"], m_state, s_state, nb, nc)

    x1, u2 = _out_proj(hm.reshape(t, MIX_M), ys.reshape(t, MIX_S), w_out16, x2d, g1,
                       w["g_post_mix"], w["g_pre_mlp"], sc2, sh2, rows_per_mod)
    y = _mlp(u2, w["w_mlp_in"], w["w_mlp_out"], x1, g2, w["g_post_mlp"], rows_per_mod)
    new_state = (c_new.reshape(nb, 1, N_DIR, M_HEADS, M_DQK, M_DV),
                 n_new.reshape(nb, 1, N_DIR, M_HEADS, M_DQK),
                 m_new[:, 0, :N_UNITS].reshape(nb, 1, N_DIR, M_HEADS),
                 s_new.reshape(nb, 1, N_DIR, S_HEADS, S_HEADDIM, S_STATE))
    return y.reshape(nb, seq, D_MODEL), new_state, w_out16


def kernel(x_prompt, x_sample, state_mlstm_c, state_mlstm_n, state_mlstm_m, state_ssd, c, c_ctx, w_mod, b_mod,
           g_pre_mix, g_post_mix, w_in, b_igate, b_fgate, conv_w, conv_b, dt_bias, a_log, d_skip, g_mlstm_norm,
           g_ssd_norm, w_out, g_pre_mlp, g_post_mlp, w_mlp_in, w_mlp_out):
    assert w_mod.shape[0] == 1, "one layer"
    nb_s = x_sample.shape[0]

    cond8 = jnp.zeros((8, D_MODEL), F32).at[0].set(c_ctx).at[1:1 + nb_s].set(c)
    mod = _modulation(cond8, w_mod[0], b_mod[0].reshape(1, -1))
    mods = [mod[:, k * D_MODEL:(k + 1) * D_MODEL] for k in range(6)]
    mods_p = [m[0:1].reshape(1, 1, D_MODEL) for m in mods]
    mods_s = [m[1:1 + nb_s].reshape(nb_s, 1, D_MODEL) for m in mods]

    w_t = w_in[0].T.astype(BF16)
    assert w_t.shape[0] == P_MAIN + N_GATES
    gate_rows = jnp.concatenate([w_t[P_Z:P_Z + G_DT], w_t[P_MAIN + G_DT:]], axis=0)
    gate_rows = jnp.pad(gate_rows, ((0, GATE_W - N_GATES), (0, 0)))
    zeros_u = jnp.zeros((N_UNITS,), F32)
    row = lambda v: v.reshape(1, -1)
    weights = dict(
        w_t=w_t, wg=gate_rows,
        g_pre_mix=row(g_pre_mix[0]), g_post_mix=row(g_post_mix[0]),
        g_pre_mlp=row(g_pre_mlp[0]), g_post_mlp=row(g_post_mlp[0]),
        conv_w9=conv_w[0].reshape(9, XBC), conv_b=row(conv_b[0]),
        gate_bias=_gate_row(b_igate[0], b_fgate[0], dt_bias[0]),
        gate_alog=_gate_row(zeros_u, zeros_u, a_log[0]),
        dskip=row(jnp.repeat(d_skip[0], S_HEADDIM)),
        g_mlstm_norm=row(g_mlstm_norm[0]), g_ssd_norm=row(g_ssd_norm[0]),
        w_out=w_out[0], w_mlp_in=w_mlp_in[0], w_mlp_out=w_mlp_out[0])

    y_p, st, w_out16 = _block(x_prompt, mods_p, None, weights, x_prompt.shape[1])
    cache = (state_mlstm_c[:, 0], state_mlstm_n[:, 0], state_mlstm_m[:, 0], state_ssd[:, 0])
    y_s, _, _ = _block(x_sample, mods_s, cache, dict(weights, w_out=w_out16), GRID_W)
    return (y_p, y_s) + st
```

```python
import functools

import jax
import jax.numpy as jnp
from jax import lax
from jax.experimental import pallas as pl
from jax.experimental.pallas import tpu as pltpu

F32 = jnp.float32
BF16 = jnp.bfloat16

D_MODEL = 2048
CHUNK = 128
N_DIR = 2
M_HEADS = 4
M_DQK = 128
M_DV = 256
MIX_M = M_HEADS * M_DV
S_HEADS = 16
S_HEADDIM = 64
S_STATE = 128
S_GROUPS = 4
S_REP = S_HEADS // S_GROUPS
MIX_S = S_HEADS * S_HEADDIM
XBC = MIX_S + 2 * S_GROUPS * S_STATE
D_FF = 4 * D_MODEL
GRID_W = 64
EPS = 1e-6
LANES = 128

P_Q = 0
P_K = M_HEADS * M_DQK
P_V = 2 * M_HEADS * M_DQK
P_O = P_V + MIX_M
P_Z = P_O + MIX_M
P_XBC = P_Z + MIX_S
P_MAIN = P_XBC + XBC
GATE_W = LANES
N_UNITS = N_DIR * M_HEADS
G_I = 0
G_F = N_UNITS
G_DT = 2 * N_UNITS
N_GATES = G_DT + N_DIR * S_HEADS
G_WK = G_DT + N_DIR * S_HEADS
G_CU = G_WK + N_DIR * S_HEADS
G_B = G_CU + N_UNITS
assert G_B + N_UNITS <= GATE_W

PAIR = LANES // S_HEADDIM
S_PAIRS = S_HEADS // PAIR
PAIRS_PER_GROUP = S_REP // PAIR

MIB = 1024 * 1024
VMEM_LIMIT = 48 * MIB
VMEM_LIMIT_LARGE = 60 * MIB

_NT = (((1,), (1,)), ((), ()))


def _params(sem, limit=VMEM_LIMIT):
    return pltpu.CompilerParams(dimension_semantics=sem, vmem_limit_bytes=limit)


def _silu(x):
    return x / (1.0 + jnp.exp(-x))


def _sigmoid(x):
    return 1.0 / (1.0 + jnp.exp(-x))


def _rms(x):
    return x * lax.rsqrt(jnp.mean(x * x, axis=-1, keepdims=True) + EPS)


def _dot(a, b, dims=None, precision=None):
    if dims is None:
        dims = (((a.ndim - 1,), (0,)), ((), ()))
    return lax.dot_general(a, b, dims, precision=precision, preferred_element_type=F32)


def _lane_bcast(tile, lane):
    return jnp.broadcast_to(tile[:, lane:lane + 1], tile.shape)


def _mod_kernel(c_ref, w_ref, b_ref, o_ref):
    a = _silu(c_ref[...]).astype(BF16)
    o_ref[...] = _dot(a, w_ref[...].astype(BF16)) + b_ref[...]


def _modulation(cond8, w_mod, b_mod):
    n = w_mod.shape[1]
    tn = 1024
    return pl.pallas_call(
        _mod_kernel,
        grid=(n // tn,),
        in_specs=[pl.BlockSpec((8, D_MODEL), lambda j: (0, 0)),
                  pl.BlockSpec((D_MODEL, tn), lambda j: (0, j)),
                  pl.BlockSpec((1, tn), lambda j: (0, j))],
        out_specs=pl.BlockSpec((8, tn), lambda j: (0, j)),
        out_shape=jax.ShapeDtypeStruct((8, n), F32),
        compiler_params=_params(("parallel",)),
        name="modulation",
    )(cond8, w_mod, b_mod)


IN_TM = 1024
IN_TN = 1024
IN_SUB = 256


def _grid_conv(x, w_ref, b_ref, seq, width):
    n = x.shape[0]
    assert n % seq == 0 and seq % width == 0 and seq & (seq - 1) == 0 and width & (width - 1) == 0
    t = lax.broadcasted_iota(jnp.int32, x.shape, 0)
    c = jnp.bitwise_and(t, width - 1)
    xl = jnp.where(c >= 1, pltpu.roll(x, 1, 0), 0.0)
    xr = jnp.where(c <= width - 2, pltpu.roll(x, n - 1, 0), 0.0)

    def taps(di):
        return w_ref[3 * di:3 * di + 1, :] * xl + w_ref[3 * di + 1:3 * di + 2, :] * x \
            + w_ref[3 * di + 2:3 * di + 3, :] * xr

    out = taps(1) + b_ref[...]
    if seq > width:
        r = jnp.bitwise_and(t, seq - 1)
        out = out + jnp.where(r >= width, pltpu.roll(taps(0), width, 0), 0.0)
        out = out + jnp.where(r < seq - width, pltpu.roll(taps(2), n - width, 0), 0.0)
    return _silu(out)


def _inproj_kernel(x_ref, sc_ref, sh_ref, g_ref, w_ref, wg_ref, *rest, conv_width):
    if conv_width is None:
        o_ref, gc_ref, u_ref = rest
    else:
        cw_ref, cb_ref, o_ref, gc_ref, u_ref = rest
    j = pl.program_id(1)

    j_o, j_z = P_O // IN_TN, P_Z // IN_TN
    j_x = P_XBC // IN_TN
    plain = (j > 0) & (j != j_o) & (j != j_z)
    if conv_width is not None:
        plain = plain & (j < j_x)

        @pl.when(j >= j_x)
        def _():
            o_ref[...] = _grid_conv(_dot(u_ref[...], w_ref[...], _NT), cw_ref, cb_ref, conv_width, conv_width)

    @pl.when(j == 0)
    def _():
        for r in range(IN_TM // IN_SUB):
            rows = slice(r * IN_SUB, (r + 1) * IN_SUB)
            y = _rms(x_ref[rows, :]) * g_ref[...]
            u = (y * (1.0 + sc_ref[...]) + sh_ref[...]).astype(BF16)
            u_ref[rows, :] = u
            o_ref[rows, :] = _dot(u, w_ref[...], _NT)

    @pl.when(plain)
    def _():
        o_ref[...] = _dot(u_ref[...], w_ref[...], _NT)

    @pl.when((j == j_o) | (j == j_z))
    def _():
        acc = _dot(u_ref[...], w_ref[...], _NT)
        sig = _sigmoid(acc)
        o_ref[...] = jnp.where(j == j_o, sig, acc * sig)

    @pl.when(j == pl.num_programs(1) - 1)
    def _():
        gc_ref[...] = _dot(u_ref[...], wg_ref[...], _NT)


def _in_proj(x2d, sc, sh, g, w_t, wg, rows_per_mod, conv=None):
    t = x2d.shape[0]
    tiles_per_mod = rows_per_mod // IN_TM
    n_a = P_Z // IN_TN
    j_x = P_XBC // IN_TN
    assert P_O % IN_TN == 0 and MIX_M == IN_TN and MIX_S == IN_TN

    def first_row(i, j):
        return (j * (IN_TN // G_DT) + (j >= n_a).astype(jnp.int32)) * G_DT, 0

    conv_specs, conv_args, conv_width = [], [], None
    if conv is not None:
        conv_w9, conv_b, conv_width = conv
        assert IN_TM % conv_width == 0 and P_XBC % IN_TN == 0
        conv_specs = [pl.BlockSpec((9, IN_TN), lambda i, j: (0, jnp.maximum(j - j_x, 0))),
                      pl.BlockSpec((1, IN_TN), lambda i, j: (0, jnp.maximum(j - j_x, 0)))]
        conv_args = [conv_w9, conv_b]
    return pl.pallas_call(
        functools.partial(_inproj_kernel, conv_width=conv_width),
        grid=(t // IN_TM, P_MAIN // IN_TN),
        in_specs=[pl.BlockSpec((IN_TM, D_MODEL), lambda i, j: (i, 0)),
                  pl.BlockSpec((None, 1, D_MODEL), lambda i, j: (i // tiles_per_mod, 0, 0)),
                  pl.BlockSpec((None, 1, D_MODEL), lambda i, j: (i // tiles_per_mod, 0, 0)),
                  pl.BlockSpec((1, D_MODEL), lambda i, j: (0, 0)),
                  pl.BlockSpec((pl.Element(IN_TN), pl.Element(D_MODEL)), first_row),
                  pl.BlockSpec((GATE_W, D_MODEL), lambda i, j: (0, 0))] + conv_specs,
        out_specs=[pl.BlockSpec((IN_TM, IN_TN), lambda i, j: (i, j)),
                   pl.BlockSpec((IN_TM, GATE_W), lambda i, j: (i, 0))],
        out_shape=[jax.ShapeDtypeStruct((t, P_MAIN), F32),
                   jax.ShapeDtypeStruct((t, GATE_W), F32)],
        scratch_shapes=[pltpu.VMEM((IN_TM, D_MODEL), BF16)],
        compiler_params=_params(("parallel", "arbitrary"), VMEM_LIMIT_LARGE),
        name="in_proj",
    )(x2d, sc, sh, g, w_t, wg, *conv_args)


def _conv_kernel(x_ref, w_ref, b_ref, o_ref, *, seq, width):
    o_ref[...] = _grid_conv(x_ref[...], w_ref, b_ref, seq, width)


CONV_BLOCK_ELEMS = 2 * MIB // 4


def _conv(p, conv_w9, conv_b, nb, seq, width, cn):
    cn = min(cn, XBC)
    col0 = P_XBC // cn
    return pl.pallas_call(
        functools.partial(_conv_kernel, seq=seq, width=width),
        grid=(nb, XBC // cn),
        in_specs=[pl.BlockSpec((seq, cn), lambda b, j: (b, col0 + j)),
                  pl.BlockSpec((9, cn), lambda b, j: (0, j)),
                  pl.BlockSpec((1, cn), lambda b, j: (0, j))],
        out_specs=pl.BlockSpec((seq, cn), lambda b, j: (b, j)),
        out_shape=jax.ShapeDtypeStruct((nb * seq, XBC), F32),
        compiler_params=_params(("parallel", "parallel")),
        name="grid_conv",
    )(p, conv_w9, conv_b)


def _tri_masks():
    r = lax.broadcasted_iota(jnp.int32, (CHUNK, CHUNK), 0)
    c = lax.broadcasted_iota(jnp.int32, (CHUNK, CHUNK), 1)
    return r >= c, r <= c


SCAN_CHUNKS = 8


def _scan_kernel(gc_ref, bias_ref, alog_ref, col_ref, bcol_ref, row_ref):
    lower, upper = _tri_masks()
    lo, up = lower.astype(F32), upper.astype(F32)
    hi = lax.Precision.HIGHEST
    lane = lax.broadcasted_iota(jnp.int32, (CHUNK, GATE_W), 1)
    time = lax.broadcasted_iota(jnp.int32, (CHUNK, GATE_W), 0)
    lane1 = lax.broadcasted_iota(jnp.int32, (1, GATE_W), 1)

    def backward(l):
        unit_bwd = (l < G_DT) & (jnp.bitwise_and(l, N_UNITS - 1) >= M_HEADS)
        return unit_bwd | ((l >= G_DT + S_HEADS) & (l < N_GATES))

    is_bwd, is_bwd1 = backward(lane), backward(lane1)
    is_i = lane < G_F
    is_f = (lane >= G_F) & (lane < G_DT)
    is_dt = (lane >= G_DT) & (lane < N_GATES)
    neg_a = -jnp.exp(alog_ref[...])
    for c in range(SCAN_CHUNKS):
        rows = slice(c * CHUNK, (c + 1) * CHUNK)
        g = gc_ref[rows, :] + bias_ref[...]
        soft = jnp.log1p(jnp.exp(-jnp.abs(g)))
        logf = jnp.minimum(g, 0.0) - soft
        dt = jnp.maximum(g, 0.0) + soft
        x = jnp.where(is_f, logf, jnp.where(is_dt, dt * neg_a, 0.0))
        cs = jnp.where(is_bwd, _dot(up, x, precision=hi), _dot(lo, x, precision=hi))
        b_units = pltpu.roll(cs, GATE_W - G_F, 1)
        u = g - b_units
        cu = u
        k = 1
        while k < CHUNK:
            prev = jnp.where(time >= k, pltpu.roll(cu, k, 0), -jnp.inf)
            nxt = jnp.where(time < CHUNK - k, pltpu.roll(cu, CHUNK - k, 0), -jnp.inf)
            cu = jnp.maximum(cu, jnp.where(is_bwd, nxt, prev))
            k *= 2
        total = jnp.where(is_bwd1, cs[0:1, :], cs[CHUNK - 1:CHUNK, :])
        wk = jnp.exp(total - cs) * dt
        q = cs - jnp.log(dt)
        col_ref[rows, :] = jnp.where(is_i, cu, cs)
        bcol_ref[rows, :] = b_units
        tail = jnp.where(lane < G_CU, pltpu.roll(wk, G_WK - G_DT, 1),
                         jnp.where(lane < G_B, pltpu.roll(cu, G_CU, 1), pltpu.roll(b_units, G_B, 1)))
        row_ref[rows, :] = jnp.where(is_i, u, jnp.where(is_dt, q, tail)).T


def _gate_scans(gc, bias, alog):
    t = gc.shape[0]
    tm = SCAN_CHUNKS * CHUNK
    row = lambda i: (i, 0)
    const = lambda i: (0, 0)
    return pl.pallas_call(
        _scan_kernel,
        grid=(t // tm,),
        in_specs=[pl.BlockSpec((tm, GATE_W), row), pl.BlockSpec((1, GATE_W), const),
                  pl.BlockSpec((1, GATE_W), const)],
        out_specs=[pl.BlockSpec((tm, GATE_W), row)] * 3,
        out_shape=[jax.ShapeDtypeStruct((t, GATE_W), F32)] * 3,
        compiler_params=_params(("parallel",)),
        name="gate_scans",
    )(gc, bias, alog)


N_ROWS = 16
_WORK, _DONE = "work", "done"


def _mlstm_phases(refs, nc, zero_init):
    (qf, kf, vf, of, colf, bcolf, rowf, qb, kb, vb, ob, colb, bcolb, rowb, gn) = refs[:15]
    pos = 15
    if not zero_init:
        c0, n0, m0 = refs[pos:pos + 3]
        pos += 3
    hm_out, c_out, n_out, m_out, c_s, n_s, m_s, hpart, hbuf = refs[pos:]
    s = pl.program_id(1)
    half = nc // 2

    @pl.when(s == 0)
    def _():
        hpart[...] = jnp.zeros_like(hpart)
        if zero_init:
            c_s[...] = jnp.zeros_like(c_s)
            n_s[...] = jnp.zeros_like(n_s)
            m_s[...] = jnp.zeros_like(m_s)
        else:
            c_s[...] = c0[...]
            m_s[...] = m0[...]
            for u in range(N_UNITS):
                n_s[u] = jnp.broadcast_to(n0[u:u + 1, :], (N_ROWS, M_DQK))

    yield _DONE
    lower, upper = _tri_masks()
    lane1 = lax.broadcasted_iota(jnp.int32, (1, LANES), 1)
    ones_rows = jnp.ones((N_ROWS, CHUNK), BF16)
    per_dir = ((qf, kf, vf, of, colf, bcolf, rowf, lower, s),
               (qb, kb, vb, ob, colb, bcolb, rowb, upper, nc - 1 - s))
    m_prev = m_s[...]
    m_next = m_prev
    c_new, n_new = [], []
    for d, (q_ref, k_ref, v_ref, o_ref, col_ref, bcol_ref, row_ref, mask, chunk) in enumerate(per_dir):
        last = CHUNK - 1 if d == 0 else 0
        stab = jnp.maximum(m_prev, col_ref[...])
        w_c = jnp.exp(m_prev - stab)
        stab_last = stab[last:last + 1, :]
        m_new = bcol_ref[last:last + 1, :] + stab_last
        decay = w_c[last:last + 1, :]
        c_old = [c_s[d, h] for h in range(M_HEADS)]
        n_old = [n_s[d * M_HEADS + h] for h in range(M_HEADS)]
        for h in range(M_HEADS):
            u = d * M_HEADS + h
            stab_b = _lane_bcast(stab, u)
            u_row = row_ref[u:u + 1, :]
            p = jnp.exp(jnp.where(mask, u_row - stab_b, -jnp.inf))
            qh = q_ref[:, h * M_DQK:(h + 1) * M_DQK] * (M_DQK ** -0.5)
            k_t = k_ref[:, h * M_DQK:(h + 1) * M_DQK].T
            v16 = v_ref[:, h * M_DV:(h + 1) * M_DV].astype(BF16)
            sc = _dot(qh.astype(BF16), k_t.astype(BF16)) * p
            lhs = jnp.concatenate([sc.astype(BF16), (qh * _lane_bcast(w_c, u)).astype(BF16)], axis=1)
            num = _dot(lhs, jnp.concatenate([v16, c_old[h].astype(BF16)], axis=0))
            den = _dot(jnp.concatenate([ones_rows, n_old[h].astype(BF16)], axis=1), lhs, _NT)[0:1, :]
            m_t = row_ref[G_B + u:G_B + u + 1, :] + jnp.maximum(m_prev[:, u:u + 1],
                                                                row_ref[G_CU + u:G_CU + u + 1, :])
            inv_row = 1.0 / jnp.maximum(jnp.abs(den), jnp.exp(-m_t))
            inv = jnp.broadcast_to(inv_row, (CHUNK, LANES)).T
            hbuf[d, :, h * M_DV:(h + 1) * M_DV] = num * jnp.concatenate([inv, inv], axis=1)

            kw_t = (k_t * jnp.exp(u_row - stab_last[:, u:u + 1])).astype(BF16)
            dec = decay[:, u:u + 1]
            c_new.append(dec * c_old[h] + _dot(kw_t, v16))
            n_new.append(dec * n_old[h] + _dot(ones_rows, kw_t, _NT))
            yield _WORK
        mine = (lane1 >= d * M_HEADS) & (lane1 < (d + 1) * M_HEADS)
        m_next = jnp.where(mine, m_new, m_next)
    for u in range(N_UNITS):
        c_s[u // M_HEADS, u % M_HEADS] = c_new[u]
        n_s[u] = n_new[u]
    m_s[...] = m_next

    yield _DONE
    for d, (_, _, _, o_ref, _, _, _, _, chunk) in enumerate(per_dir):
        rows = pl.ds(pl.multiple_of(chunk * CHUNK, CHUNK), CHUNK)
        for h in range(M_HEADS):
            cols = slice(h * M_DV, (h + 1) * M_DV)
            tot = hbuf[d, :, cols] + hpart[rows, cols]
            hm_out[rows, cols] = (_rms(tot) * gn[:, cols] * o_ref[:, cols]).astype(BF16)
        hpart[rows, :] = hbuf[d]

    @pl.when(s == nc - 1)
    def _():
        c_out[...] = c_s[...]
        m_out[...] = m_s[...]
        for u in range(N_UNITS):
            n_out[u:u + 1, :] = n_s[u][0:1, :]

    yield _DONE


def _sequence_out_spec(seq, width):
    return pl.BlockSpec((None, seq, width), lambda b, s: (b, 0, 0))


def _mlstm_specs(p, scans, g_norm, state, nb, nc):
    zero_init = state is None
    seq = nc * CHUNK
    col, bcol, row = scans

    def fwd(cb):
        return lambda b, s: (b * nc + s, cb)

    def bwd(cb):
        return lambda b, s: (b * nc + nc - 1 - s, cb)

    def stream(mk):
        return [pl.BlockSpec((CHUNK, M_HEADS * M_DQK), mk(P_Q // (M_HEADS * M_DQK))),
                pl.BlockSpec((CHUNK, M_HEADS * M_DQK), mk(P_K // (M_HEADS * M_DQK))),
                pl.BlockSpec((CHUNK, MIX_M), mk(P_V // MIX_M)),
                pl.BlockSpec((CHUNK, MIX_M), mk(P_O // MIX_M)),
                pl.BlockSpec((CHUNK, GATE_W), mk(0)),
                pl.BlockSpec((CHUNK, GATE_W), mk(0)),
                pl.BlockSpec((CHUNK, GATE_W), mk(0))]

    in_specs = stream(fwd) + stream(bwd) + [pl.BlockSpec((1, MIX_M), lambda b, s: (0, 0))]
    args = [p] * 4 + [col, bcol, row] + [p] * 4 + [col, bcol, row, g_norm]
    state_specs = [pl.BlockSpec((None, N_DIR, M_HEADS, M_DQK, M_DV), lambda b, s: (b, 0, 0, 0, 0)),
                   pl.BlockSpec((None, N_UNITS, M_DQK), lambda b, s: (b, 0, 0)),
                   pl.BlockSpec((None, 1, LANES), lambda b, s: (b, 0, 0))]
    if not zero_init:
        in_specs += state_specs
        args += list(state)
    out_specs = [_sequence_out_spec(seq, MIX_M)] + state_specs
    out_shape = [jax.ShapeDtypeStruct((nb, seq, MIX_M), BF16),
                 jax.ShapeDtypeStruct((nb, N_DIR, M_HEADS, M_DQK, M_DV), F32),
                 jax.ShapeDtypeStruct((nb, N_UNITS, M_DQK), F32),
                 jax.ShapeDtypeStruct((nb, 1, LANES), F32)]
    scratch = [pltpu.VMEM((N_DIR, M_HEADS, M_DQK, M_DV), F32),
               pltpu.VMEM((N_UNITS, N_ROWS, M_DQK), F32),
               pltpu.VMEM((1, LANES), F32),
               pltpu.VMEM((seq, MIX_M), F32),
               pltpu.VMEM((N_DIR, CHUNK, MIX_M), F32)]
    return in_specs, args, out_specs, out_shape, scratch


def _ssd_phases(refs, nc, zero_init):
    (xf, bf, cf, zf, colf, rowf, xb, bb, cb, zb, colb, rowb, dskip, gn) = refs[:14]
    pos = 14
    if not zero_init:
        s0 = refs[pos]
        pos += 1
    ys_out, s_out, st_s, ypart, ybuf = refs[pos:]
    s = pl.program_id(1)
    half = nc // 2

    @pl.when(s == 0)
    def _():
        ypart[...] = jnp.zeros_like(ypart)
        if zero_init:
            st_s[...] = jnp.zeros_like(st_s)
        else:
            for d in range(N_DIR):
                for pr in range(S_PAIRS):
                    st_s[d, pr] = s0[d, pr].T

    yield _DONE
    lower, upper = _tri_masks()
    low_half = lax.broadcasted_iota(jnp.int32, (CHUNK, LANES), 1) < S_HEADDIM
    low_half1 = lax.broadcasted_iota(jnp.int32, (1, LANES), 1) < S_HEADDIM
    per_dir = ((xf, bf, cf, zf, colf, rowf, lower, s), (xb, bb, cb, zb, colb, rowb, upper, nc - 1 - s))
    st_new = []
    for d, (x_ref, b_ref, c_ref, z_ref, col_ref, row_ref, mask, chunk) in enumerate(per_dir):
        last = CHUNK - 1 if d == 0 else 0
        cs_c = col_ref[...]
        total = cs_c[last:last + 1, :]
        st_old = [st_s[d, pr] for pr in range(S_PAIRS)]
        for g in range(S_GROUPS):
            gs = slice(g * S_STATE, (g + 1) * S_STATE)
            cg = c_ref[:, gs].astype(BF16)
            b_f32 = b_ref[:, gs]
            cbm = _dot(cg, b_f32.astype(BF16), _NT)
            b_t = b_f32.T
            for pg in range(PAIRS_PER_GROUP):
                pr = g * PAIRS_PER_GROUP + pg
                cols = slice(pr * LANES, (pr + 1) * LANES)
                x16 = x_ref[:, cols].astype(BF16)
                zero16 = jnp.zeros_like(x16)
                halves = (jnp.where(low_half, x16, zero16), jnp.where(low_half, zero16, x16))
                state = st_old[pr]
                y_acc = jnp.zeros((CHUNK, LANES), F32)
                s_acc = jnp.zeros((S_STATE, LANES), F32)
                cs_b = []
                for e in range(PAIR):
                    gi = G_DT + d * S_HEADS + pr * PAIR + e
                    cs_b.append(_lane_bcast(cs_c, gi))
                    q_row = row_ref[gi:gi + 1, :]
                    wk_row = row_ref[gi + G_WK - G_DT:gi + G_WK - G_DT + 1, :]
                    mix = cbm * jnp.exp(jnp.where(mask, cs_b[e] - q_row, -jnp.inf))
                    y_acc = y_acc + _dot(mix.astype(BF16), halves[e])
                    s_acc = s_acc + _dot((b_t * wk_row).astype(BF16), halves[e])
                carry = jnp.exp(jnp.where(low_half, cs_b[0], cs_b[1]))
                ybuf[d, :, cols] = y_acc + _dot(cg, state.astype(BF16)) * carry
                gi0 = G_DT + d * S_HEADS + pr * PAIR
                dec = jnp.exp(jnp.where(low_half1, jnp.broadcast_to(total[:, gi0:gi0 + 1], (1, LANES)),
                                        jnp.broadcast_to(total[:, gi0 + 1:gi0 + 2], (1, LANES))))
                st_new.append(state * dec + s_acc)
            yield _WORK
    for d in range(N_DIR):
        for pr in range(S_PAIRS):
            st_s[d, pr] = st_new[d * S_PAIRS + pr]

    yield _DONE
    for d, (x_ref, _, _, z_ref, _, _, _, chunk) in enumerate(per_dir):
        rows = pl.ds(pl.multiple_of(chunk * CHUNK, CHUNK), CHUNK)
        y = ybuf[d] + ypart[rows, :] + dskip[...] * x_ref[...]
        ys_out[rows, :] = (_rms(y * z_ref[...]) * gn[...]).astype(BF16)
        ypart[rows, :] = ybuf[d]

    @pl.when(s == nc - 1)
    def _():
        for d in range(N_DIR):
            for pr in range(S_PAIRS):
                s_out[d, pr] = st_s[d, pr].T

    yield _DONE


def _ssd_specs(p, xbc, xbc_col, scans, dskip, g_norm, state, nb, nc):
    zero_init = state is None
    seq = nc * CHUNK
    bc_w = S_GROUPS * S_STATE
    assert xbc_col % MIX_S == 0
    col, _, row = scans

    def fwd(cb):
        return lambda b, s: (b * nc + s, cb)

    def bwd(cb):
        return lambda b, s: (b * nc + nc - 1 - s, cb)

    def stream(mk):
        return [pl.BlockSpec((CHUNK, MIX_S), mk(xbc_col // MIX_S)),
                pl.BlockSpec((CHUNK, bc_w), mk((xbc_col + MIX_S) // bc_w)),
                pl.BlockSpec((CHUNK, bc_w), mk((xbc_col + MIX_S) // bc_w + 1)),
                pl.BlockSpec((CHUNK, MIX_S), mk(P_Z // MIX_S)),
                pl.BlockSpec((CHUNK, GATE_W), mk(0)),
                pl.BlockSpec((CHUNK, GATE_W), mk(0))]

    const = lambda b, s: (0, 0)
    in_specs = stream(fwd) + stream(bwd) + [pl.BlockSpec((1, MIX_S), const), pl.BlockSpec((1, MIX_S), const)]
    args = [xbc, xbc, xbc, p, col, row] * 2 + [dskip, g_norm]
    state_spec = pl.BlockSpec((None, N_DIR, S_PAIRS, LANES, S_STATE), lambda b, s: (b, 0, 0, 0, 0))
    if not zero_init:
        in_specs.append(state_spec)
        args.append(state)
    out_specs = [_sequence_out_spec(seq, MIX_S), state_spec]
    out_shape = [jax.ShapeDtypeStruct((nb, seq, MIX_S), BF16),
                 jax.ShapeDtypeStruct((nb, N_DIR, S_PAIRS, LANES, S_STATE), F32)]
    scratch = [pltpu.VMEM((N_DIR, S_PAIRS, S_STATE, LANES), F32),
               pltpu.VMEM((seq, MIX_S), F32),
               pltpu.VMEM((N_DIR, CHUNK, MIX_S), F32)]
    return in_specs, args, out_specs, out_shape, scratch


def _mixer_kernel(*refs, nc, zero_init, counts):
    (m_in, s_in), (m_out, s_out), (m_scr, s_scr) = counts
    pos = 0
    parts = []
    for n in (m_in, s_in, m_out, s_out, m_scr, s_scr):
        parts.append(refs[pos:pos + n])
        pos += n
    gens = (_mlstm_phases(parts[0] + parts[2] + parts[4], nc, zero_init),
            _ssd_phases(parts[1] + parts[3] + parts[5], nc, zero_init))
    for _ in range(3):
        live = list(gens)
        while live:
            live = [g for g in live if next(g) is not _DONE]


def _mixers(p, xbc, xbc_col, scans, g_mlstm, dskip, g_ssd, mlstm_state, ssd_state, nb, nc):
    m = _mlstm_specs(p, scans, g_mlstm, mlstm_state, nb, nc)
    s = _ssd_specs(p, xbc, xbc_col, scans, dskip, g_ssd, ssd_state, nb, nc)
    counts = tuple((len(a), len(b)) for a, b in ((m[0], s[0]), (m[2], s[2]), (m[4], s[4])))
    return pl.pallas_call(
        functools.partial(_mixer_kernel, nc=nc, zero_init=mlstm_state is None, counts=counts),
        grid=(nb, nc),
        in_specs=m[0] + s[0],
        out_specs=m[2] + s[2],
        out_shape=m[3] + s[3],
        scratch_shapes=m[4] + s[4],
        compiler_params=_params(("parallel", "arbitrary"), VMEM_LIMIT_LARGE),
        name="mixers",
    )(*(m[1] + s[1]))


OUT_TM = 512


OUT_SUB = 256


def _outproj_kernel(hm_ref, ys_ref, w_ref, x_ref, g1_ref, gpost_ref, gpre_ref, sc_ref, sh_ref, x1_ref, u2_ref):
    for r in range(OUT_TM // OUT_SUB):
        rows = slice(r * OUT_SUB, (r + 1) * OUT_SUB)
        mix = _dot(jnp.concatenate([hm_ref[rows, :], ys_ref[rows, :]], axis=1), w_ref[...])
        x1 = x_ref[rows, :] + g1_ref[...] * (_rms(mix) * gpost_ref[...])
        x1_ref[rows, :] = x1
        u2_ref[rows, :] = (_rms(x1) * gpre_ref[...] * (1.0 + sc_ref[...]) + sh_ref[...]).astype(BF16)


def _out_proj(hm, ys, w_out16, x2d, g1, gpost, gpre, sc2, sh2, rows_per_mod):
    t = x2d.shape[0]
    tiles_per_mod = rows_per_mod // OUT_TM
    row = lambda i: (i, 0)
    const = lambda i: (0, 0)
    mod = pl.BlockSpec((None, 1, D_MODEL), lambda i: (i // tiles_per_mod, 0, 0))
    return pl.pallas_call(
        _outproj_kernel,
        grid=(t // OUT_TM,),
        in_specs=[pl.BlockSpec((OUT_TM, MIX_M), row), pl.BlockSpec((OUT_TM, MIX_S), row),
                  pl.BlockSpec((D_MODEL, D_MODEL), const), pl.BlockSpec((OUT_TM, D_MODEL), row),
                  mod, pl.BlockSpec((1, D_MODEL), const), pl.BlockSpec((1, D_MODEL), const), mod, mod],
        out_specs=[pl.BlockSpec((OUT_TM, D_MODEL), row), pl.BlockSpec((OUT_TM, D_MODEL), row)],
        out_shape=[jax.ShapeDtypeStruct((t, D_MODEL), F32), jax.ShapeDtypeStruct((t, D_MODEL), BF16)],
        compiler_params=_params(("parallel",)),
        name="out_proj",
    )(hm, ys, w_out16, x2d, g1, gpost, gpre, sc2, sh2)


MLP_TM = 1024
MLP_TH = 512
MLP_SUB = 512


def _mlp_kernel(u_ref, w1_ref, w2_ref, x1_hbm, g2_ref, gpost_ref, o_ref, x1_buf, x1_sem):
    i = pl.program_id(0)
    j = pl.program_id(1)

    def x1_copy():
        rows = pl.ds(pl.multiple_of(i * MLP_TM, MLP_TM), MLP_TM)
        return pltpu.make_async_copy(x1_hbm.at[rows, :], x1_buf, x1_sem)

    def partial_sum(rows, w1, w2):
        hid = jnp.square(jnp.maximum(_dot(u_ref[rows, :], w1), 0.0)).astype(BF16)
        return _dot(hid, w2)

    sub_blocks = [slice(r * MLP_SUB, (r + 1) * MLP_SUB) for r in range(MLP_TM // MLP_SUB)]
    last = pl.num_programs(1) - 1

    @pl.when(j == 0)
    def _():
        x1_copy().start()
        w1, w2 = w1_ref[...].astype(BF16), w2_ref[...].astype(BF16)
        for rows in sub_blocks:
            o_ref[rows, :] = partial_sum(rows, w1, w2)

    @pl.when((j > 0) & (j < last))
    def _():
        w1, w2 = w1_ref[...].astype(BF16), w2_ref[...].astype(BF16)
        for rows in sub_blocks:
            o_ref[rows, :] += partial_sum(rows, w1, w2)

    @pl.when(j == last)
    def _():
        x1_copy().wait()
        w1, w2 = w1_ref[...].astype(BF16), w2_ref[...].astype(BF16)
        for rows in sub_blocks:
            total = o_ref[rows, :] + partial_sum(rows, w1, w2)
            o_ref[rows, :] = x1_buf[rows, :] + g2_ref[...] * (_rms(total) * gpost_ref[...])


def _mlp(u2, w1, w2, x1, g2, gpost, rows_per_mod):
    t = u2.shape[0]
    tiles_per_mod = rows_per_mod // MLP_TM
    assert D_FF // MLP_TH >= 2
    return pl.pallas_call(
        _mlp_kernel,
        grid=(t // MLP_TM, D_FF // MLP_TH),
        in_specs=[pl.BlockSpec((MLP_TM, D_MODEL), lambda i, j: (i, 0)),
                  pl.BlockSpec((D_MODEL, MLP_TH), lambda i, j: (0, j)),
                  pl.BlockSpec((MLP_TH, D_MODEL), lambda i, j: (j, 0)),
                  pl.BlockSpec(memory_space=pl.ANY),
                  pl.BlockSpec((None, 1, D_MODEL), lambda i, j: (i // tiles_per_mod, 0, 0)),
                  pl.BlockSpec((1, D_MODEL), lambda i, j: (0, 0))],
        out_specs=pl.BlockSpec((MLP_TM, D_MODEL), lambda i, j: (i, 0)),
        out_shape=jax.ShapeDtypeStruct((t, D_MODEL), F32),
        scratch_shapes=[pltpu.VMEM((MLP_TM, D_MODEL), F32), pltpu.SemaphoreType.DMA(())],
        compiler_params=_params(("parallel", "arbitrary"), VMEM_LIMIT_LARGE),
        name="mlp",
    )(u2, w1, w2, x1, g2, gpost)


def _gate_row(i_vals, f_vals, dt_vals):
    v = jnp.concatenate([i_vals.reshape(-1), f_vals.reshape(-1), dt_vals.reshape(-1)]).astype(F32)
    return jnp.pad(v, (0, GATE_W - N_GATES)).reshape(1, GATE_W)


def _block(x, mods, state, weights, width):
    nb, seq, _ = x.shape
    nc = seq // CHUNK
    t = nb * seq
    x2d = x.reshape(t, D_MODEL)
    sh1, sc1, g1, sh2, sc2, g2 = mods
    rows_per_mod = t // sh1.shape[0]
    w = weights

    fused_conv = width == seq and IN_TM % seq == 0
    conv = (w["conv_w9"], w["conv_b"], width) if fused_conv else None
    p, gc = _in_proj(x2d, sc1, sh1, w["g_pre_mix"], w["w_t"], w["wg"], rows_per_mod, conv)
    scans = _gate_scans(gc, w["gate_bias"], w["gate_alog"])
    if fused_conv:
        xbc, xbc_col = p, P_XBC
    else:
        xbc, xbc_col = _conv(p, w["conv_w9"], w["conv_b"], nb, seq, width, CONV_BLOCK_ELEMS // seq), 0

    if state is None:
        m_state = s_state = None
    else:
        c0, n0, m0, s0 = state
        m_state = (c0, n0.reshape(nb, N_UNITS, M_DQK),
                   jnp.pad(m0.reshape(nb, 1, N_UNITS), ((0, 0), (0, 0), (0, LANES - N_UNITS))))
        s_state = s0.reshape(nb, N_DIR, S_PAIRS, LANES, S_STATE)
    hm, c_new, n_new, m_new, ys, s_new = _mixers(p, xbc, xbc_col, scans, w["g_mlstm_norm"], w["dskip"],
                                                 w["g_ssd_norm"], m_state, s_state, nb, nc)

    x1, u2 = _out_proj(hm.reshape(t, MIX_M), ys.reshape(t, MIX_S), w["w_out"], x2d, g1,
                       w["g_post_mix"], w["g_pre_mlp"], sc2, sh2, rows_per_mod)
    y = _mlp(u2, w["w_mlp_in"], w["w_mlp_out"], x1, g2, w["g_post_mlp"], rows_per_mod)
    new_state = (c_new.reshape(nb, 1, N_DIR, M_HEADS, M_DQK, M_DV),
                 n_new.reshape(nb, 1, N_DIR, M_HEADS, M_DQK),
                 m_new[:, 0, :N_UNITS].reshape(nb, 1, N_DIR, M_HEADS),
                 s_new.reshape(nb, 1, N_DIR, S_HEADS, S_HEADDIM, S_STATE))
    return y.reshape(nb, seq, D_MODEL), new_state


def kernel(x_prompt, x_sample, state_mlstm_c, state_mlstm_n, state_mlstm_m, state_ssd, c, c_ctx, w_mod, b_mod,
           g_pre_mix, g_post_mix, w_in, b_igate, b_fgate, conv_w, conv_b, dt_bias, a_log, d_skip, g_mlstm_norm,
           g_ssd_norm, w_out, g_pre_mlp, g_post_mlp, w_mlp_in, w_mlp_out):
    assert w_mod.shape[0] == 1, "one layer"
    nb_s = x_sample.shape[0]

    cond8 = jnp.zeros((8, D_MODEL), F32).at[0].set(c_ctx).at[1:1 + nb_s].set(c)
    mod = _modulation(cond8, w_mod[0], b_mod[0].reshape(1, -1))
    mods = [mod[:, k * D_MODEL:(k + 1) * D_MODEL] for k in range(6)]
    mods_p = [m[0:1].reshape(1, 1, D_MODEL) for m in mods]
    mods_s = [m[1:1 + nb_s].reshape(nb_s, 1, D_MODEL) for m in mods]

    w_t = w_in[0].T.astype(BF16)
    assert w_t.shape[0] == P_MAIN + N_GATES
    gate_rows = jnp.concatenate([w_t[P_Z:P_Z + G_DT], w_t[P_MAIN + G_DT:]], axis=0)
    gate_rows = jnp.pad(gate_rows, ((0, GATE_W - N_GATES), (0, 0)))
    zeros_u = jnp.zeros((N_UNITS,), F32)
    row = lambda v: v.reshape(1, -1)
    weights = dict(
        w_t=w_t, wg=gate_rows,
        g_pre_mix=row(g_pre_mix[0]), g_post_mix=row(g_post_mix[0]),
        g_pre_mlp=row(g_pre_mlp[0]), g_post_mlp=row(g_post_mlp[0]),
        conv_w9=conv_w[0].reshape(9, XBC), conv_b=row(conv_b[0]),
        gate_bias=_gate_row(b_igate[0], b_fgate[0], dt_bias[0]),
        gate_alog=_gate_row(zeros_u, zeros_u, a_log[0]),
        dskip=row(jnp.repeat(d_skip[0], S_HEADDIM)),
        g_mlstm_norm=row(g_mlstm_norm[0]), g_ssd_norm=row(g_ssd_norm[0]),
        w_out=w_out[0].astype(BF16), w_mlp_in=w_mlp_in[0], w_mlp_out=w_mlp_out[0])

    y_p, st = _block(x_prompt, mods_p, None, weights, x_prompt.shape[1])
    cache = (state_mlstm_c[:, 0], state_mlstm_n[:, 0], state_mlstm_m[:, 0], state_ssd[:, 0])
    y_s, _ = _block(x_sample, mods_s, cache, weights, GRID_W)
    return (y_p, y_s) + st
```
